```python
import jax, jax.numpy as jnp
from jax import lax
import numpy as np

D_MODEL = 1024
BATCH = 2
SEQ = 8192
DEPTH = 1
DEC_BATCH = 128
DEC_SEQ = 1
PAST_LEN = 2048
PAGE_SIZE = 128

HEAD_DIM = 64
ATT_WIDTH = D_MODEL // 2
N_HEADS = ATT_WIDTH // HEAD_DIM
POOL_WIDTH = D_MODEL - ATT_WIDTH
POOL_WINDOWS = (2, 4, 8, 16)
POOL_GROUPS = len(POOL_WINDOWS)
POOL_GC = POOL_WIDTH // POOL_GROUPS
POOL_STATE = max(POOL_WINDOWS) - 1
IN_WIDTH = POOL_WIDTH + 3 * ATT_WIDTH
MOBA_BLOCK = 256
MOBA_TOPK = 3
Q_BLOCK = 128
D_FF = 2816
CONV_WIDTH = 3
ROPE_THETA = 10000.0
RMS_EPS = 1e-6

kernel_name = "hymba_pool_moba_convffn_step"


def rmsnorm(x, g):
    xf = x.astype(jnp.float32)
    y = xf * lax.rsqrt(jnp.mean(xf * xf, axis=-1, keepdims=True) + RMS_EPS)
    return (y * g.astype(jnp.float32)).astype(x.dtype)


def modulate(h, shift, scale):
    return h * (1 + scale) + shift


def rope(x, positions):
    half = x.shape[-1] // 2
    inv = ROPE_THETA ** (-jnp.arange(half, dtype=jnp.float32) / half)
    ang = positions.astype(jnp.float32)[:, None] * inv[None, :]
    cos = jnp.cos(ang)[None, :, None, :]
    sin = jnp.sin(ang)[None, :, None, :]
    xf = x.astype(jnp.float32)
    x1, x2 = xf[..., :half], xf[..., half:]
    return jnp.concatenate([x1 * cos - x2 * sin, x2 * cos + x1 * sin], axis=-1).astype(x.dtype)


def pool_mixer(u, prev, pos0, pool_w, pool_scale):
    b, t_len, c = u.shape
    full = u if prev is None else jnp.concatenate([prev.astype(u.dtype), u], axis=1)
    p_len = full.shape[1] - t_len
    cs = jnp.pad(jnp.cumsum(full.astype(jnp.float32), axis=1), ((0, 0), (1, 0), (0, 0)))
    t = jnp.arange(t_len, dtype=jnp.int32)
    pos = pos0 + t
    uf = u.astype(jnp.float32)
    groups = []
    for g, w in enumerate(POOL_WINDOWS):
        lo, hi = g * POOL_GC, (g + 1) * POOL_GC
        cs_g = cs[:, :, lo:hi]
        start = jnp.maximum(p_len + 1 + t - w, 0)
        win_sum = cs_g[:, p_len + 1:] - cs_g[:, start]
        count = jnp.minimum(w, pos + 1).astype(jnp.float32)[:, None]
        groups.append(win_sum / count - uf[..., lo:hi])
    pooled = jnp.stack(groups, axis=2)
    out = jnp.einsum("btgc,gcd->btgd", pooled, pool_w.astype(jnp.float32)).reshape(b, t_len, c)
    out = (out * pool_scale.astype(jnp.float32)).astype(u.dtype)
    return out, full[:, -POOL_STATE:]


def moba_attend(q, q_pos, k_blocks, v_blocks, k_means):
    b, h, nq, dh = q.shape
    nb = k_blocks.shape[2]
    kk = min(MOBA_TOPK, nb)
    qf = q.astype(jnp.float32)
    own = q_pos // MOBA_BLOCK
    gate = jnp.einsum("bhqd,bhnd->bhqn", qf, k_means)
    fully_past = jnp.arange(nb)[None, :] < own[:, None]
    gate = jnp.where(fully_past, gate, -jnp.inf)
    _, sel = lax.top_k(gate, kk)
    sel_ok = sel < own[:, None]
    own_b = jnp.broadcast_to(own[:, None], (b, h, nq, 1)).astype(sel.dtype)
    blk = jnp.concatenate([sel, own_b], axis=-1)
    ok = jnp.concatenate([sel_ok, jnp.ones((b, h, nq, 1), dtype=bool)], axis=-1)
    bi = jnp.arange(b)[:, None, None, None]
    hi = jnp.arange(h)[None, :, None, None]
    kg = k_blocks[bi, hi, blk].astype(jnp.float32)
    vg = v_blocks[bi, hi, blk].astype(jnp.float32)
    s = jnp.einsum("bhqd,bhqnkd->bhqnk", qf, kg) * (dh ** -0.5)
    kpos = blk[..., None] * MOBA_BLOCK + jnp.arange(MOBA_BLOCK, dtype=blk.dtype)
    mask = ok[..., None] & (kpos <= q_pos[:, None, None])
    s = jnp.where(mask, s, -jnp.inf)
    p = jax.nn.softmax(s.reshape(b, h, nq, -1), axis=-1).reshape(s.shape)
    out = jnp.einsum("bhqnk,bhqnkd->bhqd", p, vg)
    return out.astype(q.dtype)


def moba_mixer(q, k, v, positions, past_k, past_v):
    b, t_len, h, dh = q.shape
    kf = k if past_k is None else jnp.concatenate([past_k.astype(k.dtype), k], axis=1)
    vf = v if past_v is None else jnp.concatenate([past_v.astype(v.dtype), v], axis=1)
    total = kf.shape[1]
    nb = -(-total // MOBA_BLOCK)
    pad = nb * MOBA_BLOCK - total
    kb = jnp.pad(kf, ((0, 0), (0, pad), (0, 0), (0, 0))).reshape(b, nb, MOBA_BLOCK, h, dh).transpose(0, 3, 1, 2, 4)
    vb = jnp.pad(vf, ((0, 0), (0, pad), (0, 0), (0, 0))).reshape(b, nb, MOBA_BLOCK, h, dh).transpose(0, 3, 1, 2, 4)
    k_means = jnp.mean(kb.astype(jnp.float32), axis=3)
    qh = q.transpose(0, 2, 1, 3)
    if t_len > Q_BLOCK and t_len % Q_BLOCK == 0:
        n_qb = t_len // Q_BLOCK
        qs = qh.reshape(b, h, n_qb, Q_BLOCK, dh).transpose(2, 0, 1, 3, 4)
        ps = positions.reshape(n_qb, Q_BLOCK)
        out = lax.map(lambda a: moba_attend(a[0], a[1], kb, vb, k_means), (qs, ps))
        out = out.transpose(1, 2, 0, 3, 4).reshape(b, h, t_len, dh)
    else:
        out = moba_attend(qh, positions, kb, vb, k_means)
    return out.transpose(0, 2, 1, 3).reshape(b, t_len, h * dh)


def conv_ffn(h, prev, w_up, conv_w, conv_b, w_down):
    up = h @ w_up
    b, t_len, c = up.shape
    if prev is None:
        prev = jnp.zeros((b, CONV_WIDTH - 1, c), up.dtype)
    full = jnp.concatenate([prev.astype(up.dtype), up], axis=1)
    conv = conv_b + sum(conv_w[j] * full[:, j:j + t_len] for j in range(CONV_WIDTH))
    a, g = jnp.split(conv, 2, axis=-1)
    out = (jax.nn.silu(a) * g) @ w_down
    return out, full[:, -(CONV_WIDTH - 1):]


def decoder_layer(x, c, pos0, pool_prev, conv_prev, past_k, past_v,
                  w_ada, b_ada, g_mix_pre, g_mix_post, w_in, pool_w, pool_scale, w_out,
                  g_ffn_pre, g_ffn_post, w_up, conv_w, conv_b, w_down):
    b, t_len, _ = x.shape
    mod = (jax.nn.silu(c) @ w_ada + b_ada)[:, None, :]
    sh1, sc1, gt1, sh2, sc2, gt2 = jnp.split(mod, 6, axis=-1)
    positions = pos0 + jnp.arange(t_len, dtype=jnp.int32)
    h = modulate(rmsnorm(x, g_mix_pre), sh1, sc1)
    proj = h @ w_in
    u, q, k, v = jnp.split(proj, [POOL_WIDTH, POOL_WIDTH + ATT_WIDTH, POOL_WIDTH + 2 * ATT_WIDTH], axis=-1)
    pool_out, pool_new = pool_mixer(u, pool_prev, pos0, pool_w, pool_scale)
    q = rope(q.reshape(b, t_len, N_HEADS, HEAD_DIM), positions)
    k = rope(k.reshape(b, t_len, N_HEADS, HEAD_DIM), positions)
    v = v.reshape(b, t_len, N_HEADS, HEAD_DIM)
    att_out = moba_mixer(q, k, v, positions, past_k, past_v)
    mix = jnp.concatenate([pool_out.astype(x.dtype), att_out.astype(x.dtype)], axis=-1) @ w_out
    x = x + gt1 * rmsnorm(mix, g_mix_post)
    h2 = modulate(rmsnorm(x, g_ffn_pre), sh2, sc2)
    ffn_out, conv_new = conv_ffn(h2, conv_prev, w_up, conv_w, conv_b, w_down)
    x = x + gt2 * rmsnorm(ffn_out, g_ffn_post)
    return x, k, v, pool_new, conv_new


def setup_inputs(seed: int = 0) -> dict:
    key = jax.random.key(seed)
    ks = jax.random.split(key, 24)
    f32 = jnp.float32
    n_pages = PAST_LEN // PAGE_SIZE
    n_phys = (DEC_BATCH * n_pages * 5) // 4

    def nrm(k, shape, scale):
        return jax.random.normal(k, shape, f32) * scale

    page_table = jax.random.permutation(ks[6], n_phys)[: DEC_BATCH * n_pages].reshape(DEC_BATCH, n_pages).astype(jnp.int32)
    return {
        "x_prompt": nrm(ks[0], (BATCH, SEQ, D_MODEL), 1.0),
        "x_sample": nrm(ks[1], (DEC_BATCH, DEC_SEQ, D_MODEL), 1.0),
        "c_prompt": nrm(ks[2], (BATCH, D_MODEL), 1.0),
        "c_sample": nrm(ks[3], (DEC_BATCH, D_MODEL), 1.0),
        "cache_k": nrm(ks[4], (DEPTH, n_phys, PAGE_SIZE, N_HEADS, HEAD_DIM), 1.0),
        "cache_v": nrm(ks[5], (DEPTH, n_phys, PAGE_SIZE, N_HEADS, HEAD_DIM), 1.0),
        "page_table": page_table,
        "state_pool": nrm(ks[7], (DEPTH, DEC_BATCH, POOL_STATE, POOL_WIDTH), 1.0),
        "state_conv": nrm(ks[8], (DEPTH, DEC_BATCH, CONV_WIDTH - 1, 2 * D_FF), 1.0),
        "w_ada": nrm(ks[9], (DEPTH, D_MODEL, 6 * D_MODEL), 0.5 * D_MODEL ** -0.5),
        "b_ada": nrm(ks[10], (DEPTH, 6 * D_MODEL), 0.01),
        "g_mix_pre": 1.0 + nrm(ks[11], (DEPTH, D_MODEL), 0.1),
        "g_mix_post": 1.0 + nrm(ks[12], (DEPTH, D_MODEL), 0.1),
        "w_in": nrm(ks[13], (DEPTH, D_MODEL, IN_WIDTH), D_MODEL ** -0.5),
        "pool_w": nrm(ks[14], (DEPTH, POOL_GROUPS, POOL_GC, POOL_GC), POOL_GC ** -0.5),
        "pool_scale": 1.0 + nrm(ks[15], (DEPTH, POOL_WIDTH), 0.1),
        "w_out": nrm(ks[16], (DEPTH, D_MODEL, D_MODEL), D_MODEL ** -0.5),
        "g_ffn_pre": 1.0 + nrm(ks[17], (DEPTH, D_MODEL), 0.1),
        "g_ffn_post": 1.0 + nrm(ks[18], (DEPTH, D_MODEL), 0.1),
        "w_up": nrm(ks[19], (DEPTH, D_MODEL, 2 * D_FF), D_MODEL ** -0.5),
        "conv_w": nrm(ks[20], (DEPTH, CONV_WIDTH, 2 * D_FF), CONV_WIDTH ** -0.5),
        "conv_b": nrm(ks[21], (DEPTH, 2 * D_FF), 0.02),
        "w_down": nrm(ks[22], (DEPTH, D_FF, D_MODEL), D_FF ** -0.5),
    }


def reference(x_prompt, x_sample, c_prompt, c_sample, cache_k, cache_v, page_table, state_pool, state_conv,
              w_ada, b_ada, g_mix_pre, g_mix_post, w_in, pool_w, pool_scale, w_out,
              g_ffn_pre, g_ffn_post, w_up, conv_w, conv_b, w_down):
    dec_b, n_pages = page_table.shape
    past_len = n_pages * cache_k.shape[2]
    y_p, y_s = x_prompt, x_sample
    kp_l, vp_l, pp_l, cp_l = [], [], [], []
    ks_l, vs_l, ps_l, cs_l = [], [], [], []
    for l in range(DEPTH):
        lw = (w_ada[l], b_ada[l], g_mix_pre[l], g_mix_post[l], w_in[l], pool_w[l], pool_scale[l], w_out[l],
              g_ffn_pre[l], g_ffn_post[l], w_up[l], conv_w[l], conv_b[l], w_down[l])
        y_p, kp, vp, pp, cp = decoder_layer(y_p, c_prompt, 0, None, None, None, None, *lw)
        past_k = cache_k[l][page_table].reshape(dec_b, past_len, N_HEADS, HEAD_DIM)
        past_v = cache_v[l][page_table].reshape(dec_b, past_len, N_HEADS, HEAD_DIM)
        y_s, ksn, vsn, psn, csn = decoder_layer(y_s, c_sample, past_len, state_pool[l], state_conv[l],
                                                past_k, past_v, *lw)
        kp_l.append(kp); vp_l.append(vp); pp_l.append(pp); cp_l.append(cp)
        ks_l.append(ksn); vs_l.append(vsn); ps_l.append(psn); cs_l.append(csn)
    k_prompt = jnp.stack(kp_l)
    v_prompt = jnp.stack(vp_l)
    pool_prompt = jnp.stack(pp_l)
    conv_prompt = jnp.stack(cp_l)
    k_sample = jnp.stack(ks_l)
    v_sample = jnp.stack(vs_l)
    pool_sample = jnp.stack(ps_l)
    conv_sample = jnp.stack(cs_l)
    return (y_p, y_s, k_prompt, v_prompt, pool_prompt, conv_prompt, k_sample, v_sample, pool_sample, conv_sample)
```

```python
import functools
import math

import jax
import jax.numpy as jnp
from jax import lax
from jax.experimental import pallas as pl
from jax.experimental.pallas import tpu as pltpu

HEAD_DIM = 64
MOBA_BLOCK = 256
MOBA_TOPK = 3
POOL_WINDOWS = (2, 4, 8, 16)
POOL_STATE = max(POOL_WINDOWS) - 1
CONV_WIDTH = 3
ROPE_THETA = 10000.0
RMS_EPS = 1e-6

LANES = 128
SUBLANES = 8
VMEM_LIMIT_BYTES = 56 * 1024 * 1024

POOL_HALO = 16
PRE_TILE = 512
POST_TILE = 256
FFN_CHUNKS = 2

F32 = jnp.float32
BF16 = jnp.bfloat16
LOG2E = math.log2(math.e)


def _params(*semantics):
    return pltpu.CompilerParams(dimension_semantics=semantics, vmem_limit_bytes=VMEM_LIMIT_BYTES)


def _resident(shape, index_map):
    return pl.BlockSpec(shape, index_map, pipeline_mode=pl.Buffered(1))


def _rms(x, g):
    ms = jnp.mean(x * x, axis=-1, keepdims=True)
    return x * lax.rsqrt(ms + RMS_EPS) * g


def _rope(x, cos, sin_signed):
    lane = lax.broadcasted_iota(jnp.int32, (1, LANES), 1)
    first_half = (lane % HEAD_DIM) < HEAD_DIM // 2
    half = HEAD_DIM // 2
    outs = []
    for c in range(x.shape[1] // LANES):
        blk = x[:, c * LANES:(c + 1) * LANES]
        partner = jnp.where(first_half, pltpu.roll(blk, LANES - half, 1), pltpu.roll(blk, half, 1))
        outs.append(blk * cos + partner * sin_signed)
    return jnp.concatenate(outs, axis=1)


def _pool_project(pooled, poolw_ref, pscale):
    n_groups, gc, _ = poolw_ref.shape
    outs = [jnp.dot(pooled[:, g * gc:(g + 1) * gc].astype(BF16), poolw_ref[g], preferred_element_type=F32)
            for g in range(n_groups)]
    return jnp.concatenate(outs, axis=1) * pscale


def _mod_kernel(c_ref, w_ref, b_ref, o_ref):
    c = c_ref[...]
    a = (c * jax.nn.sigmoid(c)).astype(BF16)
    o_ref[...] = jnp.dot(a, w_ref[...].astype(BF16), preferred_element_type=F32) + b_ref[...]


def _mod_call(c, w_ada, b_ada):
    rows, d = c.shape
    n_out = w_ada.shape[1]
    return pl.pallas_call(
        _mod_kernel,
        grid=(n_out // d,),
        in_specs=[pl.BlockSpec((rows, d), lambda j: (0, 0)),
                  pl.BlockSpec((d, d), lambda j: (0, j)),
                  pl.BlockSpec((1, d), lambda j: (0, j))],
        out_specs=pl.BlockSpec((rows, d), lambda j: (0, j)),
        out_shape=jax.ShapeDtypeStruct((rows, n_out), F32),
        compiler_params=_params("arbitrary"),
        name="adaln_mod",
    )(c, w_ada, b_ada.reshape(1, n_out))


def _pre_prompt_kernel(x_ref, mod_ref, g_ref, win_ref, poolw_ref, pscale_ref, cos_ref, sin_ref,
                       k_ref, v_ref, qt_ref, kb_ref, vt_ref, pool_ref, utail_ref,
                       ubuf, s2buf, s4buf, s8buf, *, tm, d_model, pw, aw, q_scale):
    i = pl.program_id(1)
    halo = POOL_HALO
    gc = pw // len(POOL_WINDOWS)
    bufs = (ubuf, s2buf, s4buf, s8buf)

    @pl.when(i == 0)
    def _():
        for buf in bufs:
            buf[0:halo, :] = jnp.zeros((halo, buf.shape[1]), F32)

    x = x_ref[0]
    mod = mod_ref[0]
    shift, scale = mod[:, 0:d_model], mod[:, d_model:2 * d_model]
    h = _rms(x, g_ref[...]) * (1.0 + scale) + shift
    proj = jnp.dot(h.astype(BF16), win_ref[...], preferred_element_type=F32)
    u = proj[:, 0:pw]
    cos, sin_signed = cos_ref[...], sin_ref[...]
    q = _rope(proj[:, pw:pw + aw], cos, sin_signed)
    k = _rope(proj[:, pw + aw:pw + 2 * aw], cos, sin_signed)
    v = proj[:, pw + 2 * aw:pw + 3 * aw]

    k_ref[0] = k
    v_ref[0] = v
    kb_ref[0] = k.astype(BF16)
    qt_ref[0] = (q * q_scale).T.astype(BF16)
    for r in range(tm // MOBA_BLOCK):
        vt_ref[0, r] = v[r * MOBA_BLOCK:(r + 1) * MOBA_BLOCK, :].T.astype(BF16)

    ubuf[halo:halo + tm, :] = u
    s2 = u + ubuf[halo - 1:halo - 1 + tm, :]
    s2buf[halo:halo + tm, :] = s2[:, gc:]
    s4 = s2[:, gc:] + s2buf[halo - 2:halo - 2 + tm, :]
    s4buf[halo:halo + tm, :] = s4[:, gc:]
    s8 = s4[:, gc:] + s4buf[halo - 4:halo - 4 + tm, :]
    s8buf[halo:halo + tm, :] = s8[:, gc:]
    s16 = s8[:, gc:] + s8buf[halo - 8:halo - 8 + tm, :]
    wins = (s2[:, 0:gc], s4[:, 0:gc], s8[:, 0:gc], s16)
    pos1 = (i * tm + 1 + lax.broadcasted_iota(jnp.int32, (tm, 1), 0)).astype(F32)
    pooled = jnp.concatenate(
        [wins[g] / jnp.minimum(float(w), pos1) - u[:, g * gc:(g + 1) * gc] for g, w in enumerate(POOL_WINDOWS)],
        axis=1)
    pool_ref[0] = _pool_project(pooled, poolw_ref, pscale_ref[...]).astype(BF16)

    for buf in bufs:
        buf[0:halo, :] = buf[tm:tm + halo, :]
    utail_ref[0] = ubuf[0:halo, :]


def _pre_prompt_call(x, mod, g_pre, w_in_b, pool_w_b, pool_scale, cos_t, sin_t, *, pw, aw):
    b, t, d = x.shape
    tm = min(PRE_TILE, t)
    nb = t // MOBA_BLOCK
    gc = pw // len(POOL_WINDOWS)
    in_w = w_in_b.shape[1]
    kern = functools.partial(_pre_prompt_kernel, tm=tm, d_model=d, pw=pw, aw=aw,
                             q_scale=HEAD_DIM ** -0.5 * LOG2E)
    row_tile = lambda bi, i: (bi, i, 0)
    const2 = lambda bi, i: (0, 0)
    return pl.pallas_call(
        kern,
        grid=(b, t // tm),
        in_specs=[pl.BlockSpec((1, tm, d), row_tile),
                  pl.BlockSpec((1, 1, mod.shape[2]), lambda bi, i: (bi, 0, 0)),
                  _resident((1, d), const2),
                  _resident((d, in_w), const2),
                  _resident(pool_w_b.shape, lambda bi, i: (0, 0, 0)),
                  _resident((1, pw), const2),
                  pl.BlockSpec((tm, LANES), lambda bi, i: (i, 0)),
                  pl.BlockSpec((tm, LANES), lambda bi, i: (i, 0))],
        out_specs=[pl.BlockSpec((1, tm, aw), row_tile),
                   pl.BlockSpec((1, tm, aw), row_tile),
                   pl.BlockSpec((1, aw, tm), lambda bi, i: (bi, 0, i)),
                   pl.BlockSpec((1, tm, aw), row_tile),
                   pl.BlockSpec((1, tm // MOBA_BLOCK, aw, MOBA_BLOCK), lambda bi, i: (bi, i, 0, 0)),
                   pl.BlockSpec((1, tm, pw), row_tile),
                   pl.BlockSpec((1, POOL_HALO, pw), lambda bi, i: (bi, 0, 0))],
        out_shape=[jax.ShapeDtypeStruct((b, t, aw), F32),
                   jax.ShapeDtypeStruct((b, t, aw), F32),
                   jax.ShapeDtypeStruct((b, aw, t), BF16),
                   jax.ShapeDtypeStruct((b, t, aw), BF16),
                   jax.ShapeDtypeStruct((b, nb, aw, MOBA_BLOCK), BF16),
                   jax.ShapeDtypeStruct((b, t, pw), BF16),
                   jax.ShapeDtypeStruct((b, POOL_HALO, pw), F32)],
        scratch_shapes=[pltpu.VMEM((tm + POOL_HALO, pw), F32),
                        pltpu.VMEM((tm + POOL_HALO, pw - gc), F32),
                        pltpu.VMEM((tm + POOL_HALO, pw - 2 * gc), F32),
                        pltpu.VMEM((tm + POOL_HALO, pw - 3 * gc), F32)],
        compiler_params=_params("arbitrary", "arbitrary"),
        name="pre_prompt",
    )(x, mod, g_pre, w_in_b, pool_w_b, pool_scale, cos_t, sin_t)


def _moba_prompt_kernel(qt_ref, kb_ref, vt_ref, o_ref, km_ref, bias_ref, *, nb, n_heads):
    i = pl.program_id(1)
    tq = MOBA_BLOCK
    pair_w = 2 * HEAD_DIM

    @pl.when(i == 0)
    def _():
        for n in range(nb):
            km_ref[n:n + 1, :] = jnp.mean(kb_ref[0, n].astype(F32), axis=0, keepdims=True)

    blk_row = lax.broadcasted_iota(jnp.int32, (nb, tq), 0)
    blk_row_f = blk_row.astype(F32)
    fully_past = blk_row < i
    key_row = lax.broadcasted_iota(jnp.int32, (MOBA_BLOCK, tq), 0)
    qry_col = lax.broadcasted_iota(jnp.int32, (MOBA_BLOCK, tq), 1)
    causal = key_row <= qry_col
    pair_row = lax.broadcasted_iota(jnp.int32, (pair_w, tq), 0)

    for h in range(n_heads):
        lo = (h // 2) * pair_w
        hrows = slice(h * HEAD_DIM, (h + 1) * HEAD_DIM)
        mine = (pair_row >= HEAD_DIM) if h % 2 else (pair_row < HEAD_DIM)
        q2 = jnp.where(mine, qt_ref[0, lo:lo + pair_w, :], jnp.zeros((), BF16))

        gate = jnp.dot(km_ref[:, lo:lo + pair_w].astype(BF16), q2, preferred_element_type=F32)
        gate = jnp.where(fully_past, gate, -jnp.inf)
        chosen = jnp.zeros((nb, tq), jnp.bool_)
        for _ in range(MOBA_TOPK):
            best = jnp.max(gate, axis=0, keepdims=True)
            first = jnp.min(jnp.where(gate == best, blk_row_f, float(nb)), axis=0, keepdims=True)
            pick = blk_row_f == first
            chosen = jnp.logical_or(chosen, pick)
            gate = jnp.where(pick, -jnp.inf, gate)
        bias = jnp.where(jnp.logical_and(chosen, fully_past), 0.0, -jnp.inf).astype(F32)
        for n in range(nb):
            bias_ref[n] = bias[n:n + 1, :]

        s = jnp.dot(kb_ref[0, i, :, lo:lo + pair_w], q2, preferred_element_type=F32)
        s = jnp.where(causal, s, -jnp.inf)
        m0 = jnp.max(s, axis=0, keepdims=True)
        p = jnp.exp2(s - m0)
        l0 = jnp.sum(p, axis=0, keepdims=True)
        acc0 = jnp.dot(vt_ref[0, i, hrows, :], p.astype(BF16), preferred_element_type=F32)

        def body(j, carry, lo=lo, hrows=hrows, q2=q2):
            m, l, acc = carry
            s = jnp.dot(kb_ref[0, j, :, lo:lo + pair_w], q2, preferred_element_type=F32) + bias_ref[j]
            m_new = jnp.maximum(m, jnp.max(s, axis=0, keepdims=True))
            alpha = jnp.exp2(m - m_new)
            p = jnp.exp2(s - m_new)
            l = alpha * l + jnp.sum(p, axis=0, keepdims=True)
            acc = alpha * acc + jnp.dot(vt_ref[0, j, hrows, :], p.astype(BF16), preferred_element_type=F32)
            return m_new, l, acc

        _, l, acc = lax.fori_loop(0, i, body, (m0, l0, acc0))
        o_ref[0, hrows, :] = (acc / l).astype(BF16)


def _moba_prompt_call(qt, kb4, vt4):
    b, aw, t = qt.shape
    nb = t // MOBA_BLOCK
    n_heads = aw // HEAD_DIM
    kern = functools.partial(_moba_prompt_kernel, nb=nb, n_heads=n_heads)
    per_batch = lambda bi, i: (bi, 0, 0, 0)
    return pl.pallas_call(
        kern,
        grid=(b, nb),
        in_specs=[pl.BlockSpec((1, aw, MOBA_BLOCK), lambda bi, i: (bi, 0, i)),
                  pl.BlockSpec((1, nb, MOBA_BLOCK, aw), per_batch, pipeline_mode=pl.Buffered(1)),
                  pl.BlockSpec((1, nb, aw, MOBA_BLOCK), per_batch, pipeline_mode=pl.Buffered(1))],
        out_specs=pl.BlockSpec((1, aw, MOBA_BLOCK), lambda bi, i: (bi, 0, i)),
        out_shape=jax.ShapeDtypeStruct((b, aw, t), BF16),
        scratch_shapes=[pltpu.VMEM((nb, aw), F32),
                        pltpu.VMEM((nb, 1, MOBA_BLOCK), F32)],
        compiler_params=_params("arbitrary", "arbitrary"),
        name="moba_prompt",
    )(qt, kb4, vt4)


def _post_kernel(*refs, tm, d_model, d_ff, att_transposed, per_row_state):
    (x_ref, pool_ref, att_ref, mod_ref, wop_ref, woa_ref, gpost_ref, gpre2_ref, gpost2_ref,
     wup_ref, cw_ref, cb_ref, wdn_ref) = refs[:13]
    if per_row_state:
        st0_ref, st1_ref, y_ref, up_ref = refs[13:]
    else:
        y_ref, ctail_ref, carry, upbuf = refs[13:]
        i = pl.program_id(1)

        @pl.when(i == 0)
        def _():
            carry[...] = jnp.zeros(carry.shape, F32)

    d = d_model
    x = x_ref[0]
    mod = mod_ref[0]
    gate1 = mod[:, 2 * d:3 * d]
    shift2, scale2, gate2 = mod[:, 3 * d:4 * d], mod[:, 4 * d:5 * d], mod[:, 5 * d:6 * d]

    mix = jnp.dot(pool_ref[0], wop_ref[...], preferred_element_type=F32)
    if att_transposed:
        mix += lax.dot_general(att_ref[0], woa_ref[...], (((0,), (0,)), ((), ())), preferred_element_type=F32)
    else:
        mix += jnp.dot(att_ref[0], woa_ref[...], preferred_element_type=F32)
    x1 = x + gate1 * _rms(mix, gpost_ref[...])
    h2 = (_rms(x1, gpre2_ref[...]) * (1.0 + scale2) + shift2).astype(BF16)

    fc = d_ff // FFN_CHUNKS
    ffn = jnp.zeros((tm, d), F32)
    for c in range(FFN_CHUNKS):
        halves = []
        for base in (0, d_ff):
            cols = slice(base + c * fc, base + (c + 1) * fc)
            up = jnp.dot(h2, wup_ref[:, cols], preferred_element_type=F32)
            if per_row_state:
                prev2, prev1 = st0_ref[0, :, cols], st1_ref[0, :, cols]
                up_ref[0, :, cols] = up
            else:
                upbuf[0:SUBLANES, :] = carry[:, cols]
                upbuf[SUBLANES:SUBLANES + tm, :] = up
                prev1 = upbuf[SUBLANES - 1:SUBLANES - 1 + tm, :]
                prev2 = upbuf[SUBLANES - 2:SUBLANES - 2 + tm, :]
                carry[:, cols] = upbuf[tm:tm + SUBLANES, :]
            halves.append(cb_ref[:, cols] + cw_ref[0:1, cols] * prev2 + cw_ref[1:2, cols] * prev1
                          + cw_ref[2:3, cols] * up)
        a, g = halves
        act = (a * jax.nn.sigmoid(a) * g).astype(BF16)
        ffn += jnp.dot(act, wdn_ref[c * fc:(c + 1) * fc, :], preferred_element_type=F32)
    y_ref[0] = x1 + gate2 * _rms(ffn, gpost2_ref[...])
    if not per_row_state:
        ctail_ref[0] = carry[...]


def _post_call(x, pool_o, att, mod, w, *, att_transposed, state=None):
    b, t, d = x.shape
    pw = pool_o.shape[2]
    aw = w["w_out_a"].shape[0]
    d_ff = w["w_down"].shape[0]
    per_row = state is not None
    tm = t if per_row else min(POST_TILE, t)
    assert t % tm == 0 and d_ff % (FFN_CHUNKS * LANES) == 0 and CONV_WIDTH == 3
    fc = d_ff // FFN_CHUNKS
    kern = functools.partial(_post_kernel, tm=tm, d_model=d, d_ff=d_ff, att_transposed=att_transposed,
                             per_row_state=per_row)
    row_tile = lambda bi, i: (bi, i, 0)
    const2 = lambda bi, i: (0, 0)
    mod_rows = mod.shape[1]
    in_specs = [pl.BlockSpec((1, tm, d), row_tile),
                pl.BlockSpec((1, tm, pw), row_tile),
                (pl.BlockSpec((1, aw, tm), lambda bi, i: (bi, 0, i)) if att_transposed
                 else pl.BlockSpec((1, tm, aw), row_tile)),
                pl.BlockSpec((1, mod_rows, mod.shape[2]), (row_tile if mod_rows > 1 else lambda bi, i: (bi, 0, 0))),
                _resident((pw, d), const2),
                _resident((aw, d), const2),
                _resident((1, d), const2),
                _resident((1, d), const2),
                _resident((1, d), const2),
                _resident((d, 2 * d_ff), const2),
                _resident((CONV_WIDTH, 2 * d_ff), const2),
                _resident((1, 2 * d_ff), const2),
                _resident((d_ff, d), const2)]
    args = [x, pool_o, att, mod, w["w_out_p"], w["w_out_a"], w["g_mix_post"], w["g_ffn_pre"], w["g_ffn_post"],
            w["w_up"], w["conv_w"], w["conv_b"], w["w_down"]]
    if per_row:
        in_specs += [pl.BlockSpec((1, tm, 2 * d_ff), row_tile)] * 2
        args += list(state)
        out_specs = [pl.BlockSpec((1, tm, d), row_tile), pl.BlockSpec((1, tm, 2 * d_ff), row_tile)]
        out_shape = [jax.ShapeDtypeStruct((b, t, d), F32), jax.ShapeDtypeStruct((b, t, 2 * d_ff), F32)]
        scratch = []
    else:
        out_specs = [pl.BlockSpec((1, tm, d), row_tile),
                     pl.BlockSpec((1, SUBLANES, 2 * d_ff), lambda bi, i: (bi, 0, 0))]
        out_shape = [jax.ShapeDtypeStruct((b, t, d), F32), jax.ShapeDtypeStruct((b, SUBLANES, 2 * d_ff), F32)]
        scratch = [pltpu.VMEM((SUBLANES, 2 * d_ff), F32), pltpu.VMEM((tm + SUBLANES, fc), F32)]
    return pl.pallas_call(
        kern,
        grid=(b, t // tm),
        in_specs=in_specs,
        out_specs=out_specs,
        out_shape=out_shape,
        scratch_shapes=scratch,
        compiler_params=_params("arbitrary", "arbitrary"),
        name="post_sample" if per_row else "post_prompt",
    )(*args)


def _pre_sample_kernel(x_ref, mod_ref, g_ref, win_ref, poolw_ref, pscale_ref, cos_ref, sin_ref, st_ref,
                       u_ref, q_ref, k_ref, v_ref, pool_ref, *, d_model, pw, aw, pos0):
    gc = pw // len(POOL_WINDOWS)
    x = x_ref[...]
    mod = mod_ref[...]
    shift, scale = mod[:, 0:d_model], mod[:, d_model:2 * d_model]
    h = _rms(x, g_ref[...]) * (1.0 + scale) + shift
    proj = jnp.dot(h.astype(BF16), win_ref[...], preferred_element_type=F32)
    u = proj[:, 0:pw]
    cos, sin_signed = cos_ref[...], sin_ref[...]
    u_ref[...] = u
    q_ref[...] = _rope(proj[:, pw:pw + aw], cos, sin_signed)
    k_ref[...] = _rope(proj[:, pw + aw:pw + 2 * aw], cos, sin_signed)
    v_ref[...] = proj[:, pw + 2 * aw:pw + 3 * aw]

    st = st_ref[...]
    pooled = []
    for g, w in enumerate(POOL_WINDOWS):
        cols = slice(g * gc, (g + 1) * gc)
        win = u[:, cols] + jnp.sum(st[:, POOL_STATE - (w - 1):, cols], axis=1)
        pooled.append(win / float(min(w, pos0 + 1)) - u[:, cols])
    pool_ref[...] = _pool_project(jnp.concatenate(pooled, axis=1), poolw_ref, pscale_ref[...]).astype(BF16)


def _pre_sample_call(x, mod, g_pre, w_in_b, pool_w_b, pool_scale, cos_t, sin_t, state_pool, *, pw, aw, pos0):
    rows, d = x.shape
    kern = functools.partial(_pre_sample_kernel, d_model=d, pw=pw, aw=aw, pos0=pos0)
    return pl.pallas_call(
        kern,
        out_shape=[jax.ShapeDtypeStruct((rows, pw), F32),
                   jax.ShapeDtypeStruct((rows, aw), F32),
                   jax.ShapeDtypeStruct((rows, aw), F32),
                   jax.ShapeDtypeStruct((rows, aw), F32),
                   jax.ShapeDtypeStruct((rows, pw), BF16)],
        compiler_params=pltpu.CompilerParams(vmem_limit_bytes=VMEM_LIMIT_BYTES),
        name="pre_sample",
    )(x, mod, g_pre, w_in_b, pool_w_b, pool_scale, cos_t, sin_t, state_pool)


def _moba_sample_kernel(pt_ref, q_ref, kn_ref, vn_ref, *refs, n_pages, page_size):
    del pt_ref
    k_refs, v_refs, o_ref = refs[:n_pages], refs[n_pages:2 * n_pages], refs[2 * n_pages]
    n_heads = q_ref.shape[1]
    ppb = MOBA_BLOCK // page_size
    n_blocks = n_pages // ppb
    flat = page_size * n_heads

    qs = q_ref[0] * (HEAD_DIM ** -0.5 * LOG2E)
    qb = qs.astype(BF16)
    head_row = lax.broadcasted_iota(jnp.int32, (n_heads, flat), 0)
    head_of_lane = lax.broadcasted_iota(jnp.int32, (n_heads, flat), 1) % n_heads
    same_head = head_row == head_of_lane

    scores, ksums = [], []
    for p in range(n_pages):
        kp = k_refs[p][...]
        ksums.append(jnp.sum(kp, axis=0))
        kmat = kp.reshape(flat, HEAD_DIM).astype(BF16)
        scores.append(lax.dot_general(qb, kmat, (((1,), (1,)), ((), ())), preferred_element_type=F32))

    gates = []
    for n in range(n_blocks):
        kmean = sum(ksums[n * ppb:(n + 1) * ppb]) * (1.0 / MOBA_BLOCK)
        gates.append(jnp.sum(qs * kmean, axis=-1, keepdims=True))
    chosen = []
    for n in range(n_blocks):
        beaten = jnp.zeros((n_heads, 1), jnp.int32)
        for m in range(n_blocks):
            if m != n:
                wins = (gates[m] >= gates[n]) if m < n else (gates[m] > gates[n])
                beaten = beaten + wins.astype(jnp.int32)
        chosen.append(beaten < MOBA_TOPK)

    s_new = jnp.sum(qs * kn_ref[0], axis=-1, keepdims=True)
    m = s_new
    for p in range(n_pages):
        keep = jnp.logical_and(same_head, chosen[p // ppb])
        scores[p] = jnp.where(keep, scores[p], -jnp.inf)
        m = jnp.maximum(m, jnp.max(scores[p], axis=-1, keepdims=True))
    p_new = jnp.exp2(s_new - m)
    l = p_new
    acc = p_new * vn_ref[0]
    for p in range(n_pages):
        w = jnp.exp2(scores[p] - m)
        l = l + jnp.sum(w, axis=-1, keepdims=True)
        vmat = v_refs[p][...].reshape(flat, HEAD_DIM).astype(BF16)
        acc = acc + jnp.dot(w.astype(BF16), vmat, preferred_element_type=F32)
    o_ref[0] = acc / l


def _moba_sample_call(page_table, q, k_new, v_new, cache_k, cache_v):
    n_seq, n_pages = page_table.shape
    _, page_size, n_heads, dh = cache_k.shape
    assert dh == HEAD_DIM and MOBA_BLOCK % page_size == 0 and (n_pages * page_size) % MOBA_BLOCK == 0
    kern = functools.partial(_moba_sample_kernel, n_pages=n_pages, page_size=page_size)
    tok = pl.BlockSpec((1, n_heads, dh), lambda s, pt: (s, 0, 0))

    def page_spec(p):
        return pl.BlockSpec((None, page_size, n_heads, dh), lambda s, pt: (pt[s, p], 0, 0, 0))

    pages = [page_spec(p) for p in range(n_pages)]
    return pl.pallas_call(
        kern,
        grid_spec=pltpu.PrefetchScalarGridSpec(
            num_scalar_prefetch=1,
            grid=(n_seq,),
            in_specs=[tok, tok, tok] + pages + pages,
            out_specs=tok),
        out_shape=jax.ShapeDtypeStruct((n_seq, n_heads, dh), F32),
        compiler_params=_params("arbitrary"),
        name="moba_sample",
    )(page_table, q, k_new, v_new, *([cache_k] * n_pages), *([cache_v] * n_pages))


def _rope_tables(positions):
    half = HEAD_DIM // 2
    inv = ROPE_THETA ** (-jnp.arange(half, dtype=F32) / half)
    ang = positions.astype(F32)[:, None] * inv[None, :]
    cos, sin = jnp.cos(ang), jnp.sin(ang)
    reps = LANES // HEAD_DIM
    return (jnp.tile(jnp.concatenate([cos, cos], axis=1), (1, reps)),
            jnp.tile(jnp.concatenate([-sin, sin], axis=1), (1, reps)))


def kernel(x_prompt, x_sample, c_prompt, c_sample, cache_k, cache_v, page_table, state_pool, state_conv, w_ada, b_ada, g_mix_pre, g_mix_post, w_in, pool_w, pool_scale, w_out, g_ffn_pre, g_ffn_post, w_up, conv_w, conv_b, w_down):
    depth = w_ada.shape[0]
    b, t, d = x_prompt.shape
    n_seq, dec_t, _ = x_sample.shape
    n_pages = page_table.shape[1]
    page_size, n_heads = cache_k.shape[2], cache_k.shape[3]
    past_len = n_pages * page_size
    aw = n_heads * HEAD_DIM
    pw = state_pool.shape[3]
    d_ff = w_down.shape[1]
    assert dec_t == 1 and t % MOBA_BLOCK == 0 and t % PRE_TILE == 0 and pw == d - aw
    assert (pw // len(POOL_WINDOWS)) % LANES == 0 and w_in.shape[2] == pw + 3 * aw

    cos_p, sin_p = _rope_tables(jnp.arange(t, dtype=jnp.int32))
    cos_s, sin_s = _rope_tables(past_len + jnp.arange(dec_t, dtype=jnp.int32))

    y_p, y_s = x_prompt, x_sample.reshape(1, n_seq, d)
    outs = [[] for _ in range(8)]
    for l in range(depth):
        w = dict(w_out_p=w_out[l, :pw].astype(BF16), w_out_a=w_out[l, pw:].astype(BF16),
                 g_mix_post=g_mix_post[l].reshape(1, d), g_ffn_pre=g_ffn_pre[l].reshape(1, d),
                 g_ffn_post=g_ffn_post[l].reshape(1, d), w_up=w_up[l].astype(BF16), conv_w=conv_w[l],
                 conv_b=conv_b[l].reshape(1, 2 * d_ff), w_down=w_down[l].astype(BF16))
        g_pre = g_mix_pre[l].reshape(1, d)
        w_in_b = w_in[l].astype(BF16)
        pool_w_b = pool_w[l].astype(BF16)
        pscale = pool_scale[l].reshape(1, pw)

        mod = _mod_call(jnp.concatenate([c_prompt, c_sample], axis=0), w_ada[l], b_ada[l])
        mod_p, mod_s = mod[:b].reshape(b, 1, 6 * d), mod[b:].reshape(1, n_seq, 6 * d)

        k_p, v_p, qt, kb, vt4, pool_p, utail = _pre_prompt_call(
            y_p, mod_p, g_pre, w_in_b, pool_w_b, pscale, cos_p, sin_p, pw=pw, aw=aw)
        att_t = _moba_prompt_call(qt, kb.reshape(b, t // MOBA_BLOCK, MOBA_BLOCK, aw), vt4)
        y_p, ctail = _post_call(y_p, pool_p, att_t, mod_p, w, att_transposed=True)

        u_s, q_s, k_s, v_s, pool_s = _pre_sample_call(
            y_s[0], mod_s[0], g_pre, w_in_b, pool_w_b, pscale, cos_s, sin_s, state_pool[l],
            pw=pw, aw=aw, pos0=past_len)
        heads = lambda a: a.reshape(n_seq, n_heads, HEAD_DIM)
        att_s = _moba_sample_call(page_table, heads(q_s), heads(k_s), heads(v_s), cache_k[l], cache_v[l])
        y_s, up_s = _post_call(y_s, pool_s.reshape(1, n_seq, pw), att_s.reshape(1, n_seq, aw).astype(BF16), mod_s, w,
                               att_transposed=False,
                               state=(state_conv[l, :, 0].reshape(1, n_seq, 2 * d_ff),
                                      state_conv[l, :, 1].reshape(1, n_seq, 2 * d_ff)))

        outs[0].append(k_p.reshape(b, t, n_heads, HEAD_DIM))
        outs[1].append(v_p.reshape(b, t, n_heads, HEAD_DIM))
        outs[2].append(utail[:, POOL_HALO - POOL_STATE:])
        outs[3].append(ctail[:, SUBLANES - (CONV_WIDTH - 1):])
        outs[4].append(k_s.reshape(n_seq, dec_t, n_heads, HEAD_DIM))
        outs[5].append(v_s.reshape(n_seq, dec_t, n_heads, HEAD_DIM))
        outs[6].append(jnp.concatenate([state_pool[l, :, 1:], u_s[:, None, :]], axis=1))
        outs[7].append(jnp.concatenate([state_conv[l, :, 1:], up_s[0][:, None, :]], axis=1))
    stacked = [jnp.stack(o) for o in outs]
    return (y_p, y_s.reshape(n_seq, dec_t, d), *stacked)
```

```python
import functools
import math

import jax
import jax.numpy as jnp
from jax import lax
from jax.experimental import pallas as pl
from jax.experimental.pallas import tpu as pltpu

HEAD_DIM = 64
MOBA_BLOCK = 256
MOBA_TOPK = 3
POOL_WINDOWS = (2, 4, 8, 16)
POOL_STATE = max(POOL_WINDOWS) - 1
CONV_WIDTH = 3
ROPE_THETA = 10000.0
RMS_EPS = 1e-6

LANES = 128
SUBLANES = 8
VMEM_LIMIT_BYTES = 56 * 1024 * 1024

BF16_ROWS = 16
V_ROWS = HEAD_DIM + BF16_ROWS

POOL_HALO = 16
PRE_TILE = 512
POST_TILE = 256
FFN_CHUNKS = 2
SCORE_LOOKAHEAD = 3

F32 = jnp.float32
BF16 = jnp.bfloat16
LOG2E = math.log2(math.e)


def _params(*semantics):
    return pltpu.CompilerParams(dimension_semantics=semantics, vmem_limit_bytes=VMEM_LIMIT_BYTES)


def _resident(shape, index_map):
    return pl.BlockSpec(shape, index_map, pipeline_mode=pl.Buffered(1))


def _rms(x, g):
    ms = jnp.mean(x * x, axis=-1, keepdims=True)
    return x * lax.rsqrt(ms + RMS_EPS) * g


def _rope(x, cos, sin_signed):
    lane = lax.broadcasted_iota(jnp.int32, (1, LANES), 1)
    first_half = (lane % HEAD_DIM) < HEAD_DIM // 2
    half = HEAD_DIM // 2
    outs = []
    for c in range(x.shape[1] // LANES):
        blk = x[:, c * LANES:(c + 1) * LANES]
        partner = jnp.where(first_half, pltpu.roll(blk, LANES - half, 1), pltpu.roll(blk, half, 1))
        outs.append(blk * cos + partner * sin_signed)
    return jnp.concatenate(outs, axis=1)


def _pool_project(pooled, poolw_ref, pscale):
    n_groups, gc, _ = poolw_ref.shape
    outs = [jnp.dot(pooled[:, g * gc:(g + 1) * gc].astype(BF16), poolw_ref[g], preferred_element_type=F32)
            for g in range(n_groups)]
    return jnp.concatenate(outs, axis=1) * pscale


def _mod_kernel(c_ref, w_ref, b_ref, o_ref):
    c = c_ref[...]
    a = (c * jax.nn.sigmoid(c)).astype(BF16)
    o_ref[...] = jnp.dot(a, w_ref[...].astype(BF16), preferred_element_type=F32) + b_ref[...]


def _mod_call(c, w_ada, b_ada):
    rows, d = c.shape
    n_out = w_ada.shape[1]
    return pl.pallas_call(
        _mod_kernel,
        grid=(n_out // d,),
        in_specs=[pl.BlockSpec((rows, d), lambda j: (0, 0)),
                  pl.BlockSpec((d, d), lambda j: (0, j)),
                  pl.BlockSpec((1, d), lambda j: (0, j))],
        out_specs=pl.BlockSpec((rows, d), lambda j: (0, j)),
        out_shape=jax.ShapeDtypeStruct((rows, n_out), F32),
        compiler_params=_params("arbitrary"),
        name="adaln_mod",
    )(c, w_ada, b_ada.reshape(1, n_out))


def _pre_prompt_kernel(x_ref, mod_ref, g_ref, win_ref, poolw_ref, pscale_ref, cos_ref, sin_ref,
                       k_ref, v_ref, qt_ref, kb_ref, vt_ref, pool_ref, utail_ref,
                       ubuf, s2buf, s4buf, s8buf, *, tm, d_model, pw, aw, q_scale):
    i = pl.program_id(1)
    halo = POOL_HALO
    gc = pw // len(POOL_WINDOWS)
    bufs = (ubuf, s2buf, s4buf, s8buf)

    @pl.when(i == 0)
    def _():
        for buf in bufs:
            buf[0:halo, :] = jnp.zeros((halo, buf.shape[1]), F32)

    x = x_ref[0]
    mod = mod_ref[0]
    shift, scale = mod[:, 0:d_model], mod[:, d_model:2 * d_model]
    h = _rms(x, g_ref[...]) * (1.0 + scale) + shift
    proj = jnp.dot(h.astype(BF16), win_ref[...], preferred_element_type=F32)
    u = proj[:, 0:pw]
    cos, sin_signed = cos_ref[...], sin_ref[...]
    q = _rope(proj[:, pw:pw + aw], cos, sin_signed)
    k = _rope(proj[:, pw + aw:pw + 2 * aw], cos, sin_signed)
    v = proj[:, pw + 2 * aw:pw + 3 * aw]

    k_ref[0] = k
    v_ref[0] = v
    kb_ref[0] = k.astype(BF16)
    qt_ref[0] = (q * q_scale).T.astype(BF16)
    ones_pad = jnp.where(lax.broadcasted_iota(jnp.int32, (BF16_ROWS, MOBA_BLOCK), 0) == 0, 1.0, 0.0).astype(BF16)
    for r in range(tm // MOBA_BLOCK):
        vt = v[r * MOBA_BLOCK:(r + 1) * MOBA_BLOCK, :].T.astype(BF16)
        for hd in range(aw // HEAD_DIM):
            vt_ref[0, r, hd * V_ROWS:hd * V_ROWS + HEAD_DIM, :] = vt[hd * HEAD_DIM:(hd + 1) * HEAD_DIM, :]
            vt_ref[0, r, hd * V_ROWS + HEAD_DIM:(hd + 1) * V_ROWS, :] = ones_pad

    ubuf[halo:halo + tm, :] = u
    s2 = u + ubuf[halo - 1:halo - 1 + tm, :]
    s2buf[halo:halo + tm, :] = s2[:, gc:]
    s4 = s2[:, gc:] + s2buf[halo - 2:halo - 2 + tm, :]
    s4buf[halo:halo + tm, :] = s4[:, gc:]
    s8 = s4[:, gc:] + s4buf[halo - 4:halo - 4 + tm, :]
    s8buf[halo:halo + tm, :] = s8[:, gc:]
    s16 = s8[:, gc:] + s8buf[halo - 8:halo - 8 + tm, :]
    wins = (s2[:, 0:gc], s4[:, 0:gc], s8[:, 0:gc], s16)
    pos1 = (i * tm + 1 + lax.broadcasted_iota(jnp.int32, (tm, 1), 0)).astype(F32)
    pooled = jnp.concatenate(
        [wins[g] / jnp.minimum(float(w), pos1) - u[:, g * gc:(g + 1) * gc] for g, w in enumerate(POOL_WINDOWS)],
        axis=1)
    pool_ref[0] = _pool_project(pooled, poolw_ref, pscale_ref[...]).astype(BF16)

    for buf in bufs:
        buf[0:halo, :] = buf[tm:tm + halo, :]
    utail_ref[0] = ubuf[0:halo, :]


def _pre_prompt_call(x, mod, g_pre, w_in_b, pool_w_b, pool_scale, cos_t, sin_t, *, pw, aw):
    b, t, d = x.shape
    tm = min(PRE_TILE, t)
    nb = t // MOBA_BLOCK
    gc = pw // len(POOL_WINDOWS)
    in_w = w_in_b.shape[1]
    vt_rows = (aw // HEAD_DIM) * V_ROWS
    kern = functools.partial(_pre_prompt_kernel, tm=tm, d_model=d, pw=pw, aw=aw,
                             q_scale=HEAD_DIM ** -0.5 * LOG2E)
    row_tile = lambda bi, i: (bi, i, 0)
    const2 = lambda bi, i: (0, 0)
    return pl.pallas_call(
        kern,
        grid=(b, t // tm),
        in_specs=[pl.BlockSpec((1, tm, d), row_tile),
                  pl.BlockSpec((1, 1, mod.shape[2]), lambda bi, i: (bi, 0, 0)),
                  _resident((1, d), const2),
                  _resident((d, in_w), const2),
                  _resident(pool_w_b.shape, lambda bi, i: (0, 0, 0)),
                  _resident((1, pw), const2),
                  pl.BlockSpec((tm, LANES), lambda bi, i: (i, 0)),
                  pl.BlockSpec((tm, LANES), lambda bi, i: (i, 0))],
        out_specs=[pl.BlockSpec((1, tm, aw), row_tile),
                   pl.BlockSpec((1, tm, aw), row_tile),
                   pl.BlockSpec((1, aw, tm), lambda bi, i: (bi, 0, i)),
                   pl.BlockSpec((1, tm, aw), row_tile),
                   pl.BlockSpec((1, tm // MOBA_BLOCK, vt_rows, MOBA_BLOCK), lambda bi, i: (bi, i, 0, 0)),
                   pl.BlockSpec((1, tm, pw), row_tile),
                   pl.BlockSpec((1, POOL_HALO, pw), lambda bi, i: (bi, 0, 0))],
        out_shape=[jax.ShapeDtypeStruct((b, t, aw), F32),
                   jax.ShapeDtypeStruct((b, t, aw), F32),
                   jax.ShapeDtypeStruct((b, aw, t), BF16),
                   jax.ShapeDtypeStruct((b, t, aw), BF16),
                   jax.ShapeDtypeStruct((b, nb, vt_rows, MOBA_BLOCK), BF16),
                   jax.ShapeDtypeStruct((b, t, pw), BF16),
                   jax.ShapeDtypeStruct((b, POOL_HALO, pw), F32)],
        scratch_shapes=[pltpu.VMEM((tm + POOL_HALO, pw), F32),
                        pltpu.VMEM((tm + POOL_HALO, pw - gc), F32),
                        pltpu.VMEM((tm + POOL_HALO, pw - 2 * gc), F32),
                        pltpu.VMEM((tm + POOL_HALO, pw - 3 * gc), F32)],
        compiler_params=_params("arbitrary", "arbitrary"),
        name="pre_prompt",
    )(x, mod, g_pre, w_in_b, pool_w_b, pool_scale, cos_t, sin_t)


def _moba_prompt_kernel(qt_ref, kb_ref, vt_ref, o_ref, km_ref, bias_ref, q2_ref, m_ref, acc_ref, *, nb, n_heads):
    i = pl.program_id(1)
    tq = MOBA_BLOCK
    pair_w = 2 * HEAD_DIM

    @pl.when(i == 0)
    def _():
        for n in range(nb):
            km_ref[n:n + 1, :] = jnp.mean(kb_ref[0, n].astype(F32), axis=0, keepdims=True)

    blk_row = lax.broadcasted_iota(jnp.int32, (nb, tq), 0)
    blk_row_f = blk_row.astype(F32)
    fully_past = blk_row < i
    key_row = lax.broadcasted_iota(jnp.int32, (MOBA_BLOCK, tq), 0)
    qry_col = lax.broadcasted_iota(jnp.int32, (MOBA_BLOCK, tq), 1)
    causal = key_row <= qry_col
    pair_row = lax.broadcasted_iota(jnp.int32, (pair_w, tq), 0)

    for h in range(n_heads):
        lo = (h // 2) * pair_w
        mine = (pair_row >= HEAD_DIM) if h % 2 else (pair_row < HEAD_DIM)
        q2 = jnp.where(mine, qt_ref[0, lo:lo + pair_w, :], jnp.zeros((), BF16))
        q2_ref[h] = q2

        gate = jnp.dot(km_ref[:, lo:lo + pair_w].astype(BF16), q2, preferred_element_type=F32)
        gate = jnp.where(fully_past, gate, -jnp.inf)
        chosen = jnp.zeros((nb, tq), jnp.bool_)
        for _ in range(MOBA_TOPK):
            best = jnp.max(gate, axis=0, keepdims=True)
            first = jnp.min(jnp.where(gate == best, blk_row_f, float(nb)), axis=0, keepdims=True)
            pick = blk_row_f == first
            chosen = jnp.logical_or(chosen, pick)
            gate = jnp.where(pick, -jnp.inf, gate)
        bias = jnp.where(jnp.logical_and(chosen, fully_past), 0.0, -jnp.inf).astype(F32)
        for n in range(nb):
            bias_ref[h, n] = bias[n:n + 1, :]

    def scores(j, h):
        lo = (h // 2) * pair_w
        return jnp.dot(kb_ref[0, j, :, lo:lo + pair_w], q2_ref[h], preferred_element_type=F32)

    def all_heads(j, consume):
        ahead = {h: scores(j, h) for h in range(min(SCORE_LOOKAHEAD, n_heads))}
        for h in range(n_heads):
            if h + SCORE_LOOKAHEAD < n_heads:
                ahead[h + SCORE_LOOKAHEAD] = scores(j, h + SCORE_LOOKAHEAD)
            consume(h, ahead.pop(h))

    def first_block(h, s):
        s = jnp.where(causal, s, -jnp.inf)
        m0 = jnp.max(s, axis=0, keepdims=True)
        m_ref[h] = m0
        acc_ref[h] = jnp.dot(vt_ref[0, i, h * V_ROWS:(h + 1) * V_ROWS, :], jnp.exp2(s - m0).astype(BF16),
                             preferred_element_type=F32)

    all_heads(i, first_block)

    def body(j, carry):
        ms, accs = list(carry[0]), list(carry[1])

        def past_block(h, s):
            s = bias_ref[h, j] + s
            m_new = jnp.maximum(ms[h], jnp.max(s, axis=0, keepdims=True))
            p = jnp.exp2(s - m_new).astype(BF16)
            pv = jnp.dot(vt_ref[0, j, h * V_ROWS:(h + 1) * V_ROWS, :], p, preferred_element_type=F32)
            accs[h] = jnp.exp2(ms[h] - m_new) * accs[h] + pv
            ms[h] = m_new

        all_heads(j, past_block)
        return tuple(ms), tuple(accs)

    init = (tuple(m_ref[h] for h in range(n_heads)), tuple(acc_ref[h] for h in range(n_heads)))
    _, accs = lax.fori_loop(0, i, body, init)
    for h in range(n_heads):
        acc_ref[h] = accs[h]

    for h in range(n_heads):
        acc = acc_ref[h]
        o_ref[0, h * HEAD_DIM:(h + 1) * HEAD_DIM, :] = (acc[0:HEAD_DIM] / acc[HEAD_DIM:HEAD_DIM + 1]).astype(BF16)


def _moba_prompt_call(qt, kb4, vt4):
    b, aw, t = qt.shape
    nb = t // MOBA_BLOCK
    n_heads = aw // HEAD_DIM
    kern = functools.partial(_moba_prompt_kernel, nb=nb, n_heads=n_heads)
    per_batch = lambda bi, i: (bi, 0, 0, 0)
    return pl.pallas_call(
        kern,
        grid=(b, nb),
        in_specs=[pl.BlockSpec((1, aw, MOBA_BLOCK), lambda bi, i: (bi, 0, i)),
                  pl.BlockSpec((1, nb, MOBA_BLOCK, aw), per_batch, pipeline_mode=pl.Buffered(1)),
                  pl.BlockSpec((1, nb, n_heads * V_ROWS, MOBA_BLOCK), per_batch, pipeline_mode=pl.Buffered(1))],
        out_specs=pl.BlockSpec((1, aw, MOBA_BLOCK), lambda bi, i: (bi, 0, i)),
        out_shape=jax.ShapeDtypeStruct((b, aw, t), BF16),
        scratch_shapes=[pltpu.VMEM((nb, aw), F32),
                        pltpu.VMEM((n_heads, nb, 1, MOBA_BLOCK), F32),
                        pltpu.VMEM((n_heads, 2 * HEAD_DIM, MOBA_BLOCK), BF16),
                        pltpu.VMEM((n_heads, 1, MOBA_BLOCK), F32),
                        pltpu.VMEM((n_heads, V_ROWS, MOBA_BLOCK), F32)],
        compiler_params=_params("arbitrary", "arbitrary"),
        name="moba_prompt",
    )(qt, kb4, vt4)


def _post_kernel(*refs, tm, d_model, d_ff, att_transposed, per_row_state):
    (x_ref, pool_ref, att_ref, mod_ref, wop_ref, woa_ref, gpost_ref, gpre2_ref, gpost2_ref,
     wup_ref, cw_ref, cb_ref, wdn_ref) = refs[:13]
    if per_row_state:
        st0_ref, st1_ref, y_ref, up_ref = refs[13:]
    else:
        y_ref, ctail_ref, carry, upbuf = refs[13:]
        i = pl.program_id(1)

        @pl.when(i == 0)
        def _():
            carry[...] = jnp.zeros(carry.shape, F32)

    d = d_model
    x = x_ref[0]
    mod = mod_ref[0]
    gate1 = mod[:, 2 * d:3 * d]
    shift2, scale2, gate2 = mod[:, 3 * d:4 * d], mod[:, 4 * d:5 * d], mod[:, 5 * d:6 * d]

    mix = jnp.dot(pool_ref[0], wop_ref[...], preferred_element_type=F32)
    if att_transposed:
        mix += lax.dot_general(att_ref[0], woa_ref[...], (((0,), (0,)), ((), ())), preferred_element_type=F32)
    else:
        mix += jnp.dot(att_ref[0], woa_ref[...], preferred_element_type=F32)
    x1 = x + gate1 * _rms(mix, gpost_ref[...])
    h2 = (_rms(x1, gpre2_ref[...]) * (1.0 + scale2) + shift2).astype(BF16)

    fc = d_ff // FFN_CHUNKS
    ffn = jnp.zeros((tm, d), F32)
    for c in range(FFN_CHUNKS):
        halves = []
        for base in (0, d_ff):
            cols = slice(base + c * fc, base + (c + 1) * fc)
            up = jnp.dot(h2, wup_ref[:, cols], preferred_element_type=F32)
            if per_row_state:
                prev2, prev1 = st0_ref[0, :, cols], st1_ref[0, :, cols]
                up_ref[0, :, cols] = up
            else:
                upbuf[0:SUBLANES, :] = carry[:, cols]
                upbuf[SUBLANES:SUBLANES + tm, :] = up
                prev1 = upbuf[SUBLANES - 1:SUBLANES - 1 + tm, :]
                prev2 = upbuf[SUBLANES - 2:SUBLANES - 2 + tm, :]
                carry[:, cols] = upbuf[tm:tm + SUBLANES, :]
            halves.append(cb_ref[:, cols] + cw_ref[0:1, cols] * prev2 + cw_ref[1:2, cols] * prev1
                          + cw_ref[2:3, cols] * up)
        a, g = halves
        act = (a * jax.nn.sigmoid(a) * g).astype(BF16)
        ffn += jnp.dot(act, wdn_ref[c * fc:(c + 1) * fc, :], preferred_element_type=F32)
    y_ref[0] = x1 + gate2 * _rms(ffn, gpost2_ref[...])
    if not per_row_state:
        ctail_ref[0] = carry[...]


def _post_call(x, pool_o, att, mod, w, *, att_transposed, state=None):
    b, t, d = x.shape
    pw = pool_o.shape[2]
    aw = w["w_out_a"].shape[0]
    d_ff = w["w_down"].shape[0]
    per_row = state is not None
    tm = t if per_row else min(POST_TILE, t)
    assert t % tm == 0 and d_ff % (FFN_CHUNKS * LANES) == 0 and CONV_WIDTH == 3
    fc = d_ff // FFN_CHUNKS
    kern = functools.partial(_post_kernel, tm=tm, d_model=d, d_ff=d_ff, att_transposed=att_transposed,
                             per_row_state=per_row)
    row_tile = lambda bi, i: (bi, i, 0)
    const2 = lambda bi, i: (0, 0)
    mod_rows = mod.shape[1]
    in_specs = [pl.BlockSpec((1, tm, d), row_tile),
                pl.BlockSpec((1, tm, pw), row_tile),
                (pl.BlockSpec((1, aw, tm), lambda bi, i: (bi, 0, i)) if att_transposed
                 else pl.BlockSpec((1, tm, aw), row_tile)),
                pl.BlockSpec((1, mod_rows, mod.shape[2]), (row_tile if mod_rows > 1 else lambda bi, i: (bi, 0, 0))),
                _resident((pw, d), const2),
                _resident((aw, d), const2),
                _resident((1, d), const2),
                _resident((1, d), const2),
                _resident((1, d), const2),
                _resident((d, 2 * d_ff), const2),
                _resident((CONV_WIDTH, 2 * d_ff), const2),
                _resident((1, 2 * d_ff), const2),
                _resident((d_ff, d), const2)]
    args = [x, pool_o, att, mod, w["w_out_p"], w["w_out_a"], w["g_mix_post"], w["g_ffn_pre"], w["g_ffn_post"],
            w["w_up"], w["conv_w"], w["conv_b"], w["w_down"]]
    if per_row:
        in_specs += [pl.BlockSpec((1, tm, 2 * d_ff), row_tile)] * 2
        args += list(state)
        out_specs = [pl.BlockSpec((1, tm, d), row_tile), pl.BlockSpec((1, tm, 2 * d_ff), row_tile)]
        out_shape = [jax.ShapeDtypeStruct((b, t, d), F32), jax.ShapeDtypeStruct((b, t, 2 * d_ff), F32)]
        scratch = []
    else:
        out_specs = [pl.BlockSpec((1, tm, d), row_tile),
                     pl.BlockSpec((1, SUBLANES, 2 * d_ff), lambda bi, i: (bi, 0, 0))]
        out_shape = [jax.ShapeDtypeStruct((b, t, d), F32), jax.ShapeDtypeStruct((b, SUBLANES, 2 * d_ff), F32)]
        scratch = [pltpu.VMEM((SUBLANES, 2 * d_ff), F32), pltpu.VMEM((tm + SUBLANES, fc), F32)]
    return pl.pallas_call(
        kern,
        grid=(b, t // tm),
        in_specs=in_specs,
        out_specs=out_specs,
        out_shape=out_shape,
        scratch_shapes=scratch,
        compiler_params=_params("arbitrary", "arbitrary"),
        name="post_sample" if per_row else "post_prompt",
    )(*args)


def _pre_sample_kernel(x_ref, mod_ref, g_ref, win_ref, poolw_ref, pscale_ref, cos_ref, sin_ref, st_ref,
                       u_ref, q_ref, k_ref, v_ref, pool_ref, *, d_model, pw, aw, pos0):
    gc = pw // len(POOL_WINDOWS)
    x = x_ref[...]
    mod = mod_ref[...]
    shift, scale = mod[:, 0:d_model], mod[:, d_model:2 * d_model]
    h = _rms(x, g_ref[...]) * (1.0 + scale) + shift
    proj = jnp.dot(h.astype(BF16), win_ref[...], preferred_element_type=F32)
    u = proj[:, 0:pw]
    cos, sin_signed = cos_ref[...], sin_ref[...]
    u_ref[...] = u
    q_ref[...] = _rope(proj[:, pw:pw + aw], cos, sin_signed)
    k_ref[...] = _rope(proj[:, pw + aw:pw + 2 * aw], cos, sin_signed)
    v_ref[...] = proj[:, pw + 2 * aw:pw + 3 * aw]

    st = st_ref[...]
    pooled = []
    for g, w in enumerate(POOL_WINDOWS):
        cols = slice(g * gc, (g + 1) * gc)
        win = u[:, cols] + jnp.sum(st[:, POOL_STATE - (w - 1):, cols], axis=1)
        pooled.append(win / float(min(w, pos0 + 1)) - u[:, cols])
    pool_ref[...] = _pool_project(jnp.concatenate(pooled, axis=1), poolw_ref, pscale_ref[...]).astype(BF16)


def _pre_sample_call(x, mod, g_pre, w_in_b, pool_w_b, pool_scale, cos_t, sin_t, state_pool, *, pw, aw, pos0):
    rows, d = x.shape
    kern = functools.partial(_pre_sample_kernel, d_model=d, pw=pw, aw=aw, pos0=pos0)
    return pl.pallas_call(
        kern,
        out_shape=[jax.ShapeDtypeStruct((rows, pw), F32),
                   jax.ShapeDtypeStruct((rows, aw), F32),
                   jax.ShapeDtypeStruct((rows, aw), F32),
                   jax.ShapeDtypeStruct((rows, aw), F32),
                   jax.ShapeDtypeStruct((rows, pw), BF16)],
        compiler_params=pltpu.CompilerParams(vmem_limit_bytes=VMEM_LIMIT_BYTES),
        name="pre_sample",
    )(x, mod, g_pre, w_in_b, pool_w_b, pool_scale, cos_t, sin_t, state_pool)


def _moba_sample_kernel(pt_ref, q_ref, kn_ref, vn_ref, *refs, n_pages, page_size):
    del pt_ref
    k_refs, v_refs, o_ref = refs[:n_pages], refs[n_pages:2 * n_pages], refs[2 * n_pages]
    n_heads = k_refs[0].shape[0]
    aw = n_heads * HEAD_DIM
    ppb = MOBA_BLOCK // page_size
    n_blocks = n_pages // ppb

    head_row = lax.broadcasted_iota(jnp.int32, (n_heads, aw), 0)
    head_of_lane = lax.broadcasted_iota(jnp.int32, (n_heads, aw), 1) // HEAD_DIM
    own_lanes = head_row == head_of_lane
    qbd = jnp.where(own_lanes, q_ref[0] * (HEAD_DIM ** -0.5 * LOG2E), 0.0)
    qbd_b = qbd.astype(BF16)

    scores = []
    for p in range(n_pages):
        kt = k_refs[p][...].reshape(aw, page_size).astype(BF16)
        scores.append(jnp.dot(qbd_b, kt, preferred_element_type=F32))

    gates = [sum(jnp.sum(scores[p], axis=-1, keepdims=True) for p in range(n * ppb, (n + 1) * ppb))
             for n in range(n_blocks)]
    chosen = []
    for n in range(n_blocks):
        beaten = jnp.zeros((n_heads, 1), jnp.int32)
        for m in range(n_blocks):
            if m != n:
                wins = (gates[m] >= gates[n]) if m < n else (gates[m] > gates[n])
                beaten = beaten + wins.astype(jnp.int32)
        chosen.append(beaten < MOBA_TOPK)

    s_new = jnp.sum(qbd * kn_ref[0], axis=-1, keepdims=True)
    m = s_new
    for p in range(n_pages):
        scores[p] = jnp.where(chosen[p // ppb], scores[p], -jnp.inf)
        m = jnp.maximum(m, jnp.max(scores[p], axis=-1, keepdims=True))
    p_new = jnp.exp2(s_new - m)
    l = p_new
    acc = p_new * vn_ref[0]
    for p in range(n_pages):
        w = jnp.exp2(scores[p] - m)
        l = l + jnp.sum(w, axis=-1, keepdims=True)
        vt = v_refs[p][...].reshape(aw, page_size).astype(BF16)
        acc = acc + lax.dot_general(w.astype(BF16), vt, (((1,), (1,)), ((), ())), preferred_element_type=F32)
    o_ref[0] = jnp.sum(jnp.where(own_lanes, acc / l, 0.0), axis=0, keepdims=True)


def _moba_sample_call(page_table, q, k_new, v_new, cache_kt, cache_vt):
    n_seq, n_pages = page_table.shape
    _, n_heads, dh, page_size = cache_kt.shape
    aw = n_heads * dh
    assert dh == HEAD_DIM and MOBA_BLOCK % page_size == 0 and (n_pages * page_size) % MOBA_BLOCK == 0
    kern = functools.partial(_moba_sample_kernel, n_pages=n_pages, page_size=page_size)
    tok = pl.BlockSpec((1, 1, aw), lambda s, pt: (s, 0, 0))

    def page_spec(p):
        return pl.BlockSpec((None, n_heads, dh, page_size), lambda s, pt: (pt[s, p], 0, 0, 0))

    pages = [page_spec(p) for p in range(n_pages)]
    return pl.pallas_call(
        kern,
        grid_spec=pltpu.PrefetchScalarGridSpec(
            num_scalar_prefetch=1,
            grid=(n_seq,),
            in_specs=[tok, tok, tok] + pages + pages,
            out_specs=tok),
        out_shape=jax.ShapeDtypeStruct((n_seq, 1, aw), F32),
        compiler_params=_params("arbitrary"),
        name="moba_sample",
    )(page_table, q, k_new, v_new, *([cache_kt] * n_pages), *([cache_vt] * n_pages))


def _rope_tables(positions):
    half = HEAD_DIM // 2
    inv = ROPE_THETA ** (-jnp.arange(half, dtype=F32) / half)
    ang = positions.astype(F32)[:, None] * inv[None, :]
    cos, sin = jnp.cos(ang), jnp.sin(ang)
    reps = LANES // HEAD_DIM
    return (jnp.tile(jnp.concatenate([cos, cos], axis=1), (1, reps)),
            jnp.tile(jnp.concatenate([-sin, sin], axis=1), (1, reps)))


def kernel(x_prompt, x_sample, c_prompt, c_sample, cache_k, cache_v, page_table, state_pool, state_conv, w_ada, b_ada, g_mix_pre, g_mix_post, w_in, pool_w, pool_scale, w_out, g_ffn_pre, g_ffn_post, w_up, conv_w, conv_b, w_down):
    depth = w_ada.shape[0]
    b, t, d = x_prompt.shape
    n_seq, dec_t, _ = x_sample.shape
    n_pages = page_table.shape[1]
    page_size, n_heads = cache_k.shape[2], cache_k.shape[3]
    past_len = n_pages * page_size
    aw = n_heads * HEAD_DIM
    pw = state_pool.shape[3]
    d_ff = w_down.shape[1]
    assert dec_t == 1 and t % MOBA_BLOCK == 0 and t % PRE_TILE == 0 and pw == d - aw
    assert (pw // len(POOL_WINDOWS)) % LANES == 0 and w_in.shape[2] == pw + 3 * aw

    cos_p, sin_p = _rope_tables(jnp.arange(t, dtype=jnp.int32))
    cos_s, sin_s = _rope_tables(past_len + jnp.arange(dec_t, dtype=jnp.int32))

    y_p, y_s = x_prompt, x_sample.reshape(1, n_seq, d)
    outs = [[] for _ in range(8)]
    for l in range(depth):
        w = dict(w_out_p=w_out[l, :pw].astype(BF16), w_out_a=w_out[l, pw:].astype(BF16),
                 g_mix_post=g_mix_post[l].reshape(1, d), g_ffn_pre=g_ffn_pre[l].reshape(1, d),
                 g_ffn_post=g_ffn_post[l].reshape(1, d), w_up=w_up[l].astype(BF16), conv_w=conv_w[l],
                 conv_b=conv_b[l].reshape(1, 2 * d_ff), w_down=w_down[l].astype(BF16))
        g_pre = g_mix_pre[l].reshape(1, d)
        w_in_b = w_in[l].astype(BF16)
        pool_w_b = pool_w[l].astype(BF16)
        pscale = pool_scale[l].reshape(1, pw)

        mod = _mod_call(jnp.concatenate([c_prompt, c_sample], axis=0), w_ada[l], b_ada[l])
        mod_p, mod_s = mod[:b].reshape(b, 1, 6 * d), mod[b:].reshape(1, n_seq, 6 * d)

        k_p, v_p, qt, kb, vt4, pool_p, utail = _pre_prompt_call(
            y_p, mod_p, g_pre, w_in_b, pool_w_b, pscale, cos_p, sin_p, pw=pw, aw=aw)
        att_t = _moba_prompt_call(qt, kb.reshape(b, t // MOBA_BLOCK, MOBA_BLOCK, aw), vt4)
        y_p, ctail = _post_call(y_p, pool_p, att_t, mod_p, w, att_transposed=True)

        u_s, q_s, k_s, v_s, pool_s = _pre_sample_call(
            y_s[0], mod_s[0], g_pre, w_in_b, pool_w_b, pscale, cos_s, sin_s, state_pool[l],
            pw=pw, aw=aw, pos0=past_len)
        tok = lambda a: a.reshape(n_seq, 1, aw)
        att_s = _moba_sample_call(page_table, tok(q_s), tok(k_s), tok(v_s),
                                  jnp.transpose(cache_k[l], (0, 2, 3, 1)), jnp.transpose(cache_v[l], (0, 2, 3, 1)))
        y_s, up_s = _post_call(y_s, pool_s.reshape(1, n_seq, pw), att_s.reshape(1, n_seq, aw).astype(BF16), mod_s, w,
                               att_transposed=False,
                               state=(state_conv[l, :, 0].reshape(1, n_seq, 2 * d_ff),
                                      state_conv[l, :, 1].reshape(1, n_seq, 2 * d_ff)))

        outs[0].append(k_p.reshape(b, t, n_heads, HEAD_DIM))
        outs[1].append(v_p.reshape(b, t, n_heads, HEAD_DIM))
        outs[2].append(utail[:, POOL_HALO - POOL_STATE:])
        outs[3].append(ctail[:, SUBLANES - (CONV_WIDTH - 1):])
        outs[4].append(k_s.reshape(n_seq, dec_t, n_heads, HEAD_DIM))
        outs[5].append(v_s.reshape(n_seq, dec_t, n_heads, HEAD_DIM))
        outs[6].append(jnp.concatenate([state_pool[l, :, 1:], u_s[:, None, :]], axis=1))
        outs[7].append(jnp.concatenate([state_conv[l, :, 1:], up_s[0][:, None, :]], axis=1))
    stacked = [jnp.stack(o) for o in outs]
    return (y_p, y_s.reshape(n_seq, dec_t, d), *stacked)
```

```python
import functools
import math

import jax
import jax.numpy as jnp
from jax import lax
from jax.experimental import pallas as pl
from jax.experimental.pallas import tpu as pltpu

HEAD_DIM = 64
MOBA_BLOCK = 256
MOBA_TOPK = 3
POOL_WINDOWS = (2, 4, 8, 16)
POOL_STATE = max(POOL_WINDOWS) - 1
CONV_WIDTH = 3
ROPE_THETA = 10000.0
RMS_EPS = 1e-6

LANES = 128
SUBLANES = 8
VMEM_LIMIT_BYTES = 56 * 1024 * 1024

BF16_ROWS = 16
V_ROWS = HEAD_DIM + BF16_ROWS

POOL_HALO = 16
PRE_TILE = 512
POST_TILE = 512
FFN_CHUNKS = 11
FFN_LOOKAHEAD = 2
FFN_DOWN_SPLITS = 2
SCORE_LOOKAHEAD = 3

F32 = jnp.float32
BF16 = jnp.bfloat16
LOG2E = math.log2(math.e)


def _params(*semantics):
    return pltpu.CompilerParams(dimension_semantics=semantics, vmem_limit_bytes=VMEM_LIMIT_BYTES)


def _resident(shape, index_map):
    return pl.BlockSpec(shape, index_map, pipeline_mode=pl.Buffered(1))


def _rms(x, g):
    ms = jnp.mean(x * x, axis=-1, keepdims=True)
    return x * lax.rsqrt(ms + RMS_EPS) * g


def _rope(x, cos, sin_signed):
    lane = lax.broadcasted_iota(jnp.int32, (1, LANES), 1)
    first_half = (lane % HEAD_DIM) < HEAD_DIM // 2
    half = HEAD_DIM // 2
    outs = []
    for c in range(x.shape[1] // LANES):
        blk = x[:, c * LANES:(c + 1) * LANES]
        partner = jnp.where(first_half, pltpu.roll(blk, LANES - half, 1), pltpu.roll(blk, half, 1))
        outs.append(blk * cos + partner * sin_signed)
    return jnp.concatenate(outs, axis=1)


def _pool_project(pooled, poolw_ref, pscale):
    n_groups, gc, _ = poolw_ref.shape
    outs = [jnp.dot(pooled[:, g * gc:(g + 1) * gc].astype(BF16), poolw_ref[g], preferred_element_type=F32)
            for g in range(n_groups)]
    return jnp.concatenate(outs, axis=1) * pscale


def _mod_kernel(c_ref, w_ref, b_ref, o_ref):
    c = c_ref[...]
    a = (c * jax.nn.sigmoid(c)).astype(BF16)
    o_ref[...] = jnp.dot(a, w_ref[...].astype(BF16), preferred_element_type=F32) + b_ref[...]


def _mod_call(c, w_ada, b_ada):
    rows, d = c.shape
    n_out = w_ada.shape[1]
    return pl.pallas_call(
        _mod_kernel,
        grid=(n_out // d,),
        in_specs=[pl.BlockSpec((rows, d), lambda j: (0, 0)),
                  pl.BlockSpec((d, d), lambda j: (0, j)),
                  pl.BlockSpec((1, d), lambda j: (0, j))],
        out_specs=pl.BlockSpec((rows, d), lambda j: (0, j)),
        out_shape=jax.ShapeDtypeStruct((rows, n_out), F32),
        compiler_params=_params("arbitrary"),
        name="adaln_mod",
    )(c, w_ada, b_ada.reshape(1, n_out))


def _pre_prompt_kernel(x_ref, mod_ref, g_ref, win_ref, poolw_ref, pscale_ref, cos_ref, sin_ref,
                       k_ref, v_ref, qt_ref, kb_ref, vt_ref, pool_ref, utail_ref,
                       ubuf, s2buf, s4buf, s8buf, *, tm, d_model, pw, aw, q_scale):
    i = pl.program_id(1)
    halo = POOL_HALO
    gc = pw // len(POOL_WINDOWS)
    bufs = (ubuf, s2buf, s4buf, s8buf)

    @pl.when(i == 0)
    def _():
        for buf in bufs:
            buf[0:halo, :] = jnp.zeros((halo, buf.shape[1]), F32)

    x = x_ref[0]
    mod = mod_ref[0]
    shift, scale = mod[:, 0:d_model], mod[:, d_model:2 * d_model]
    h = _rms(x, g_ref[...]) * (1.0 + scale) + shift
    proj = jnp.dot(h.astype(BF16), win_ref[...], preferred_element_type=F32)
    u = proj[:, 0:pw]
    cos, sin_signed = cos_ref[...], sin_ref[...]
    q = _rope(proj[:, pw:pw + aw], cos, sin_signed)
    k = _rope(proj[:, pw + aw:pw + 2 * aw], cos, sin_signed)
    v = proj[:, pw + 2 * aw:pw + 3 * aw]

    k_ref[0] = k
    v_ref[0] = v
    kb_ref[0] = k.astype(BF16)
    qt_ref[0] = (q * q_scale).T.astype(BF16)
    ones_pad = jnp.where(lax.broadcasted_iota(jnp.int32, (BF16_ROWS, MOBA_BLOCK), 0) == 0, 1.0, 0.0).astype(BF16)
    for r in range(tm // MOBA_BLOCK):
        vt = v[r * MOBA_BLOCK:(r + 1) * MOBA_BLOCK, :].T.astype(BF16)
        for hd in range(aw // HEAD_DIM):
            vt_ref[0, r, hd * V_ROWS:hd * V_ROWS + HEAD_DIM, :] = vt[hd * HEAD_DIM:(hd + 1) * HEAD_DIM, :]
            vt_ref[0, r, hd * V_ROWS + HEAD_DIM:(hd + 1) * V_ROWS, :] = ones_pad

    ubuf[halo:halo + tm, :] = u
    s2 = u + ubuf[halo - 1:halo - 1 + tm, :]
    s2buf[halo:halo + tm, :] = s2[:, gc:]
    s4 = s2[:, gc:] + s2buf[halo - 2:halo - 2 + tm, :]
    s4buf[halo:halo + tm, :] = s4[:, gc:]
    s8 = s4[:, gc:] + s4buf[halo - 4:halo - 4 + tm, :]
    s8buf[halo:halo + tm, :] = s8[:, gc:]
    s16 = s8[:, gc:] + s8buf[halo - 8:halo - 8 + tm, :]
    wins = (s2[:, 0:gc], s4[:, 0:gc], s8[:, 0:gc], s16)
    pos1 = (i * tm + 1 + lax.broadcasted_iota(jnp.int32, (tm, 1), 0)).astype(F32)
    pooled = jnp.concatenate(
        [wins[g] / jnp.minimum(float(w), pos1) - u[:, g * gc:(g + 1) * gc] for g, w in enumerate(POOL_WINDOWS)],
        axis=1)
    pool_ref[0] = _pool_project(pooled, poolw_ref, pscale_ref[...]).astype(BF16)

    for buf in bufs:
        buf[0:halo, :] = buf[tm:tm + halo, :]
    utail_ref[0] = ubuf[0:halo, :]


def _pre_prompt_call(x, mod, g_pre, w_in_b, pool_w_b, pool_scale, cos_t, sin_t, *, pw, aw):
    b, t, d = x.shape
    tm = min(PRE_TILE, t)
    nb = t // MOBA_BLOCK
    gc = pw // len(POOL_WINDOWS)
    in_w = w_in_b.shape[1]
    vt_rows = (aw // HEAD_DIM) * V_ROWS
    kern = functools.partial(_pre_prompt_kernel, tm=tm, d_model=d, pw=pw, aw=aw,
                             q_scale=HEAD_DIM ** -0.5 * LOG2E)
    row_tile = lambda bi, i: (bi, i, 0)
    const2 = lambda bi, i: (0, 0)
    return pl.pallas_call(
        kern,
        grid=(b, t // tm),
        in_specs=[pl.BlockSpec((1, tm, d), row_tile),
                  pl.BlockSpec((1, 1, mod.shape[2]), lambda bi, i: (bi, 0, 0)),
                  _resident((1, d), const2),
                  _resident((d, in_w), const2),
                  _resident(pool_w_b.shape, lambda bi, i: (0, 0, 0)),
                  _resident((1, pw), const2),
                  pl.BlockSpec((tm, LANES), lambda bi, i: (i, 0)),
                  pl.BlockSpec((tm, LANES), lambda bi, i: (i, 0))],
        out_specs=[pl.BlockSpec((1, tm, aw), row_tile),
                   pl.BlockSpec((1, tm, aw), row_tile),
                   pl.BlockSpec((1, aw, tm), lambda bi, i: (bi, 0, i)),
                   pl.BlockSpec((1, tm, aw), row_tile),
                   pl.BlockSpec((1, tm // MOBA_BLOCK, vt_rows, MOBA_BLOCK), lambda bi, i: (bi, i, 0, 0)),
                   pl.BlockSpec((1, tm, pw), row_tile),
                   pl.BlockSpec((1, POOL_HALO, pw), lambda bi, i: (bi, 0, 0))],
        out_shape=[jax.ShapeDtypeStruct((b, t, aw), F32),
                   jax.ShapeDtypeStruct((b, t, aw), F32),
                   jax.ShapeDtypeStruct((b, aw, t), BF16),
                   jax.ShapeDtypeStruct((b, t, aw), BF16),
                   jax.ShapeDtypeStruct((b, nb, vt_rows, MOBA_BLOCK), BF16),
                   jax.ShapeDtypeStruct((b, t, pw), BF16),
                   jax.ShapeDtypeStruct((b, POOL_HALO, pw), F32)],
        scratch_shapes=[pltpu.VMEM((tm + POOL_HALO, pw), F32),
                        pltpu.VMEM((tm + POOL_HALO, pw - gc), F32),
                        pltpu.VMEM((tm + POOL_HALO, pw - 2 * gc), F32),
                        pltpu.VMEM((tm + POOL_HALO, pw - 3 * gc), F32)],
        compiler_params=_params("arbitrary", "arbitrary"),
        name="pre_prompt",
    )(x, mod, g_pre, w_in_b, pool_w_b, pool_scale, cos_t, sin_t)


def _moba_prompt_kernel(qt_ref, kb_ref, vt_ref, o_ref, km_ref, bias_ref, q2_ref, m_ref, acc_ref, *, nb, n_heads):
    i = pl.program_id(1)
    tq = MOBA_BLOCK
    pair_w = 2 * HEAD_DIM

    @pl.when(i == 0)
    def _():
        for n in range(nb):
            km_ref[n:n + 1, :] = jnp.mean(kb_ref[0, n].astype(F32), axis=0, keepdims=True)

    blk_row = lax.broadcasted_iota(jnp.int32, (nb, tq), 0)
    blk_row_f = blk_row.astype(F32)
    fully_past = blk_row < i
    key_row = lax.broadcasted_iota(jnp.int32, (MOBA_BLOCK, tq), 0)
    qry_col = lax.broadcasted_iota(jnp.int32, (MOBA_BLOCK, tq), 1)
    causal = key_row <= qry_col
    pair_row = lax.broadcasted_iota(jnp.int32, (pair_w, tq), 0)

    for h in range(n_heads):
        lo = (h // 2) * pair_w
        mine = (pair_row >= HEAD_DIM) if h % 2 else (pair_row < HEAD_DIM)
        q2 = jnp.where(mine, qt_ref[0, lo:lo + pair_w, :], jnp.zeros((), BF16))
        q2_ref[h] = q2

        gate = jnp.dot(km_ref[:, lo:lo + pair_w].astype(BF16), q2, preferred_element_type=F32)
        gate = jnp.where(fully_past, gate, -jnp.inf)
        chosen = jnp.zeros((nb, tq), jnp.bool_)
        for _ in range(MOBA_TOPK):
            best = jnp.max(gate, axis=0, keepdims=True)
            first = jnp.min(jnp.where(gate == best, blk_row_f, float(nb)), axis=0, keepdims=True)
            pick = blk_row_f == first
            chosen = jnp.logical_or(chosen, pick)
            gate = jnp.where(pick, -jnp.inf, gate)
        bias = jnp.where(jnp.logical_and(chosen, fully_past), 0.0, -jnp.inf).astype(F32)
        for n in range(nb):
            bias_ref[h, n] = bias[n:n + 1, :]

    def scores(js, h):
        lo = (h // 2) * pair_w
        return [jnp.dot(kb_ref[0, j, :, lo:lo + pair_w], q2_ref[h], preferred_element_type=F32) for j in js]

    def all_heads(js, consume):
        ahead = {h: scores(js, h) for h in range(min(SCORE_LOOKAHEAD, n_heads))}
        for h in range(n_heads):
            if h + SCORE_LOOKAHEAD < n_heads:
                ahead[h + SCORE_LOOKAHEAD] = scores(js, h + SCORE_LOOKAHEAD)
            consume(h, ahead.pop(h))

    def values(j, h):
        return vt_ref[0, j, h * V_ROWS:(h + 1) * V_ROWS, :]

    def first_block(h, ss):
        s = jnp.where(causal, ss[0], -jnp.inf)
        m0 = jnp.max(s, axis=0, keepdims=True)
        m_ref[h] = m0
        acc_ref[h] = jnp.dot(values(i, h), jnp.exp2(s - m0).astype(BF16), preferred_element_type=F32)

    all_heads([i], first_block)

    def online_update(js, h, ss, m_old, acc_old):
        ss = [bias_ref[h, j] + s for j, s in zip(js, ss)]
        m_new = m_old
        for s in ss:
            m_new = jnp.maximum(m_new, jnp.max(s, axis=0, keepdims=True))
        pv = None
        for j, s in zip(js, ss):
            d = jnp.dot(values(j, h), jnp.exp2(s - m_new).astype(BF16), preferred_element_type=F32)
            pv = d if pv is None else pv + d
        return m_new, jnp.exp2(m_old - m_new) * acc_old + pv

    @pl.when(i % 2 == 1)
    def _():
        def single(h, ss):
            m_ref[h], acc_ref[h] = online_update([i - 1], h, ss, m_ref[h], acc_ref[h])

        all_heads([i - 1], single)

    def pair_body(t, carry):
        js = [2 * t, 2 * t + 1]

        def pair(h, ss):
            m_ref[h], acc_ref[h] = online_update(js, h, ss, m_ref[h], acc_ref[h])

        all_heads(js, pair)
        return carry

    lax.fori_loop(0, i // 2, pair_body, 0)

    for h in range(n_heads):
        acc = acc_ref[h]
        o_ref[0, h * HEAD_DIM:(h + 1) * HEAD_DIM, :] = (acc[0:HEAD_DIM] / acc[HEAD_DIM:HEAD_DIM + 1]).astype(BF16)


def _moba_prompt_call(qt, kb4, vt4):
    b, aw, t = qt.shape
    nb = t // MOBA_BLOCK
    n_heads = aw // HEAD_DIM
    kern = functools.partial(_moba_prompt_kernel, nb=nb, n_heads=n_heads)
    per_batch = lambda bi, i: (bi, 0, 0, 0)
    return pl.pallas_call(
        kern,
        grid=(b, nb),
        in_specs=[pl.BlockSpec((1, aw, MOBA_BLOCK), lambda bi, i: (bi, 0, i)),
                  pl.BlockSpec((1, nb, MOBA_BLOCK, aw), per_batch, pipeline_mode=pl.Buffered(1)),
                  pl.BlockSpec((1, nb, n_heads * V_ROWS, MOBA_BLOCK), per_batch, pipeline_mode=pl.Buffered(1))],
        out_specs=pl.BlockSpec((1, aw, MOBA_BLOCK), lambda bi, i: (bi, 0, i)),
        out_shape=jax.ShapeDtypeStruct((b, aw, t), BF16),
        scratch_shapes=[pltpu.VMEM((nb, aw), F32),
                        pltpu.VMEM((n_heads, nb, 1, MOBA_BLOCK), F32),
                        pltpu.VMEM((n_heads, 2 * HEAD_DIM, MOBA_BLOCK), BF16),
                        pltpu.VMEM((n_heads, 1, MOBA_BLOCK), F32),
                        pltpu.VMEM((n_heads, V_ROWS, MOBA_BLOCK), F32)],
        compiler_params=_params("arbitrary", "arbitrary"),
        name="moba_prompt",
    )(qt, kb4, vt4)


def _post_kernel(*refs, tm, d_model, d_ff, att_transposed, per_row_state):
    (x_ref, pool_ref, att_ref, mod_ref, wop_ref, woa_ref, gpost_ref, gpre2_ref, gpost2_ref,
     wup_ref, cw_ref, cb_ref, wdn_ref) = refs[:13]
    if per_row_state:
        st0_ref, st1_ref, y_ref, up_ref, actbuf = refs[13:]
    else:
        y_ref, ctail_ref, carry, upbuf, actbuf = refs[13:]
        i = pl.program_id(1)

        @pl.when(i == 0)
        def _():
            carry[...] = jnp.zeros(carry.shape, F32)

    d = d_model
    x = x_ref[0]
    mod = mod_ref[0]
    gate1 = mod[:, 2 * d:3 * d]
    shift2, scale2, gate2 = mod[:, 3 * d:4 * d], mod[:, 4 * d:5 * d], mod[:, 5 * d:6 * d]

    mix = jnp.dot(pool_ref[0], wop_ref[...], preferred_element_type=F32)
    if att_transposed:
        mix += lax.dot_general(att_ref[0], woa_ref[...], (((0,), (0,)), ((), ())), preferred_element_type=F32)
    else:
        mix += jnp.dot(att_ref[0], woa_ref[...], preferred_element_type=F32)
    x1 = x + gate1 * _rms(mix, gpost_ref[...])
    h2 = (_rms(x1, gpre2_ref[...]) * (1.0 + scale2) + shift2).astype(BF16)

    fc = d_ff // FFN_CHUNKS

    def up_proj(c):
        out = []
        for base in (0, d_ff):
            cols = slice(base + c * fc, base + (c + 1) * fc)
            out.append((cols, jnp.dot(h2, wup_ref[:, cols], preferred_element_type=F32)))
        return out

    def conv_act(c, ups):
        halves = []
        for half, (cols, up) in enumerate(ups):
            if per_row_state:
                prev2, prev1 = st0_ref[0, :, cols], st1_ref[0, :, cols]
                up_ref[0, :, cols] = up
            else:
                buf = upbuf.at[(2 * c + half) % upbuf.shape[0]]
                buf[0:SUBLANES, :] = carry[:, cols]
                buf[SUBLANES:SUBLANES + tm, :] = up
                prev1 = buf[SUBLANES - 1:SUBLANES - 1 + tm, :]
                prev2 = buf[SUBLANES - 2:SUBLANES - 2 + tm, :]
                carry[:, cols] = buf[tm:tm + SUBLANES, :]
            halves.append(cb_ref[:, cols] + cw_ref[0:1, cols] * prev2 + cw_ref[1:2, cols] * prev1
                          + cw_ref[2:3, cols] * up)
        a, g = halves
        actbuf[:, c * fc:(c + 1) * fc] = (a * jax.nn.sigmoid(a) * g).astype(BF16)

    split_at = [((s + 1) * FFN_CHUNKS) // FFN_DOWN_SPLITS for s in range(FFN_DOWN_SPLITS)]
    ahead = {c: up_proj(c) for c in range(min(FFN_LOOKAHEAD, FFN_CHUNKS))}
    ffn, done = None, 0
    for c in range(FFN_CHUNKS):
        if c + FFN_LOOKAHEAD < FFN_CHUNKS:
            ahead[c + FFN_LOOKAHEAD] = up_proj(c + FFN_LOOKAHEAD)
        conv_act(c, ahead.pop(c))
        if c + 1 in split_at:
            rows = slice(done * fc, (c + 1) * fc)
            part = jnp.dot(actbuf[:, rows], wdn_ref[rows, :], preferred_element_type=F32)
            ffn = part if ffn is None else ffn + part
            done = c + 1
    y_ref[0] = x1 + gate2 * _rms(ffn, gpost2_ref[...])
    if not per_row_state:
        ctail_ref[0] = carry[...]


def _post_call(x, pool_o, att, mod, w, *, att_transposed, state=None):
    b, t, d = x.shape
    pw = pool_o.shape[2]
    aw = w["w_out_a"].shape[0]
    d_ff = w["w_down"].shape[0]
    per_row = state is not None
    tm = t if per_row else min(POST_TILE, t)
    assert t % tm == 0 and d_ff % (FFN_CHUNKS * LANES) == 0 and CONV_WIDTH == 3
    fc = d_ff // FFN_CHUNKS
    kern = functools.partial(_post_kernel, tm=tm, d_model=d, d_ff=d_ff, att_transposed=att_transposed,
                             per_row_state=per_row)
    row_tile = lambda bi, i: (bi, i, 0)
    const2 = lambda bi, i: (0, 0)
    mod_rows = mod.shape[1]
    in_specs = [pl.BlockSpec((1, tm, d), row_tile),
                pl.BlockSpec((1, tm, pw), row_tile),
                (pl.BlockSpec((1, aw, tm), lambda bi, i: (bi, 0, i)) if att_transposed
                 else pl.BlockSpec((1, tm, aw), row_tile)),
                pl.BlockSpec((1, mod_rows, mod.shape[2]), (row_tile if mod_rows > 1 else lambda bi, i: (bi, 0, 0))),
                _resident((pw, d), const2),
                _resident((aw, d), const2),
                _resident((1, d), const2),
                _resident((1, d), const2),
                _resident((1, d), const2),
                _resident((d, 2 * d_ff), const2),
                _resident((CONV_WIDTH, 2 * d_ff), const2),
                _resident((1, 2 * d_ff), const2),
                _resident((d_ff, d), const2)]
    args = [x, pool_o, att, mod, w["w_out_p"], w["w_out_a"], w["g_mix_post"], w["g_ffn_pre"], w["g_ffn_post"],
            w["w_up"], w["conv_w"], w["conv_b"], w["w_down"]]
    if per_row:
        in_specs += [pl.BlockSpec((1, tm, 2 * d_ff), row_tile)] * 2
        args += list(state)
        out_specs = [pl.BlockSpec((1, tm, d), row_tile), pl.BlockSpec((1, tm, 2 * d_ff), row_tile)]
        out_shape = [jax.ShapeDtypeStruct((b, t, d), F32), jax.ShapeDtypeStruct((b, t, 2 * d_ff), F32)]
        scratch = []
    else:
        out_specs = [pl.BlockSpec((1, tm, d), row_tile),
                     pl.BlockSpec((1, SUBLANES, 2 * d_ff), lambda bi, i: (bi, 0, 0))]
        out_shape = [jax.ShapeDtypeStruct((b, t, d), F32), jax.ShapeDtypeStruct((b, SUBLANES, 2 * d_ff), F32)]
        scratch = [pltpu.VMEM((SUBLANES, 2 * d_ff), F32),
                   pltpu.VMEM((4, tm + SUBLANES, fc), F32)]
    scratch.append(pltpu.VMEM((tm, d_ff), BF16))
    return pl.pallas_call(
        kern,
        grid=(b, t // tm),
        in_specs=in_specs,
        out_specs=out_specs,
        out_shape=out_shape,
        scratch_shapes=scratch,
        compiler_params=_params("arbitrary", "arbitrary"),
        name="post_sample" if per_row else "post_prompt",
    )(*args)


def _pre_sample_kernel(x_ref, mod_ref, g_ref, win_ref, poolw_ref, pscale_ref, cos_ref, sin_ref, st_ref,
                       u_ref, q_ref, k_ref, v_ref, pool_ref, *, d_model, pw, aw, pos0):
    gc = pw // len(POOL_WINDOWS)
    x = x_ref[...]
    mod = mod_ref[...]
    shift, scale = mod[:, 0:d_model], mod[:, d_model:2 * d_model]
    h = _rms(x, g_ref[...]) * (1.0 + scale) + shift
    proj = jnp.dot(h.astype(BF16), win_ref[...], preferred_element_type=F32)
    u = proj[:, 0:pw]
    cos, sin_signed = cos_ref[...], sin_ref[...]
    u_ref[...] = u
    q_ref[...] = _rope(proj[:, pw:pw + aw], cos, sin_signed)
    k_ref[...] = _rope(proj[:, pw + aw:pw + 2 * aw], cos, sin_signed)
    v_ref[...] = proj[:, pw + 2 * aw:pw + 3 * aw]

    st = st_ref[...]
    pooled = []
    for g, w in enumerate(POOL_WINDOWS):
        cols = slice(g * gc, (g + 1) * gc)
        win = u[:, cols] + jnp.sum(st[:, POOL_STATE - (w - 1):, cols], axis=1)
        pooled.append(win / float(min(w, pos0 + 1)) - u[:, cols])
    pool_ref[...] = _pool_project(jnp.concatenate(pooled, axis=1), poolw_ref, pscale_ref[...]).astype(BF16)


def _pre_sample_call(x, mod, g_pre, w_in_b, pool_w_b, pool_scale, cos_t, sin_t, state_pool, *, pw, aw, pos0):
    rows, d = x.shape
    kern = functools.partial(_pre_sample_kernel, d_model=d, pw=pw, aw=aw, pos0=pos0)
    return pl.pallas_call(
        kern,
        out_shape=[jax.ShapeDtypeStruct((rows, pw), F32),
                   jax.ShapeDtypeStruct((rows, aw), F32),
                   jax.ShapeDtypeStruct((rows, aw), F32),
                   jax.ShapeDtypeStruct((rows, aw), F32),
                   jax.ShapeDtypeStruct((rows, pw), BF16)],
        compiler_params=pltpu.CompilerParams(vmem_limit_bytes=VMEM_LIMIT_BYTES),
        name="pre_sample",
    )(x, mod, g_pre, w_in_b, pool_w_b, pool_scale, cos_t, sin_t, state_pool)


def _moba_sample_kernel(pt_ref, q_ref, kn_ref, vn_ref, *refs, n_pages, page_size):
    del pt_ref
    k_refs, v_refs, o_ref = refs[:n_pages], refs[n_pages:2 * n_pages], refs[2 * n_pages]
    n_heads = k_refs[0].shape[0]
    aw = n_heads * HEAD_DIM
    ppb = MOBA_BLOCK // page_size
    n_blocks = n_pages // ppb

    head_row = lax.broadcasted_iota(jnp.int32, (n_heads, aw), 0)
    head_of_lane = lax.broadcasted_iota(jnp.int32, (n_heads, aw), 1) // HEAD_DIM
    own_lanes = head_row == head_of_lane
    qbd = jnp.where(own_lanes, q_ref[0] * (HEAD_DIM ** -0.5 * LOG2E), 0.0)
    qbd_b = qbd.astype(BF16)

    scores = []
    for p in range(n_pages):
        kt = k_refs[p][...].reshape(aw, page_size).astype(BF16)
        scores.append(jnp.dot(qbd_b, kt, preferred_element_type=F32))

    gates = [sum(jnp.sum(scores[p], axis=-1, keepdims=True) for p in range(n * ppb, (n + 1) * ppb))
             for n in range(n_blocks)]
    chosen = []
    for n in range(n_blocks):
        beaten = jnp.zeros((n_heads, 1), jnp.int32)
        for m in range(n_blocks):
            if m != n:
                wins = (gates[m] >= gates[n]) if m < n else (gates[m] > gates[n])
                beaten = beaten + wins.astype(jnp.int32)
        chosen.append(beaten < MOBA_TOPK)

    s_new = jnp.sum(qbd * kn_ref[0], axis=-1, keepdims=True)
    m = s_new
    for p in range(n_pages):
        scores[p] = jnp.where(chosen[p // ppb], scores[p], -jnp.inf)
        m = jnp.maximum(m, jnp.max(scores[p], axis=-1, keepdims=True))
    p_new = jnp.exp2(s_new - m)
    l = p_new
    acc = p_new * vn_ref[0]
    for p in range(n_pages):
        w = jnp.exp2(scores[p] - m)
        l = l + jnp.sum(w, axis=-1, keepdims=True)
        vt = v_refs[p][...].reshape(aw, page_size).astype(BF16)
        acc = acc + lax.dot_general(w.astype(BF16), vt, (((1,), (1,)), ((), ())), preferred_element_type=F32)
    o_ref[0] = jnp.sum(jnp.where(own_lanes, acc / l, 0.0), axis=0, keepdims=True)


def _moba_sample_call(page_table, q, k_new, v_new, cache_kt, cache_vt):
    n_seq, n_pages = page_table.shape
    _, n_heads, dh, page_size = cache_kt.shape
    aw = n_heads * dh
    assert dh == HEAD_DIM and MOBA_BLOCK % page_size == 0 and (n_pages * page_size) % MOBA_BLOCK == 0
    kern = functools.partial(_moba_sample_kernel, n_pages=n_pages, page_size=page_size)
    tok = pl.BlockSpec((1, 1, aw), lambda s, pt: (s, 0, 0))

    def page_spec(p):
        return pl.BlockSpec((None, n_heads, dh, page_size), lambda s, pt: (pt[s, p], 0, 0, 0))

    pages = [page_spec(p) for p in range(n_pages)]
    return pl.pallas_call(
        kern,
        grid_spec=pltpu.PrefetchScalarGridSpec(
            num_scalar_prefetch=1,
            grid=(n_seq,),
            in_specs=[tok, tok, tok] + pages + pages,
            out_specs=tok),
        out_shape=jax.ShapeDtypeStruct((n_seq, 1, aw), F32),
        compiler_params=_params("arbitrary"),
        name="moba_sample",
    )(page_table, q, k_new, v_new, *([cache_kt] * n_pages), *([cache_vt] * n_pages))


def _rope_tables(positions):
    half = HEAD_DIM // 2
    inv = ROPE_THETA ** (-jnp.arange(half, dtype=F32) / half)
    ang = positions.astype(F32)[:, None] * inv[None, :]
    cos, sin = jnp.cos(ang), jnp.sin(ang)
    reps = LANES // HEAD_DIM
    return (jnp.tile(jnp.concatenate([cos, cos], axis=1), (1, reps)),
            jnp.tile(jnp.concatenate([-sin, sin], axis=1), (1, reps)))


def kernel(x_prompt, x_sample, c_prompt, c_sample, cache_k, cache_v, page_table, state_pool, state_conv, w_ada, b_ada, g_mix_pre, g_mix_post, w_in, pool_w, pool_scale, w_out, g_ffn_pre, g_ffn_post, w_up, conv_w, conv_b, w_down):
    depth = w_ada.shape[0]
    b, t, d = x_prompt.shape
    n_seq, dec_t, _ = x_sample.shape
    n_pages = page_table.shape[1]
    page_size, n_heads = cache_k.shape[2], cache_k.shape[3]
    past_len = n_pages * page_size
    aw = n_heads * HEAD_DIM
    pw = state_pool.shape[3]
    d_ff = w_down.shape[1]
    assert dec_t == 1 and t % MOBA_BLOCK == 0 and t % PRE_TILE == 0 and pw == d - aw
    assert (pw // len(POOL_WINDOWS)) % LANES == 0 and w_in.shape[2] == pw + 3 * aw

    cos_p, sin_p = _rope_tables(jnp.arange(t, dtype=jnp.int32))
    cos_s, sin_s = _rope_tables(past_len + jnp.arange(dec_t, dtype=jnp.int32))

    y_p, y_s = x_prompt, x_sample.reshape(1, n_seq, d)
    outs = [[] for _ in range(8)]
    for l in range(depth):
        w = dict(w_out_p=w_out[l, :pw].astype(BF16), w_out_a=w_out[l, pw:].astype(BF16),
                 g_mix_post=g_mix_post[l].reshape(1, d), g_ffn_pre=g_ffn_pre[l].reshape(1, d),
                 g_ffn_post=g_ffn_post[l].reshape(1, d), w_up=w_up[l].astype(BF16), conv_w=conv_w[l],
                 conv_b=conv_b[l].reshape(1, 2 * d_ff), w_down=w_down[l].astype(BF16))
        g_pre = g_mix_pre[l].reshape(1, d)
        w_in_b = w_in[l].astype(BF16)
        pool_w_b = pool_w[l].astype(BF16)
        pscale = pool_scale[l].reshape(1, pw)

        mod = _mod_call(jnp.concatenate([c_prompt, c_sample], axis=0), w_ada[l], b_ada[l])
        mod_p, mod_s = mod[:b].reshape(b, 1, 6 * d), mod[b:].reshape(1, n_seq, 6 * d)

        k_p, v_p, qt, kb, vt4, pool_p, utail = _pre_prompt_call(
            y_p, mod_p, g_pre, w_in_b, pool_w_b, pscale, cos_p, sin_p, pw=pw, aw=aw)
        att_t = _moba_prompt_call(qt, kb.reshape(b, t // MOBA_BLOCK, MOBA_BLOCK, aw), vt4)
        y_p, ctail = _post_call(y_p, pool_p, att_t, mod_p, w, att_transposed=True)

        u_s, q_s, k_s, v_s, pool_s = _pre_sample_call(
            y_s[0], mod_s[0], g_pre, w_in_b, pool_w_b, pscale, cos_s, sin_s, state_pool[l],
            pw=pw, aw=aw, pos0=past_len)
        tok = lambda a: a.reshape(n_seq, 1, aw)
        att_s = _moba_sample_call(page_table, tok(q_s), tok(k_s), tok(v_s),
                                  jnp.transpose(cache_k[l], (0, 2, 3, 1)), jnp.transpose(cache_v[l], (0, 2, 3, 1)))
        y_s, up_s = _post_call(y_s, pool_s.reshape(1, n_seq, pw), att_s.reshape(1, n_seq, aw).astype(BF16), mod_s, w,
                               att_transposed=False,
                               state=(state_conv[l, :, 0].reshape(1, n_seq, 2 * d_ff),
                                      state_conv[l, :, 1].reshape(1, n_seq, 2 * d_ff)))

        outs[0].append(k_p.reshape(b, t, n_heads, HEAD_DIM))
        outs[1].append(v_p.reshape(b, t, n_heads, HEAD_DIM))
        outs[2].append(utail[:, POOL_HALO - POOL_STATE:])
        outs[3].append(ctail[:, SUBLANES - (CONV_WIDTH - 1):])
        outs[4].append(k_s.reshape(n_seq, dec_t, n_heads, HEAD_DIM))
        outs[5].append(v_s.reshape(n_seq, dec_t, n_heads, HEAD_DIM))
        outs[6].append(jnp.concatenate([state_pool[l, :, 1:], u_s[:, None, :]], axis=1))
        outs[7].append(jnp.concatenate([state_conv[l, :, 1:], up_s[0][:, None, :]], axis=1))
    stacked = [jnp.stack(o) for o in outs]
    return (y_p, y_s.reshape(n_seq, dec_t, d), *stacked)
```

```python
import functools
import math

import jax
import jax.numpy as jnp
from jax import lax
from jax.experimental import pallas as pl
from jax.experimental.pallas import tpu as pltpu

HEAD_DIM = 64
MOBA_BLOCK = 256
MOBA_TOPK = 3
POOL_WINDOWS = (2, 4, 8, 16)
POOL_STATE = max(POOL_WINDOWS) - 1
CONV_WIDTH = 3
ROPE_THETA = 10000.0
RMS_EPS = 1e-6

LANES = 128
SUBLANES = 8
VMEM_LIMIT_BYTES = 56 * 1024 * 1024

BF16_ROWS = 16
V_ROWS = HEAD_DIM + BF16_ROWS

POOL_HALO = 16
PRE_TILE = 512
POST_TILE = 512
FFN_CHUNKS = 11
FFN_LOOKAHEAD = 2
FFN_DOWN_SPLITS = 2
HEAD_GROUPS = 2
PAST_GROUP = 4
SCORE_LOOKAHEAD = 3

F32 = jnp.float32
BF16 = jnp.bfloat16
LOG2E = math.log2(math.e)
MASKED = -1e30


def _params(*semantics):
    return pltpu.CompilerParams(dimension_semantics=semantics, vmem_limit_bytes=VMEM_LIMIT_BYTES)


def _resident(shape, index_map):
    return pl.BlockSpec(shape, index_map, pipeline_mode=pl.Buffered(1))


def _rms(x, g):
    ms = jnp.mean(x * x, axis=-1, keepdims=True)
    return x * lax.rsqrt(ms + RMS_EPS) * g


def _rope(x, cos, sin_signed):
    lane = lax.broadcasted_iota(jnp.int32, (1, LANES), 1)
    first_half = (lane % HEAD_DIM) < HEAD_DIM // 2
    half = HEAD_DIM // 2
    outs = []
    for c in range(x.shape[1] // LANES):
        blk = x[:, c * LANES:(c + 1) * LANES]
        partner = jnp.where(first_half, pltpu.roll(blk, LANES - half, 1), pltpu.roll(blk, half, 1))
        outs.append(blk * cos + partner * sin_signed)
    return jnp.concatenate(outs, axis=1)


def _pool_project(pooled, poolw_ref, pscale):
    n_groups, gc, _ = poolw_ref.shape
    outs = [jnp.dot(pooled[:, g * gc:(g + 1) * gc].astype(BF16), poolw_ref[g], preferred_element_type=F32)
            for g in range(n_groups)]
    return jnp.concatenate(outs, axis=1) * pscale


def _mod_kernel(c_ref, w_ref, b_ref, o_ref):
    c = c_ref[...]
    a = (c * jax.nn.sigmoid(c)).astype(BF16)
    o_ref[...] = jnp.dot(a, w_ref[...].astype(BF16), preferred_element_type=F32) + b_ref[...]


def _mod_call(c, w_ada, b_ada):
    rows, d = c.shape
    n_out = w_ada.shape[1]
    return pl.pallas_call(
        _mod_kernel,
        grid=(n_out // d,),
        in_specs=[pl.BlockSpec((rows, d), lambda j: (0, 0)),
                  pl.BlockSpec((d, d), lambda j: (0, j)),
                  pl.BlockSpec((1, d), lambda j: (0, j))],
        out_specs=pl.BlockSpec((rows, d), lambda j: (0, j)),
        out_shape=jax.ShapeDtypeStruct((rows, n_out), F32),
        compiler_params=_params("arbitrary"),
        name="adaln_mod",
    )(c, w_ada, b_ada.reshape(1, n_out))


def _pre_prompt_kernel(x_ref, mod_ref, g_ref, win_ref, poolw_ref, pscale_ref, cos_ref, sin_ref,
                       k_ref, v_ref, qt_ref, kb_ref, vt_ref, pool_ref, utail_ref,
                       ubuf, s2buf, s4buf, s8buf, *, tm, d_model, pw, aw, q_scale):
    i = pl.program_id(1)
    halo = POOL_HALO
    gc = pw // len(POOL_WINDOWS)
    bufs = (ubuf, s2buf, s4buf, s8buf)

    @pl.when(i == 0)
    def _():
        for buf in bufs:
            buf[0:halo, :] = jnp.zeros((halo, buf.shape[1]), F32)

    x = x_ref[0]
    mod = mod_ref[0]
    shift, scale = mod[:, 0:d_model], mod[:, d_model:2 * d_model]
    h = _rms(x, g_ref[...]) * (1.0 + scale) + shift
    proj = jnp.dot(h.astype(BF16), win_ref[...], preferred_element_type=F32)
    u = proj[:, 0:pw]
    cos, sin_signed = cos_ref[...], sin_ref[...]
    q = _rope(proj[:, pw:pw + aw], cos, sin_signed)
    k = _rope(proj[:, pw + aw:pw + 2 * aw], cos, sin_signed)
    v = proj[:, pw + 2 * aw:pw + 3 * aw]

    k_ref[0] = k
    v_ref[0] = v
    kb_ref[0] = k.astype(BF16)
    qt_ref[0] = (q * q_scale).T.astype(BF16)
    ones_pad = jnp.where(lax.broadcasted_iota(jnp.int32, (BF16_ROWS, MOBA_BLOCK), 0) == 0, 1.0, 0.0).astype(BF16)
    for r in range(tm // MOBA_BLOCK):
        vt = v[r * MOBA_BLOCK:(r + 1) * MOBA_BLOCK, :].T.astype(BF16)
        for hd in range(aw // HEAD_DIM):
            vt_ref[0, r, hd * V_ROWS:hd * V_ROWS + HEAD_DIM, :] = vt[hd * HEAD_DIM:(hd + 1) * HEAD_DIM, :]
            vt_ref[0, r, hd * V_ROWS + HEAD_DIM:(hd + 1) * V_ROWS, :] = ones_pad

    ubuf[halo:halo + tm, :] = u
    s2 = u + ubuf[halo - 1:halo - 1 + tm, :]
    s2buf[halo:halo + tm, :] = s2[:, gc:]
    s4 = s2[:, gc:] + s2buf[halo - 2:halo - 2 + tm, :]
    s4buf[halo:halo + tm, :] = s4[:, gc:]
    s8 = s4[:, gc:] + s4buf[halo - 4:halo - 4 + tm, :]
    s8buf[halo:halo + tm, :] = s8[:, gc:]
    s16 = s8[:, gc:] + s8buf[halo - 8:halo - 8 + tm, :]
    wins = (s2[:, 0:gc], s4[:, 0:gc], s8[:, 0:gc], s16)
    pos1 = (i * tm + 1 + lax.broadcasted_iota(jnp.int32, (tm, 1), 0)).astype(F32)
    pooled = jnp.concatenate(
        [wins[g] / jnp.minimum(float(w), pos1) - u[:, g * gc:(g + 1) * gc] for g, w in enumerate(POOL_WINDOWS)],
        axis=1)
    pool_ref[0] = _pool_project(pooled, poolw_ref, pscale_ref[...]).astype(BF16)

    for buf in bufs:
        buf[0:halo, :] = buf[tm:tm + halo, :]
    utail_ref[0] = ubuf[0:halo, :]


def _pre_prompt_call(x, mod, g_pre, w_in_b, pool_w_b, pool_scale, cos_t, sin_t, *, pw, aw):
    b, t, d = x.shape
    tm = min(PRE_TILE, t)
    nb = t // MOBA_BLOCK
    gc = pw // len(POOL_WINDOWS)
    in_w = w_in_b.shape[1]
    vt_rows = (aw // HEAD_DIM) * V_ROWS
    kern = functools.partial(_pre_prompt_kernel, tm=tm, d_model=d, pw=pw, aw=aw,
                             q_scale=HEAD_DIM ** -0.5 * LOG2E)
    row_tile = lambda bi, i: (bi, i, 0)
    const2 = lambda bi, i: (0, 0)
    return pl.pallas_call(
        kern,
        grid=(b, t // tm),
        in_specs=[pl.BlockSpec((1, tm, d), row_tile),
                  pl.BlockSpec((1, 1, mod.shape[2]), lambda bi, i: (bi, 0, 0)),
                  _resident((1, d), const2),
                  _resident((d, in_w), const2),
                  _resident(pool_w_b.shape, lambda bi, i: (0, 0, 0)),
                  _resident((1, pw), const2),
                  pl.BlockSpec((tm, LANES), lambda bi, i: (i, 0)),
                  pl.BlockSpec((tm, LANES), lambda bi, i: (i, 0))],
        out_specs=[pl.BlockSpec((1, tm, aw), row_tile),
                   pl.BlockSpec((1, tm, aw), row_tile),
                   pl.BlockSpec((1, aw, tm), lambda bi, i: (bi, 0, i)),
                   pl.BlockSpec((1, tm, aw), row_tile),
                   pl.BlockSpec((1, tm // MOBA_BLOCK, vt_rows, MOBA_BLOCK), lambda bi, i: (bi, i, 0, 0)),
                   pl.BlockSpec((1, tm, pw), row_tile),
                   pl.BlockSpec((1, POOL_HALO, pw), lambda bi, i: (bi, 0, 0))],
        out_shape=[jax.ShapeDtypeStruct((b, t, aw), F32),
                   jax.ShapeDtypeStruct((b, t, aw), F32),
                   jax.ShapeDtypeStruct((b, aw, t), BF16),
                   jax.ShapeDtypeStruct((b, t, aw), BF16),
                   jax.ShapeDtypeStruct((b, nb, vt_rows, MOBA_BLOCK), BF16),
                   jax.ShapeDtypeStruct((b, t, pw), BF16),
                   jax.ShapeDtypeStruct((b, POOL_HALO, pw), F32)],
        scratch_shapes=[pltpu.VMEM((tm + POOL_HALO, pw), F32),
                        pltpu.VMEM((tm + POOL_HALO, pw - gc), F32),
                        pltpu.VMEM((tm + POOL_HALO, pw - 2 * gc), F32),
                        pltpu.VMEM((tm + POOL_HALO, pw - 3 * gc), F32)],
        compiler_params=_params("arbitrary", "arbitrary"),
        name="pre_prompt",
    )(x, mod, g_pre, w_in_b, pool_w_b, pool_scale, cos_t, sin_t)


def _moba_kernel(pt_ref, qt_ref, kb_ref, vt_ref, qs_ref, kn_ref, vn_ref, *refs, nb, n_heads, n_pages, page_size):
    del pt_ref
    k_pages, v_pages = refs[:n_pages], refs[n_pages:2 * n_pages]
    o_ref, os_ref, km_ref, ind_ref, q2_ref, m_ref, acc_ref = refs[2 * n_pages:]
    os_ref[0] = _sample_attend(qs_ref[0], kn_ref[0], vn_ref[0], k_pages, v_pages, page_size)

    i = pl.program_id(2)
    tq = MOBA_BLOCK
    pair_w = 2 * HEAD_DIM

    @pl.when(i == 0)
    def _():
        lane = lax.broadcasted_iota(jnp.int32, (MOBA_BLOCK, LANES), 1)
        for n in range(nb):
            km_ref[n:n + 1, :] = jnp.mean(kb_ref[0, n].astype(F32), axis=0, keepdims=True)
            ind_ref[n] = jnp.where(lane == n, 1.0, 0.0).astype(BF16)
        for h in range(n_heads):
            q2_ref[h, pair_w + nb:, :] = jnp.zeros((pair_w - nb, tq), BF16)

    blk_row = lax.broadcasted_iota(jnp.int32, (nb, tq), 0)
    blk_row_f = blk_row.astype(F32)
    fully_past = blk_row < i
    key_row = lax.broadcasted_iota(jnp.int32, (MOBA_BLOCK, tq), 0)
    qry_col = lax.broadcasted_iota(jnp.int32, (MOBA_BLOCK, tq), 1)
    causal = key_row <= qry_col
    pair_row = lax.broadcasted_iota(jnp.int32, (pair_w, tq), 0)

    for h in range(n_heads):
        lo = (h // 2) * pair_w
        mine = (pair_row >= HEAD_DIM) if h % 2 else (pair_row < HEAD_DIM)
        q2 = jnp.where(mine, qt_ref[0, lo:lo + pair_w, :], jnp.zeros((), BF16))
        q2_ref[h, 0:pair_w, :] = q2

        gate = jnp.dot(km_ref[:, lo:lo + pair_w].astype(BF16), q2, preferred_element_type=F32)
        gate = jnp.where(fully_past, gate, -jnp.inf)
        chosen = jnp.zeros((nb, tq), jnp.bool_)
        for _ in range(MOBA_TOPK):
            best = jnp.max(gate, axis=0, keepdims=True)
            first = jnp.min(jnp.where(gate == best, blk_row_f, float(nb)), axis=0, keepdims=True)
            pick = blk_row_f == first
            chosen = jnp.logical_or(chosen, pick)
            gate = jnp.where(pick, -jnp.inf, gate)
        q2_ref[h, pair_w:pair_w + nb, :] = jnp.where(jnp.logical_and(chosen, fully_past), 0.0, MASKED).astype(BF16)

    def scores(js, h):
        lo = (h // 2) * pair_w
        return [jnp.dot(jnp.concatenate([kb_ref[0, j, :, lo:lo + pair_w], ind_ref[j]], axis=1), q2_ref[h],
                        preferred_element_type=F32) for j in js]

    def own_scores(js, h):
        lo = (h // 2) * pair_w
        return [jnp.dot(kb_ref[0, j, :, lo:lo + pair_w], q2_ref[h, 0:pair_w, :], preferred_element_type=F32)
                for j in js]

    look = min(SCORE_LOOKAHEAD, n_heads)

    def all_heads(js, consume, scores=scores, first=None, js_next=None):
        ahead = dict(enumerate(first)) if first is not None else {h: scores(js, h) for h in range(look)}
        nxt = []
        for h in range(n_heads):
            if h + look < n_heads:
                ahead[h + look] = scores(js, h + look)
            elif js_next is not None:
                nxt.append(scores(js_next, h + look - n_heads))
            consume(h, ahead.pop(h))
        return nxt

    def values(j, h):
        return vt_ref[0, j, h * V_ROWS:(h + 1) * V_ROWS, :]

    def first_block(h, ss):
        s = jnp.where(causal, ss[0], -jnp.inf)
        m0 = jnp.max(s, axis=0, keepdims=True)
        m_ref[h] = m0
        acc_ref[h] = jnp.dot(values(i, h), jnp.exp2(s - m0).astype(BF16), preferred_element_type=F32)

    all_heads([i], first_block, own_scores)

    def online_update(js, h, ss, m_old, acc_old):
        m_new = m_old
        for s in ss:
            m_new = jnp.maximum(m_new, jnp.max(s, axis=0, keepdims=True))
        pv = None
        for j, s in zip(js, ss):
            d = jnp.dot(values(j, h), jnp.exp2(s - m_new).astype(BF16), preferred_element_type=F32)
            pv = d if pv is None else pv + d
        return m_new, jnp.exp2(m_old - m_new) * acc_old + pv

    size, top = 1, i
    while size < PAST_GROUP:
        @pl.when((i & size) != 0)
        def _(size=size, top=top):
            js = [top - size + r for r in range(size)]

            def piece(h, ss):
                m_ref[h], acc_ref[h] = online_update(js, h, ss, m_ref[h], acc_ref[h])

            all_heads(js, piece)

        top = top - (i & size)
        size *= 2

    def group_body(t, carry):
        js = [PAST_GROUP * t + r for r in range(PAST_GROUP)]

        def group(h, ss):
            m_ref[h], acc_ref[h] = online_update(js, h, ss, m_ref[h], acc_ref[h])

        all_heads(js, group)
        return carry

    lax.fori_loop(0, i // PAST_GROUP, group_body, 0)

    for h in range(n_heads):
        acc = acc_ref[h]
        o_ref[0, h * HEAD_DIM:(h + 1) * HEAD_DIM, :] = (acc[0:HEAD_DIM] / acc[HEAD_DIM:HEAD_DIM + 1]).astype(BF16)


def _moba_call(qt, kb4, vt4, page_table, q_s, k_s, v_s, cache_kt, cache_vt):
    b, aw, t = qt.shape
    nb = t // MOBA_BLOCK
    n_heads = aw // HEAD_DIM
    n_seq, n_pages = page_table.shape
    _, _, dh, page_size = cache_kt.shape
    hg = n_heads // HEAD_GROUPS
    gw = hg * HEAD_DIM
    assert nb <= LANES and hg % 2 == 0
    assert dh == HEAD_DIM and MOBA_BLOCK % page_size == 0 and (n_pages * page_size) % MOBA_BLOCK == 0
    assert n_seq <= b * HEAD_GROUPS * nb
    kern = functools.partial(_moba_kernel, nb=nb, n_heads=hg, n_pages=n_pages, page_size=page_size)

    def seq(bi, g, i):
        return jnp.minimum((bi * HEAD_GROUPS + g) * nb + i, n_seq - 1)

    tok = pl.BlockSpec((1, 1, aw), lambda bi, g, i, pt: (seq(bi, g, i), 0, 0))

    def page_spec(p):
        return pl.BlockSpec((None, n_heads, dh, page_size), lambda bi, g, i, pt: (pt[seq(bi, g, i), p], 0, 0, 0))

    pages = [page_spec(p) for p in range(n_pages)]
    q_tile = pl.BlockSpec((1, gw, MOBA_BLOCK), lambda bi, g, i, pt: (bi, g, i))
    return pl.pallas_call(
        kern,
        grid_spec=pltpu.PrefetchScalarGridSpec(
            num_scalar_prefetch=1,
            grid=(b, HEAD_GROUPS, nb),
            in_specs=[q_tile,
                      pl.BlockSpec((1, nb, MOBA_BLOCK, gw), lambda bi, g, i, pt: (bi, 0, 0, g),
                                   pipeline_mode=pl.Buffered(1)),
                      pl.BlockSpec((1, nb, hg * V_ROWS, MOBA_BLOCK), lambda bi, g, i, pt: (bi, 0, g, 0),
                                   pipeline_mode=pl.Buffered(1)),
                      tok, tok, tok] + pages + pages,
            out_specs=[q_tile, tok],
            scratch_shapes=[pltpu.VMEM((nb, gw), F32),
                            pltpu.VMEM((nb, MOBA_BLOCK, LANES), BF16),
                            pltpu.VMEM((hg, 4 * HEAD_DIM, MOBA_BLOCK), BF16),
                            pltpu.VMEM((hg, 1, MOBA_BLOCK), F32),
                            pltpu.VMEM((hg, V_ROWS, MOBA_BLOCK), F32)]),
        out_shape=[jax.ShapeDtypeStruct((b, aw, t), BF16), jax.ShapeDtypeStruct((n_seq, 1, aw), F32)],
        compiler_params=_params("arbitrary", "arbitrary", "arbitrary"),
        name="moba",
    )(page_table, qt, kb4, vt4, q_s, k_s, v_s, *([cache_kt] * n_pages), *([cache_vt] * n_pages))


def _post_kernel(*refs, tm, d_model, d_ff, att_transposed, per_row_state):
    (x_ref, pool_ref, att_ref, mod_ref, wop_ref, woa_ref, gpost_ref, gpre2_ref, gpost2_ref,
     wup_ref, cw_ref, cb_ref, wdn_ref) = refs[:13]
    if per_row_state:
        st0_ref, st1_ref, y_ref, up_ref, actbuf = refs[13:]
    else:
        y_ref, ctail_ref, carry, upbuf, actbuf = refs[13:]
        i = pl.program_id(1)

        @pl.when(i == 0)
        def _():
            carry[...] = jnp.zeros(carry.shape, F32)

    d = d_model
    x = x_ref[0]
    mod = mod_ref[0]
    gate1 = mod[:, 2 * d:3 * d]
    shift2, scale2, gate2 = mod[:, 3 * d:4 * d], mod[:, 4 * d:5 * d], mod[:, 5 * d:6 * d]

    mix = jnp.dot(pool_ref[0], wop_ref[...], preferred_element_type=F32)
    if att_transposed:
        mix += lax.dot_general(att_ref[0], woa_ref[...], (((0,), (0,)), ((), ())), preferred_element_type=F32)
    else:
        mix += jnp.dot(att_ref[0], woa_ref[...], preferred_element_type=F32)
    x1 = x + gate1 * _rms(mix, gpost_ref[...])
    h2 = (_rms(x1, gpre2_ref[...]) * (1.0 + scale2) + shift2).astype(BF16)

    fc = d_ff // FFN_CHUNKS

    def up_proj(c):
        out = []
        for base in (0, d_ff):
            cols = slice(base + c * fc, base + (c + 1) * fc)
            out.append((cols, jnp.dot(h2, wup_ref[:, cols], preferred_element_type=F32)))
        return out

    def conv_act(c, ups):
        halves = []
        for half, (cols, up) in enumerate(ups):
            if per_row_state:
                prev2, prev1 = st0_ref[0, :, cols], st1_ref[0, :, cols]
                up_ref[0, :, cols] = up
            else:
                buf = upbuf.at[(2 * c + half) % upbuf.shape[0]]
                buf[0:SUBLANES, :] = carry[:, cols]
                buf[SUBLANES:SUBLANES + tm, :] = up
                prev1 = buf[SUBLANES - 1:SUBLANES - 1 + tm, :]
                prev2 = buf[SUBLANES - 2:SUBLANES - 2 + tm, :]
                carry[:, cols] = buf[tm:tm + SUBLANES, :]
            halves.append(cb_ref[:, cols] + cw_ref[0:1, cols] * prev2 + cw_ref[1:2, cols] * prev1
                          + cw_ref[2:3, cols] * up)
        a, g = halves
        actbuf[:, c * fc:(c + 1) * fc] = (a * jax.nn.sigmoid(a) * g).astype(BF16)

    split_at = [((s + 1) * FFN_CHUNKS) // FFN_DOWN_SPLITS for s in range(FFN_DOWN_SPLITS)]
    ahead = {c: up_proj(c) for c in range(min(FFN_LOOKAHEAD, FFN_CHUNKS))}
    ffn, done = None, 0
    for c in range(FFN_CHUNKS):
        if c + FFN_LOOKAHEAD < FFN_CHUNKS:
            ahead[c + FFN_LOOKAHEAD] = up_proj(c + FFN_LOOKAHEAD)
        conv_act(c, ahead.pop(c))
        if c + 1 in split_at:
            rows = slice(done * fc, (c + 1) * fc)
            part = jnp.dot(actbuf[:, rows], wdn_ref[rows, :], preferred_element_type=F32)
            ffn = part if ffn is None else ffn + part
            done = c + 1
    y_ref[0] = x1 + gate2 * _rms(ffn, gpost2_ref[...])
    if not per_row_state:
        ctail_ref[0] = carry[...]


def _post_call(x, pool_o, att, mod, w, *, att_transposed, state=None):
    b, t, d = x.shape
    pw = pool_o.shape[2]
    aw = w["w_out_a"].shape[0]
    d_ff = w["w_down"].shape[0]
    per_row = state is not None
    tm = t if per_row else min(POST_TILE, t)
    assert t % tm == 0 and d_ff % (FFN_CHUNKS * LANES) == 0 and CONV_WIDTH == 3
    fc = d_ff // FFN_CHUNKS
    kern = functools.partial(_post_kernel, tm=tm, d_model=d, d_ff=d_ff, att_transposed=att_transposed,
                             per_row_state=per_row)
    row_tile = lambda bi, i: (bi, i, 0)
    const2 = lambda bi, i: (0, 0)
    mod_rows = mod.shape[1]
    in_specs = [pl.BlockSpec((1, tm, d), row_tile),
                pl.BlockSpec((1, tm, pw), row_tile),
                (pl.BlockSpec((1, aw, tm), lambda bi, i: (bi, 0, i)) if att_transposed
                 else pl.BlockSpec((1, tm, aw), row_tile)),
                pl.BlockSpec((1, mod_rows, mod.shape[2]), (row_tile if mod_rows > 1 else lambda bi, i: (bi, 0, 0))),
                _resident((pw, d), const2),
                _resident((aw, d), const2),
                _resident((1, d), const2),
                _resident((1, d), const2),
                _resident((1, d), const2),
                _resident((d, 2 * d_ff), const2),
                _resident((CONV_WIDTH, 2 * d_ff), const2),
                _resident((1, 2 * d_ff), const2),
                _resident((d_ff, d), const2)]
    args = [x, pool_o, att, mod, w["w_out_p"], w["w_out_a"], w["g_mix_post"], w["g_ffn_pre"], w["g_ffn_post"],
            w["w_up"], w["conv_w"], w["conv_b"], w["w_down"]]
    if per_row:
        in_specs += [pl.BlockSpec((1, tm, 2 * d_ff), row_tile)] * 2
        args += list(state)
        out_specs = [pl.BlockSpec((1, tm, d), row_tile), pl.BlockSpec((1, tm, 2 * d_ff), row_tile)]
        out_shape = [jax.ShapeDtypeStruct((b, t, d), F32), jax.ShapeDtypeStruct((b, t, 2 * d_ff), F32)]
        scratch = []
    else:
        out_specs = [pl.BlockSpec((1, tm, d), row_tile),
                     pl.BlockSpec((1, SUBLANES, 2 * d_ff), lambda bi, i: (bi, 0, 0))]
        out_shape = [jax.ShapeDtypeStruct((b, t, d), F32), jax.ShapeDtypeStruct((b, SUBLANES, 2 * d_ff), F32)]
        scratch = [pltpu.VMEM((SUBLANES, 2 * d_ff), F32),
                   pltpu.VMEM((4, tm + SUBLANES, fc), F32)]
    scratch.append(pltpu.VMEM((tm, d_ff), BF16))
    return pl.pallas_call(
        kern,
        grid=(b, t // tm),
        in_specs=in_specs,
        out_specs=out_specs,
        out_shape=out_shape,
        scratch_shapes=scratch,
        compiler_params=_params("arbitrary", "arbitrary"),
        name="post_sample" if per_row else "post_prompt",
    )(*args)


def _pre_sample_kernel(x_ref, mod_ref, g_ref, win_ref, poolw_ref, pscale_ref, cos_ref, sin_ref, st_ref,
                       u_ref, q_ref, k_ref, v_ref, pool_ref, *, d_model, pw, aw, pos0):
    gc = pw // len(POOL_WINDOWS)
    x = x_ref[...]
    mod = mod_ref[...]
    shift, scale = mod[:, 0:d_model], mod[:, d_model:2 * d_model]
    h = _rms(x, g_ref[...]) * (1.0 + scale) + shift
    proj = jnp.dot(h.astype(BF16), win_ref[...], preferred_element_type=F32)
    u = proj[:, 0:pw]
    cos, sin_signed = cos_ref[...], sin_ref[...]
    u_ref[...] = u
    q_ref[...] = _rope(proj[:, pw:pw + aw], cos, sin_signed)
    k_ref[...] = _rope(proj[:, pw + aw:pw + 2 * aw], cos, sin_signed)
    v_ref[...] = proj[:, pw + 2 * aw:pw + 3 * aw]

    st = st_ref[...]
    pooled = []
    for g, w in enumerate(POOL_WINDOWS):
        cols = slice(g * gc, (g + 1) * gc)
        win = u[:, cols] + jnp.sum(st[:, POOL_STATE - (w - 1):, cols], axis=1)
        pooled.append(win / float(min(w, pos0 + 1)) - u[:, cols])
    pool_ref[...] = _pool_project(jnp.concatenate(pooled, axis=1), poolw_ref, pscale_ref[...]).astype(BF16)


def _pre_sample_call(x, mod, g_pre, w_in_b, pool_w_b, pool_scale, cos_t, sin_t, state_pool, *, pw, aw, pos0):
    rows, d = x.shape
    kern = functools.partial(_pre_sample_kernel, d_model=d, pw=pw, aw=aw, pos0=pos0)
    return pl.pallas_call(
        kern,
        out_shape=[jax.ShapeDtypeStruct((rows, pw), F32),
                   jax.ShapeDtypeStruct((rows, aw), F32),
                   jax.ShapeDtypeStruct((rows, aw), F32),
                   jax.ShapeDtypeStruct((rows, aw), F32),
                   jax.ShapeDtypeStruct((rows, pw), BF16)],
        compiler_params=pltpu.CompilerParams(vmem_limit_bytes=VMEM_LIMIT_BYTES),
        name="pre_sample",
    )(x, mod, g_pre, w_in_b, pool_w_b, pool_scale, cos_t, sin_t, state_pool)


def _sample_attend(q, k_new, v_new, k_refs, v_refs, page_size):
    n_pages = len(k_refs)
    n_heads = k_refs[0].shape[0]
    aw = n_heads * HEAD_DIM
    ppb = MOBA_BLOCK // page_size
    n_blocks = n_pages // ppb

    head_row = lax.broadcasted_iota(jnp.int32, (n_heads, aw), 0)
    head_of_lane = lax.broadcasted_iota(jnp.int32, (n_heads, aw), 1) // HEAD_DIM
    own_lanes = head_row == head_of_lane
    qbd = jnp.where(own_lanes, q * (HEAD_DIM ** -0.5 * LOG2E), 0.0)
    qbd_b = qbd.astype(BF16)

    scores = []
    for p in range(n_pages):
        kt = k_refs[p][...].reshape(aw, page_size).astype(BF16)
        scores.append(jnp.dot(qbd_b, kt, preferred_element_type=F32))

    gates = [sum(jnp.sum(scores[p], axis=-1, keepdims=True) for p in range(n * ppb, (n + 1) * ppb))
             for n in range(n_blocks)]
    chosen = []
    for n in range(n_blocks):
        beaten = jnp.zeros((n_heads, 1), jnp.int32)
        for m in range(n_blocks):
            if m != n:
                wins = (gates[m] >= gates[n]) if m < n else (gates[m] > gates[n])
                beaten = beaten + wins.astype(jnp.int32)
        chosen.append(beaten < MOBA_TOPK)

    s_new = jnp.sum(qbd * k_new, axis=-1, keepdims=True)
    m = s_new
    for p in range(n_pages):
        scores[p] = jnp.where(chosen[p // ppb], scores[p], -jnp.inf)
        m = jnp.maximum(m, jnp.max(scores[p], axis=-1, keepdims=True))
    p_new = jnp.exp2(s_new - m)
    weights = [jnp.exp2(s - m) for s in scores]
    inv_l = 1.0 / (p_new + sum(jnp.sum(w, axis=-1, keepdims=True) for w in weights))
    weights = [w * inv_l for w in weights]

    partial = []
    for h in range(n_heads):
        acc = None
        for p in range(n_pages):
            term = v_refs[p][h] * weights[p][h:h + 1, :]
            acc = term if acc is None else acc + term
        partial.append(acc)
    partial = jnp.concatenate(partial, axis=0)
    ones = jnp.ones((SUBLANES, page_size), BF16)
    high = partial.astype(BF16)
    low = (partial - high.astype(F32)).astype(BF16)
    contract_lanes = (((1,), (1,)), ((), ()))
    past = (lax.dot_general(ones, high, contract_lanes, preferred_element_type=F32)
            + lax.dot_general(ones, low, contract_lanes, preferred_element_type=F32))[0:1, :]
    own = jnp.sum(jnp.where(own_lanes, p_new * inv_l, 0.0), axis=0, keepdims=True)
    return past + own * v_new


def _rope_tables(positions):
    half = HEAD_DIM // 2
    inv = ROPE_THETA ** (-jnp.arange(half, dtype=F32) / half)
    ang = positions.astype(F32)[:, None] * inv[None, :]
    cos, sin = jnp.cos(ang), jnp.sin(ang)
    reps = LANES // HEAD_DIM
    return (jnp.tile(jnp.concatenate([cos, cos], axis=1), (1, reps)),
            jnp.tile(jnp.concatenate([-sin, sin], axis=1), (1, reps)))


def kernel(x_prompt, x_sample, c_prompt, c_sample, cache_k, cache_v, page_table, state_pool, state_conv, w_ada, b_ada, g_mix_pre, g_mix_post, w_in, pool_w, pool_scale, w_out, g_ffn_pre, g_ffn_post, w_up, conv_w, conv_b, w_down):
    depth = w_ada.shape[0]
    b, t, d = x_prompt.shape
    n_seq, dec_t, _ = x_sample.shape
    n_pages = page_table.shape[1]
    page_size, n_heads = cache_k.shape[2], cache_k.shape[3]
    past_len = n_pages * page_size
    aw = n_heads * HEAD_DIM
    pw = state_pool.shape[3]
    d_ff = w_down.shape[1]
    assert dec_t == 1 and t % MOBA_BLOCK == 0 and t % PRE_TILE == 0 and pw == d - aw
    assert (pw // len(POOL_WINDOWS)) % LANES == 0 and w_in.shape[2] == pw + 3 * aw

    cos_p, sin_p = _rope_tables(jnp.arange(t, dtype=jnp.int32))
    cos_s, sin_s = _rope_tables(past_len + jnp.arange(dec_t, dtype=jnp.int32))

    y_p, y_s = x_prompt, x_sample.reshape(1, n_seq, d)
    outs = [[] for _ in range(8)]
    for l in range(depth):
        w = dict(w_out_p=w_out[l, :pw].astype(BF16), w_out_a=w_out[l, pw:].astype(BF16),
                 g_mix_post=g_mix_post[l].reshape(1, d), g_ffn_pre=g_ffn_pre[l].reshape(1, d),
                 g_ffn_post=g_ffn_post[l].reshape(1, d), w_up=w_up[l].astype(BF16), conv_w=conv_w[l],
                 conv_b=conv_b[l].reshape(1, 2 * d_ff), w_down=w_down[l].astype(BF16))
        g_pre = g_mix_pre[l].reshape(1, d)
        w_in_b = w_in[l].astype(BF16)
        pool_w_b = pool_w[l].astype(BF16)
        pscale = pool_scale[l].reshape(1, pw)

        mod = _mod_call(jnp.concatenate([c_prompt, c_sample], axis=0), w_ada[l], b_ada[l])
        mod_p, mod_s = mod[:b].reshape(b, 1, 6 * d), mod[b:].reshape(1, n_seq, 6 * d)

        k_p, v_p, qt, kb, vt4, pool_p, utail = _pre_prompt_call(
            y_p, mod_p, g_pre, w_in_b, pool_w_b, pscale, cos_p, sin_p, pw=pw, aw=aw)
        u_s, q_s, k_s, v_s, pool_s = _pre_sample_call(
            y_s[0], mod_s[0], g_pre, w_in_b, pool_w_b, pscale, cos_s, sin_s, state_pool[l],
            pw=pw, aw=aw, pos0=past_len)
        tok = lambda a: a.reshape(n_seq, 1, aw)
        att_t, att_s = _moba_call(qt, kb.reshape(b, t // MOBA_BLOCK, MOBA_BLOCK, aw), vt4,
                                  page_table, tok(q_s), tok(k_s), tok(v_s),
                                  jnp.transpose(cache_k[l], (0, 2, 3, 1)), jnp.transpose(cache_v[l], (0, 2, 3, 1)))
        y_p, ctail = _post_call(y_p, pool_p, att_t, mod_p, w, att_transposed=True)
        y_s, up_s = _post_call(y_s, pool_s.reshape(1, n_seq, pw), att_s.reshape(1, n_seq, aw).astype(BF16), mod_s, w,
                               att_transposed=False,
                               state=(state_conv[l, :, 0].reshape(1, n_seq, 2 * d_ff),
                                      state_conv[l, :, 1].reshape(1, n_seq, 2 * d_ff)))

        outs[0].append(k_p.reshape(b, t, n_heads, HEAD_DIM))
        outs[1].append(v_p.reshape(b, t, n_heads, HEAD_DIM))
        outs[2].append(utail[:, POOL_HALO - POOL_STATE:])
        outs[3].append(ctail[:, SUBLANES - (CONV_WIDTH - 1):])
        outs[4].append(k_s.reshape(n_seq, dec_t, n_heads, HEAD_DIM))
        outs[5].append(v_s.reshape(n_seq, dec_t, n_heads, HEAD_DIM))
        outs[6].append(jnp.concatenate([state_pool[l, :, 1:], u_s[:, None, :]], axis=1))
        outs[7].append(jnp.concatenate([state_conv[l, :, 1:], up_s[0][:, None, :]], axis=1))
    stacked = [jnp.stack(o) for o in outs]
    return (y_p, y_s.reshape(n_seq, dec_t, d), *stacked)
```

```python
import functools
import math

import jax
import jax.numpy as jnp
from jax import lax
from jax.experimental import pallas as pl
from jax.experimental.pallas import tpu as pltpu

HEAD_DIM = 64
MOBA_BLOCK = 256
MOBA_TOPK = 3
POOL_WINDOWS = (2, 4, 8, 16)
POOL_STATE = max(POOL_WINDOWS) - 1
CONV_WIDTH = 3
ROPE_THETA = 10000.0
RMS_EPS = 1e-6

LANES = 128
SUBLANES = 8
VMEM_LIMIT_BYTES = 56 * 1024 * 1024

BF16_ROWS = 16
V_ROWS = HEAD_DIM + BF16_ROWS

POOL_HALO = 16
PRE_TILE = 512
POST_TILE = 512
FFN_CHUNKS = 11
FFN_LOOKAHEAD = 2
FFN_DOWN_SPLITS = 2
HEAD_GROUPS = 2
PAST_GROUP = 4
SCORE_LOOKAHEAD = 3

F32 = jnp.float32
BF16 = jnp.bfloat16
LOG2E = math.log2(math.e)
MASKED = -1e30


def _params(*semantics):
    return pltpu.CompilerParams(dimension_semantics=semantics, vmem_limit_bytes=VMEM_LIMIT_BYTES)


def _resident(shape, index_map):
    return pl.BlockSpec(shape, index_map, pipeline_mode=pl.Buffered(1))


def _rms(x, g):
    ms = jnp.mean(x * x, axis=-1, keepdims=True)
    return x * lax.rsqrt(ms + RMS_EPS) * g


def _rope(x, cos, sin_signed):
    lane = lax.broadcasted_iota(jnp.int32, (1, LANES), 1)
    first_half = (lane % HEAD_DIM) < HEAD_DIM // 2
    half = HEAD_DIM // 2
    outs = []
    for c in range(x.shape[1] // LANES):
        blk = x[:, c * LANES:(c + 1) * LANES]
        partner = jnp.where(first_half, pltpu.roll(blk, LANES - half, 1), pltpu.roll(blk, half, 1))
        outs.append(blk * cos + partner * sin_signed)
    return jnp.concatenate(outs, axis=1)


def _pool_project(pooled, poolw_ref, pscale):
    n_groups, gc, _ = poolw_ref.shape
    outs = [jnp.dot(pooled[:, g * gc:(g + 1) * gc].astype(BF16), poolw_ref[g], preferred_element_type=F32)
            for g in range(n_groups)]
    return jnp.concatenate(outs, axis=1) * pscale


def _mod_kernel(cp_ref, cs_ref, w_ref, b_ref, op_ref, os_ref):
    w = w_ref[...].astype(BF16)
    for c_ref, o_ref in ((cp_ref, op_ref), (cs_ref, os_ref)):
        c = c_ref[...]
        a = (c * jax.nn.sigmoid(c)).astype(BF16)
        o_ref[...] = jnp.dot(a, w, preferred_element_type=F32) + b_ref[...]


def _mod_call(c_prompt, c_sample, w_ada, b_ada):
    d, n_out = w_ada.shape
    rows_p, rows_s = c_prompt.shape[0], c_sample.shape[0]
    full = lambda j: (0, 0)
    col = lambda j: (0, j)
    return pl.pallas_call(
        _mod_kernel,
        grid=(n_out // d,),
        in_specs=[pl.BlockSpec((rows_p, d), full), pl.BlockSpec((rows_s, d), full),
                  pl.BlockSpec((d, d), col), pl.BlockSpec((1, d), col)],
        out_specs=[pl.BlockSpec((rows_p, d), col), pl.BlockSpec((rows_s, d), col)],
        out_shape=[jax.ShapeDtypeStruct((rows_p, n_out), F32), jax.ShapeDtypeStruct((rows_s, n_out), F32)],
        compiler_params=_params("arbitrary"),
        name="adaln_mod",
    )(c_prompt, c_sample, w_ada, b_ada.reshape(1, n_out))


def _pre_prompt_kernel(x_ref, mod_ref, g_ref, win_ref, poolw_ref, pscale_ref, cos_ref, sin_ref,
                       k_ref, v_ref, qt_ref, kb_ref, vt_ref, pool_ref, utail_ref,
                       ubuf, s2buf, s4buf, s8buf, *, tm, d_model, pw, aw, q_scale):
    i = pl.program_id(1)
    halo = POOL_HALO
    gc = pw // len(POOL_WINDOWS)
    bufs = (ubuf, s2buf, s4buf, s8buf)

    @pl.when(i == 0)
    def _():
        for buf in bufs:
            buf[0:halo, :] = jnp.zeros((halo, buf.shape[1]), F32)

    x = x_ref[0]
    mod = mod_ref[0]
    shift, scale = mod[:, 0:d_model], mod[:, d_model:2 * d_model]
    h = _rms(x, g_ref[...]) * (1.0 + scale) + shift
    proj = jnp.dot(h.astype(BF16), win_ref[...], preferred_element_type=F32)
    u = proj[:, 0:pw]
    cos, sin_signed = cos_ref[...], sin_ref[...]
    q = _rope(proj[:, pw:pw + aw], cos, sin_signed)
    k = _rope(proj[:, pw + aw:pw + 2 * aw], cos, sin_signed)
    v = proj[:, pw + 2 * aw:pw + 3 * aw]

    k_ref[0] = k
    v_ref[0] = v
    kb_ref[0] = k.astype(BF16)
    qt_ref[0] = (q * q_scale).T.astype(BF16)
    ones_pad = jnp.where(lax.broadcasted_iota(jnp.int32, (BF16_ROWS, MOBA_BLOCK), 0) == 0, 1.0, 0.0).astype(BF16)
    for r in range(tm // MOBA_BLOCK):
        vt = v[r * MOBA_BLOCK:(r + 1) * MOBA_BLOCK, :].T.astype(BF16)
        for hd in range(aw // HEAD_DIM):
            vt_ref[0, r, hd * V_ROWS:hd * V_ROWS + HEAD_DIM, :] = vt[hd * HEAD_DIM:(hd + 1) * HEAD_DIM, :]
            vt_ref[0, r, hd * V_ROWS + HEAD_DIM:(hd + 1) * V_ROWS, :] = ones_pad

    ubuf[halo:halo + tm, :] = u
    s2 = u + ubuf[halo - 1:halo - 1 + tm, :]
    s2buf[halo:halo + tm, :] = s2[:, gc:]
    s4 = s2[:, gc:] + s2buf[halo - 2:halo - 2 + tm, :]
    s4buf[halo:halo + tm, :] = s4[:, gc:]
    s8 = s4[:, gc:] + s4buf[halo - 4:halo - 4 + tm, :]
    s8buf[halo:halo + tm, :] = s8[:, gc:]
    s16 = s8[:, gc:] + s8buf[halo - 8:halo - 8 + tm, :]
    wins = (s2[:, 0:gc], s4[:, 0:gc], s8[:, 0:gc], s16)
    pos1 = (i * tm + 1 + lax.broadcasted_iota(jnp.int32, (tm, 1), 0)).astype(F32)
    pooled = jnp.concatenate(
        [wins[g] / jnp.minimum(float(w), pos1) - u[:, g * gc:(g + 1) * gc] for g, w in enumerate(POOL_WINDOWS)],
        axis=1)
    pool_ref[0] = _pool_project(pooled, poolw_ref, pscale_ref[...]).astype(BF16)

    for buf in bufs:
        buf[0:halo, :] = buf[tm:tm + halo, :]
    utail_ref[0] = ubuf[0:halo, :]


def _pre_prompt_call(x, mod, g_pre, w_in_b, pool_w_b, pool_scale, cos_t, sin_t, *, pw, aw):
    b, t, d = x.shape
    tm = min(PRE_TILE, t)
    nb = t // MOBA_BLOCK
    gc = pw // len(POOL_WINDOWS)
    in_w = w_in_b.shape[1]
    vt_rows = (aw // HEAD_DIM) * V_ROWS
    kern = functools.partial(_pre_prompt_kernel, tm=tm, d_model=d, pw=pw, aw=aw,
                             q_scale=HEAD_DIM ** -0.5 * LOG2E)
    row_tile = lambda bi, i: (bi, i, 0)
    const2 = lambda bi, i: (0, 0)
    return pl.pallas_call(
        kern,
        grid=(b, t // tm),
        in_specs=[pl.BlockSpec((1, tm, d), row_tile),
                  pl.BlockSpec((1, 1, mod.shape[2]), lambda bi, i: (bi, 0, 0)),
                  _resident((1, d), const2),
                  _resident((d, in_w), const2),
                  _resident(pool_w_b.shape, lambda bi, i: (0, 0, 0)),
                  _resident((1, pw), const2),
                  pl.BlockSpec((tm, LANES), lambda bi, i: (i, 0)),
                  pl.BlockSpec((tm, LANES), lambda bi, i: (i, 0))],
        out_specs=[pl.BlockSpec((1, tm, aw), row_tile),
                   pl.BlockSpec((1, tm, aw), row_tile),
                   pl.BlockSpec((1, aw, tm), lambda bi, i: (bi, 0, i)),
                   pl.BlockSpec((1, tm, aw), row_tile),
                   pl.BlockSpec((1, tm // MOBA_BLOCK, vt_rows, MOBA_BLOCK), lambda bi, i: (bi, i, 0, 0)),
                   pl.BlockSpec((1, tm, pw), row_tile),
                   pl.BlockSpec((1, POOL_HALO, pw), lambda bi, i: (bi, 0, 0))],
        out_shape=[jax.ShapeDtypeStruct((b, t, aw), F32),
                   jax.ShapeDtypeStruct((b, t, aw), F32),
                   jax.ShapeDtypeStruct((b, aw, t), BF16),
                   jax.ShapeDtypeStruct((b, t, aw), BF16),
                   jax.ShapeDtypeStruct((b, nb, vt_rows, MOBA_BLOCK), BF16),
                   jax.ShapeDtypeStruct((b, t, pw), BF16),
                   jax.ShapeDtypeStruct((b, POOL_HALO, pw), F32)],
        scratch_shapes=[pltpu.VMEM((tm + POOL_HALO, pw), F32),
                        pltpu.VMEM((tm + POOL_HALO, pw - gc), F32),
                        pltpu.VMEM((tm + POOL_HALO, pw - 2 * gc), F32),
                        pltpu.VMEM((tm + POOL_HALO, pw - 3 * gc), F32)],
        compiler_params=_params("arbitrary", "arbitrary"),
        name="pre_prompt",
    )(x, mod, g_pre, w_in_b, pool_w_b, pool_scale, cos_t, sin_t)


def _moba_kernel(pt_ref, qt_ref, kb_ref, vt_ref, qs_ref, kn_ref, vn_ref, *refs, nb, n_heads, n_pages, page_size):
    del pt_ref
    k_pages, v_pages = refs[:n_pages], refs[n_pages:2 * n_pages]
    o_ref, os_ref, km_ref, ind_ref, q2_ref, m_ref, acc_ref = refs[2 * n_pages:]

    i = pl.program_id(2)
    tq = MOBA_BLOCK
    pair_w = 2 * HEAD_DIM

    @pl.when(i == 0)
    def _():
        lane = lax.broadcasted_iota(jnp.int32, (MOBA_BLOCK, LANES), 1)
        for n in range(nb):
            km_ref[n:n + 1, :] = jnp.mean(kb_ref[0, n].astype(F32), axis=0, keepdims=True)
            ind_ref[n] = jnp.where(lane == n, 1.0, 0.0).astype(BF16)
        for h in range(n_heads):
            q2_ref[h, pair_w + nb:, :] = jnp.zeros((pair_w - nb, tq), BF16)

    sample_weights = _sample_weights(qs_ref[0], kn_ref[0], k_pages, page_size)

    blk_row = lax.broadcasted_iota(jnp.int32, (nb, tq), 0)
    blk_row_f = blk_row.astype(F32)
    fully_past = blk_row < i
    key_row = lax.broadcasted_iota(jnp.int32, (MOBA_BLOCK, tq), 0)
    qry_col = lax.broadcasted_iota(jnp.int32, (MOBA_BLOCK, tq), 1)
    causal = key_row <= qry_col
    pair_row = lax.broadcasted_iota(jnp.int32, (pair_w, tq), 0)

    for h in range(n_heads):
        lo = (h // 2) * pair_w
        mine = (pair_row >= HEAD_DIM) if h % 2 else (pair_row < HEAD_DIM)
        q2 = jnp.where(mine, qt_ref[0, lo:lo + pair_w, :], jnp.zeros((), BF16))
        q2_ref[h, 0:pair_w, :] = q2

        gate = jnp.dot(km_ref[:, lo:lo + pair_w].astype(BF16), q2, preferred_element_type=F32)
        gate = jnp.where(fully_past, gate, -jnp.inf)
        chosen = jnp.zeros((nb, tq), jnp.bool_)
        for _ in range(MOBA_TOPK):
            best = jnp.max(gate, axis=0, keepdims=True)
            first = jnp.min(jnp.where(gate == best, blk_row_f, float(nb)), axis=0, keepdims=True)
            pick = blk_row_f == first
            chosen = jnp.logical_or(chosen, pick)
            gate = jnp.where(pick, -jnp.inf, gate)
        q2_ref[h, pair_w:pair_w + nb, :] = jnp.where(jnp.logical_and(chosen, fully_past), 0.0, MASKED).astype(BF16)

    def scores(js, h):
        lo = (h // 2) * pair_w
        return [jnp.dot(jnp.concatenate([kb_ref[0, j, :, lo:lo + pair_w], ind_ref[j]], axis=1), q2_ref[h],
                        preferred_element_type=F32) for j in js]

    def own_scores(js, h):
        lo = (h // 2) * pair_w
        return [jnp.dot(kb_ref[0, j, :, lo:lo + pair_w], q2_ref[h, 0:pair_w, :], preferred_element_type=F32)
                for j in js]

    look = min(SCORE_LOOKAHEAD, n_heads)

    def all_heads(js, consume, scores=scores, first=None, js_next=None):
        ahead = dict(enumerate(first)) if first is not None else {h: scores(js, h) for h in range(look)}
        nxt = []
        for h in range(n_heads):
            if h + look < n_heads:
                ahead[h + look] = scores(js, h + look)
            elif js_next is not None:
                nxt.append(scores(js_next, h + look - n_heads))
            consume(h, ahead.pop(h))
        return nxt

    def values(j, h):
        return vt_ref[0, j, h * V_ROWS:(h + 1) * V_ROWS, :]

    def first_block(h, ss):
        s = jnp.where(causal, ss[0], -jnp.inf)
        m0 = jnp.max(s, axis=0, keepdims=True)
        m_ref[h] = m0
        acc_ref[h] = jnp.dot(values(i, h), jnp.exp2(s - m0).astype(BF16), preferred_element_type=F32)

    all_heads([i], first_block, own_scores)
    os_ref[0] = _sample_values(*sample_weights, vn_ref[0], v_pages, page_size)

    def online_update(js, h, ss, m_old, acc_old):
        m_new = m_old
        for s in ss:
            m_new = jnp.maximum(m_new, jnp.max(s, axis=0, keepdims=True))
        pv = None
        for j, s in zip(js, ss):
            d = jnp.dot(values(j, h), jnp.exp2(s - m_new).astype(BF16), preferred_element_type=F32)
            pv = d if pv is None else pv + d
        return m_new, jnp.exp2(m_old - m_new) * acc_old + pv

    size, top = 1, i
    while size < PAST_GROUP:
        @pl.when((i & size) != 0)
        def _(size=size, top=top):
            js = [top - size + r for r in range(size)]

            def piece(h, ss):
                m_ref[h], acc_ref[h] = online_update(js, h, ss, m_ref[h], acc_ref[h])

            all_heads(js, piece)

        top = top - (i & size)
        size *= 2

    def group_body(t, carry):
        js = [PAST_GROUP * t + r for r in range(PAST_GROUP)]

        def group(h, ss):
            m_ref[h], acc_ref[h] = online_update(js, h, ss, m_ref[h], acc_ref[h])

        all_heads(js, group)
        return carry

    lax.fori_loop(0, i // PAST_GROUP, group_body, 0)

    for h in range(n_heads):
        acc = acc_ref[h]
        o_ref[0, h * HEAD_DIM:(h + 1) * HEAD_DIM, :] = (acc[0:HEAD_DIM] / acc[HEAD_DIM:HEAD_DIM + 1]).astype(BF16)


def _moba_call(qt, kb4, vt4, page_table, q_s, k_s, v_s, cache_kt, cache_vt):
    b, aw, t = qt.shape
    nb = t // MOBA_BLOCK
    n_heads = aw // HEAD_DIM
    n_seq, n_pages = page_table.shape
    _, _, dh, page_size = cache_kt.shape
    hg = n_heads // HEAD_GROUPS
    gw = hg * HEAD_DIM
    assert nb <= LANES and hg % 2 == 0
    assert dh == HEAD_DIM and MOBA_BLOCK % page_size == 0 and (n_pages * page_size) % MOBA_BLOCK == 0
    assert n_seq <= b * HEAD_GROUPS * nb
    kern = functools.partial(_moba_kernel, nb=nb, n_heads=hg, n_pages=n_pages, page_size=page_size)

    def seq(bi, g, i):
        return jnp.minimum((bi * HEAD_GROUPS + g) * nb + i, n_seq - 1)

    tok = pl.BlockSpec((1, 1, aw), lambda bi, g, i, pt: (seq(bi, g, i), 0, 0))

    def page_spec(p):
        return pl.BlockSpec((None, n_heads, dh, page_size), lambda bi, g, i, pt: (pt[seq(bi, g, i), p], 0, 0, 0))

    pages = [page_spec(p) for p in range(n_pages)]
    q_tile = pl.BlockSpec((1, gw, MOBA_BLOCK), lambda bi, g, i, pt: (bi, g, i))
    return pl.pallas_call(
        kern,
        grid_spec=pltpu.PrefetchScalarGridSpec(
            num_scalar_prefetch=1,
            grid=(b, HEAD_GROUPS, nb),
            in_specs=[q_tile,
                      pl.BlockSpec((1, nb, MOBA_BLOCK, gw), lambda bi, g, i, pt: (bi, 0, 0, g),
                                   pipeline_mode=pl.Buffered(1)),
                      pl.BlockSpec((1, nb, hg * V_ROWS, MOBA_BLOCK), lambda bi, g, i, pt: (bi, 0, g, 0),
                                   pipeline_mode=pl.Buffered(1)),
                      tok, tok, tok] + pages + pages,
            out_specs=[q_tile, tok],
            scratch_shapes=[pltpu.VMEM((nb, gw), F32),
                            pltpu.VMEM((nb, MOBA_BLOCK, LANES), BF16),
                            pltpu.VMEM((hg, 4 * HEAD_DIM, MOBA_BLOCK), BF16),
                            pltpu.VMEM((hg, 1, MOBA_BLOCK), F32),
                            pltpu.VMEM((hg, V_ROWS, MOBA_BLOCK), F32)]),
        out_shape=[jax.ShapeDtypeStruct((b, aw, t), BF16), jax.ShapeDtypeStruct((n_seq, 1, aw), F32)],
        compiler_params=_params("arbitrary", "arbitrary", "arbitrary"),
        name="moba",
    )(page_table, qt, kb4, vt4, q_s, k_s, v_s, *([cache_kt] * n_pages), *([cache_vt] * n_pages))


def _post_kernel(*refs, tm, d_model, d_ff, att_transposed, per_row_state):
    (x_ref, pool_ref, att_ref, mod_ref, wop_ref, woa_ref, gpost_ref, gpre2_ref, gpost2_ref,
     wup_ref, cw_ref, cb_ref, wdn_ref) = refs[:13]
    if per_row_state:
        st_ref, y_ref, stn_ref, actbuf = refs[13:]
    else:
        y_ref, ctail_ref, carry, upbuf, actbuf = refs[13:]
        i = pl.program_id(1)

        @pl.when(i == 0)
        def _():
            carry[...] = jnp.zeros(carry.shape, F32)

    d = d_model
    x = x_ref[0]
    mod = mod_ref[0]
    gate1 = mod[:, 2 * d:3 * d]
    shift2, scale2, gate2 = mod[:, 3 * d:4 * d], mod[:, 4 * d:5 * d], mod[:, 5 * d:6 * d]

    mix = jnp.dot(pool_ref[0], wop_ref[...], preferred_element_type=F32)
    if att_transposed:
        mix += lax.dot_general(att_ref[0], woa_ref[...], (((0,), (0,)), ((), ())), preferred_element_type=F32)
    else:
        mix += jnp.dot(att_ref[0], woa_ref[...], preferred_element_type=F32)
    x1 = x + gate1 * _rms(mix, gpost_ref[...])
    h2 = (_rms(x1, gpre2_ref[...]) * (1.0 + scale2) + shift2).astype(BF16)

    fc = d_ff // FFN_CHUNKS

    def up_proj(c):
        out = []
        for base in (0, d_ff):
            cols = slice(base + c * fc, base + (c + 1) * fc)
            out.append((cols, jnp.dot(h2, wup_ref[:, cols], preferred_element_type=F32)))
        return out

    def conv_act(c, ups):
        halves = []
        for half, (cols, up) in enumerate(ups):
            if per_row_state:
                prev2, prev1 = st_ref[0, :, 0, cols], st_ref[0, :, 1, cols]
                stn_ref[0, :, 0, cols] = prev1
                stn_ref[0, :, 1, cols] = up
            else:
                buf = upbuf.at[(2 * c + half) % upbuf.shape[0]]
                buf[0:SUBLANES, :] = carry[:, cols]
                buf[SUBLANES:SUBLANES + tm, :] = up
                prev1 = buf[SUBLANES - 1:SUBLANES - 1 + tm, :]
                prev2 = buf[SUBLANES - 2:SUBLANES - 2 + tm, :]
                carry[:, cols] = buf[tm:tm + SUBLANES, :]
            halves.append(cb_ref[:, cols] + cw_ref[0:1, cols] * prev2 + cw_ref[1:2, cols] * prev1
                          + cw_ref[2:3, cols] * up)
        a, g = halves
        actbuf[:, c * fc:(c + 1) * fc] = (a * jax.nn.sigmoid(a) * g).astype(BF16)

    split_at = [((s + 1) * FFN_CHUNKS) // FFN_DOWN_SPLITS for s in range(FFN_DOWN_SPLITS)]
    ahead = {c: up_proj(c) for c in range(min(FFN_LOOKAHEAD, FFN_CHUNKS))}
    ffn, done = None, 0
    for c in range(FFN_CHUNKS):
        if c + FFN_LOOKAHEAD < FFN_CHUNKS:
            ahead[c + FFN_LOOKAHEAD] = up_proj(c + FFN_LOOKAHEAD)
        conv_act(c, ahead.pop(c))
        if c + 1 in split_at:
            rows = slice(done * fc, (c + 1) * fc)
            part = jnp.dot(actbuf[:, rows], wdn_ref[rows, :], preferred_element_type=F32)
            ffn = part if ffn is None else ffn + part
            done = c + 1
    y_ref[0] = x1 + gate2 * _rms(ffn, gpost2_ref[...])
    if not per_row_state:
        ctail_ref[0] = carry[...]


def _post_call(x, pool_o, att, mod, w, *, att_transposed, state=None):
    b, t, d = x.shape
    pw = pool_o.shape[2]
    aw = w["w_out_a"].shape[0]
    d_ff = w["w_down"].shape[0]
    per_row = state is not None
    tm = t if per_row else min(POST_TILE, t)
    assert t % tm == 0 and d_ff % (FFN_CHUNKS * LANES) == 0 and CONV_WIDTH == 3
    fc = d_ff // FFN_CHUNKS
    kern = functools.partial(_post_kernel, tm=tm, d_model=d, d_ff=d_ff, att_transposed=att_transposed,
                             per_row_state=per_row)
    row_tile = lambda bi, i: (bi, i, 0)
    const2 = lambda bi, i: (0, 0)
    mod_rows = mod.shape[1]
    in_specs = [pl.BlockSpec((1, tm, d), row_tile),
                pl.BlockSpec((1, tm, pw), row_tile),
                (pl.BlockSpec((1, aw, tm), lambda bi, i: (bi, 0, i)) if att_transposed
                 else pl.BlockSpec((1, tm, aw), row_tile)),
                pl.BlockSpec((1, mod_rows, mod.shape[2]), (row_tile if mod_rows > 1 else lambda bi, i: (bi, 0, 0))),
                _resident((pw, d), const2),
                _resident((aw, d), const2),
                _resident((1, d), const2),
                _resident((1, d), const2),
                _resident((1, d), const2),
                _resident((d, 2 * d_ff), const2),
                _resident((CONV_WIDTH, 2 * d_ff), const2),
                _resident((1, 2 * d_ff), const2),
                _resident((d_ff, d), const2)]
    args = [x, pool_o, att, mod, w["w_out_p"], w["w_out_a"], w["g_mix_post"], w["g_ffn_pre"], w["g_ffn_post"],
            w["w_up"], w["conv_w"], w["conv_b"], w["w_down"]]
    if per_row:
        state_block = pl.BlockSpec((1, tm, CONV_WIDTH - 1, 2 * d_ff), lambda bi, i: (bi, i, 0, 0))
        in_specs.append(state_block)
        args.append(state)
        out_specs = [pl.BlockSpec((1, tm, d), row_tile), state_block]
        out_shape = [jax.ShapeDtypeStruct((b, t, d), F32), jax.ShapeDtypeStruct(state.shape, F32)]
        scratch = []
    else:
        out_specs = [pl.BlockSpec((1, tm, d), row_tile),
                     pl.BlockSpec((1, SUBLANES, 2 * d_ff), lambda bi, i: (bi, 0, 0))]
        out_shape = [jax.ShapeDtypeStruct((b, t, d), F32), jax.ShapeDtypeStruct((b, SUBLANES, 2 * d_ff), F32)]
        scratch = [pltpu.VMEM((SUBLANES, 2 * d_ff), F32),
                   pltpu.VMEM((4, tm + SUBLANES, fc), F32)]
    scratch.append(pltpu.VMEM((tm, d_ff), BF16))
    return pl.pallas_call(
        kern,
        grid=(b, t // tm),
        in_specs=in_specs,
        out_specs=out_specs,
        out_shape=out_shape,
        scratch_shapes=scratch,
        compiler_params=_params("arbitrary", "arbitrary"),
        name="post_sample" if per_row else "post_prompt",
    )(*args)


def _pre_sample_kernel(x_ref, mod_ref, g_ref, win_ref, poolw_ref, pscale_ref, cos_ref, sin_ref, st_ref,
                       stn_ref, q_ref, k_ref, v_ref, pool_ref, *, d_model, pw, aw, pos0):
    gc = pw // len(POOL_WINDOWS)
    x = x_ref[...]
    mod = mod_ref[...]
    shift, scale = mod[:, 0:d_model], mod[:, d_model:2 * d_model]
    h = _rms(x, g_ref[...]) * (1.0 + scale) + shift
    proj = jnp.dot(h.astype(BF16), win_ref[...], preferred_element_type=F32)
    u = proj[:, 0:pw]
    cos, sin_signed = cos_ref[...], sin_ref[...]
    q_ref[...] = _rope(proj[:, pw:pw + aw], cos, sin_signed)
    k_ref[...] = _rope(proj[:, pw + aw:pw + 2 * aw], cos, sin_signed)
    v_ref[...] = proj[:, pw + 2 * aw:pw + 3 * aw]

    pooled = []
    for g, w in enumerate(POOL_WINDOWS):
        cols = slice(g * gc, (g + 1) * gc)
        win = u[:, cols]
        for r in range(POOL_STATE - (w - 1), POOL_STATE):
            win = win + st_ref[r, :, cols]
        pooled.append(win / float(min(w, pos0 + 1)) - u[:, cols])
    pool_ref[...] = _pool_project(jnp.concatenate(pooled, axis=1), poolw_ref, pscale_ref[...]).astype(BF16)

    for r in range(POOL_STATE - 1):
        stn_ref[r] = st_ref[r + 1]
    stn_ref[POOL_STATE - 1] = u


def _pre_sample_call(x, mod, g_pre, w_in_b, pool_w_b, pool_scale, cos_t, sin_t, state_pool, *, pw, aw, pos0):
    rows, d = x.shape
    kern = functools.partial(_pre_sample_kernel, d_model=d, pw=pw, aw=aw, pos0=pos0)
    return pl.pallas_call(
        kern,
        out_shape=[jax.ShapeDtypeStruct((POOL_STATE, rows, pw), F32),
                   jax.ShapeDtypeStruct((rows, aw), F32),
                   jax.ShapeDtypeStruct((rows, aw), F32),
                   jax.ShapeDtypeStruct((rows, aw), F32),
                   jax.ShapeDtypeStruct((rows, pw), BF16)],
        compiler_params=pltpu.CompilerParams(vmem_limit_bytes=VMEM_LIMIT_BYTES),
        name="pre_sample",
    )(x, mod, g_pre, w_in_b, pool_w_b, pool_scale, cos_t, sin_t, state_pool)


def _own_lanes(n_heads):
    aw = n_heads * HEAD_DIM
    head_row = lax.broadcasted_iota(jnp.int32, (n_heads, aw), 0)
    head_of_lane = lax.broadcasted_iota(jnp.int32, (n_heads, aw), 1) // HEAD_DIM
    return head_row == head_of_lane


def _sample_weights(q, k_new, k_refs, page_size):
    n_pages = len(k_refs)
    n_heads = k_refs[0].shape[0]
    aw = n_heads * HEAD_DIM
    ppb = MOBA_BLOCK // page_size
    n_blocks = n_pages // ppb

    qbd = jnp.where(_own_lanes(n_heads), q * (HEAD_DIM ** -0.5 * LOG2E), 0.0)
    qbd_b = qbd.astype(BF16)

    scores = []
    for n in range(n_blocks):
        kt = jnp.concatenate([k_refs[p][...].reshape(aw, page_size).astype(BF16)
                              for p in range(n * ppb, (n + 1) * ppb)], axis=1)
        s = jnp.dot(qbd_b, kt, preferred_element_type=F32)
        scores += [s[:, r * page_size:(r + 1) * page_size] for r in range(ppb)]

    pages_of = lambda n: range(n * ppb, (n + 1) * ppb)
    gates = [sum(jnp.sum(scores[p], axis=-1, keepdims=True) for p in pages_of(n)) for n in range(n_blocks)]
    peaks = [functools.reduce(jnp.maximum, [jnp.max(scores[p], axis=-1, keepdims=True) for p in pages_of(n)])
             for n in range(n_blocks)]
    chosen = []
    for n in range(n_blocks):
        beaten = jnp.zeros((n_heads, 1), jnp.int32)
        for m in range(n_blocks):
            if m != n:
                wins = (gates[m] >= gates[n]) if m < n else (gates[m] > gates[n])
                beaten = beaten + wins.astype(jnp.int32)
        chosen.append(beaten < MOBA_TOPK)

    s_new = jnp.sum(qbd * k_new, axis=-1, keepdims=True)
    m = s_new
    for n in range(n_blocks):
        m = jnp.maximum(m, jnp.where(chosen[n], peaks[n], -jnp.inf))
    p_new = jnp.exp2(s_new - m)
    weights = [jnp.exp2(jnp.where(chosen[p // ppb], scores[p], -jnp.inf) - m) for p in range(n_pages)]
    inv_l = 1.0 / (p_new + sum(jnp.sum(w, axis=-1, keepdims=True) for w in weights))
    return weights, p_new, inv_l


def _sample_values(weights, p_new, inv_l, v_new, v_refs, page_size):
    n_pages = len(v_refs)
    n_heads = v_refs[0].shape[0]
    partial = []
    for h in range(n_heads):
        acc = None
        for p in range(n_pages):
            term = v_refs[p][h] * weights[p][h:h + 1, :]
            acc = term if acc is None else acc + term
        partial.append(acc)
    partial = jnp.concatenate(partial, axis=0)
    ones = jnp.ones((SUBLANES, page_size), BF16)
    high = partial.astype(BF16)
    low = (partial - high.astype(F32)).astype(BF16)
    contract_lanes = (((1,), (1,)), ((), ()))
    past = (lax.dot_general(ones, high, contract_lanes, preferred_element_type=F32)
            + lax.dot_general(ones, low, contract_lanes, preferred_element_type=F32))[0:1, :]
    own_lanes = _own_lanes(n_heads)
    inv_row = jnp.sum(jnp.where(own_lanes, inv_l, 0.0), axis=0, keepdims=True)
    own_row = jnp.sum(jnp.where(own_lanes, p_new, 0.0), axis=0, keepdims=True)
    return (past + own_row * v_new) * inv_row


def _rope_tables(positions):
    half = HEAD_DIM // 2
    lane = jnp.arange(LANES, dtype=jnp.int32)
    inv = ROPE_THETA ** (-(lane % half).astype(F32) / half)
    sign = jnp.where((lane % HEAD_DIM) < half, -1.0, 1.0).astype(F32)
    ang = positions.astype(F32)[:, None] * inv[None, :]
    return jnp.cos(ang), jnp.sin(ang) * sign[None, :]


def kernel(x_prompt, x_sample, c_prompt, c_sample, cache_k, cache_v, page_table, state_pool, state_conv, w_ada, b_ada, g_mix_pre, g_mix_post, w_in, pool_w, pool_scale, w_out, g_ffn_pre, g_ffn_post, w_up, conv_w, conv_b, w_down):
    depth = w_ada.shape[0]
    b, t, d = x_prompt.shape
    n_seq, dec_t, _ = x_sample.shape
    n_pages = page_table.shape[1]
    page_size, n_heads = cache_k.shape[2], cache_k.shape[3]
    past_len = n_pages * page_size
    aw = n_heads * HEAD_DIM
    pw = state_pool.shape[3]
    d_ff = w_down.shape[1]
    assert dec_t == 1 and t % MOBA_BLOCK == 0 and t % PRE_TILE == 0 and pw == d - aw
    assert (pw // len(POOL_WINDOWS)) % LANES == 0 and w_in.shape[2] == pw + 3 * aw

    cos_p, sin_p = _rope_tables(jnp.arange(t, dtype=jnp.int32))
    cos_s, sin_s = _rope_tables(past_len + jnp.arange(dec_t, dtype=jnp.int32))

    y_p, y_s = x_prompt, x_sample.reshape(1, n_seq, d)
    outs = [[] for _ in range(8)]
    for l in range(depth):
        w = dict(w_out_p=w_out[l, :pw].astype(BF16), w_out_a=w_out[l, pw:].astype(BF16),
                 g_mix_post=g_mix_post[l].reshape(1, d), g_ffn_pre=g_ffn_pre[l].reshape(1, d),
                 g_ffn_post=g_ffn_post[l].reshape(1, d), w_up=w_up[l].astype(BF16), conv_w=conv_w[l],
                 conv_b=conv_b[l].reshape(1, 2 * d_ff), w_down=w_down[l].astype(BF16))
        g_pre = g_mix_pre[l].reshape(1, d)
        w_in_b = w_in[l].astype(BF16)
        pool_w_b = pool_w[l].astype(BF16)
        pscale = pool_scale[l].reshape(1, pw)

        mod_p, mod_s = _mod_call(c_prompt, c_sample, w_ada[l], b_ada[l])
        mod_p, mod_s = mod_p.reshape(b, 1, 6 * d), mod_s.reshape(1, n_seq, 6 * d)

        k_p, v_p, qt, kb, vt4, pool_p, utail = _pre_prompt_call(
            y_p, mod_p, g_pre, w_in_b, pool_w_b, pscale, cos_p, sin_p, pw=pw, aw=aw)
        pool_state_s, q_s, k_s, v_s, pool_s = _pre_sample_call(
            y_s[0], mod_s[0], g_pre, w_in_b, pool_w_b, pscale, cos_s, sin_s, jnp.swapaxes(state_pool[l], 0, 1),
            pw=pw, aw=aw, pos0=past_len)
        tok = lambda a: a.reshape(n_seq, 1, aw)
        att_t, att_s = _moba_call(qt, kb.reshape(b, t // MOBA_BLOCK, MOBA_BLOCK, aw), vt4,
                                  page_table, tok(q_s), tok(k_s), tok(v_s),
                                  jnp.transpose(cache_k[l], (0, 2, 3, 1)), jnp.transpose(cache_v[l], (0, 2, 3, 1)))
        y_p, ctail = _post_call(y_p, pool_p, att_t, mod_p, w, att_transposed=True)
        y_s, conv_state_s = _post_call(y_s, pool_s.reshape(1, n_seq, pw), att_s.reshape(1, n_seq, aw).astype(BF16),
                                       mod_s, w, att_transposed=False, state=state_conv[l][None])

        outs[0].append(k_p.reshape(b, t, n_heads, HEAD_DIM))
        outs[1].append(v_p.reshape(b, t, n_heads, HEAD_DIM))
        outs[2].append(utail[:, POOL_HALO - POOL_STATE:])
        outs[3].append(ctail[:, SUBLANES - (CONV_WIDTH - 1):])
        outs[4].append(k_s.reshape(n_seq, dec_t, n_heads, HEAD_DIM))
        outs[5].append(v_s.reshape(n_seq, dec_t, n_heads, HEAD_DIM))
        outs[6].append(jnp.swapaxes(pool_state_s, 0, 1))
        outs[7].append(conv_state_s[0])
    stacked = [jnp.stack(o) for o in outs]
    return (y_p, y_s.reshape(n_seq, dec_t, d), *stacked)
```

```python
import functools
import math

import jax
import jax.numpy as jnp
from jax import lax
from jax.experimental import pallas as pl
from jax.experimental.pallas import tpu as pltpu

HEAD_DIM = 64
MOBA_BLOCK = 256
MOBA_TOPK = 3
POOL_WINDOWS = (2, 4, 8, 16)
POOL_STATE = max(POOL_WINDOWS) - 1
CONV_WIDTH = 3
ROPE_THETA = 10000.0
RMS_EPS = 1e-6

LANES = 128
SUBLANES = 8
VMEM_LIMIT_BYTES = 56 * 1024 * 1024

BF16_ROWS = 16
V_ROWS = HEAD_DIM + BF16_ROWS

POOL_HALO = 16
PRE_TILE = 512
POST_TILE = 512
FFN_CHUNKS = 11
FFN_LOOKAHEAD = 2
FFN_DOWN_SPLITS = 2
HEAD_GROUPS = 2
PAST_GROUP = 4
SOFTMAX_BLOCKS = 1
SCORE_LOOKAHEAD = 6

F32 = jnp.float32
BF16 = jnp.bfloat16
LOG2E = math.log2(math.e)
MASKED = -1e30


def _params(*semantics):
    return pltpu.CompilerParams(dimension_semantics=semantics, vmem_limit_bytes=VMEM_LIMIT_BYTES)


def _resident(shape, index_map):
    return pl.BlockSpec(shape, index_map, pipeline_mode=pl.Buffered(1))


def _rms(x, g):
    ms = jnp.mean(x * x, axis=-1, keepdims=True)
    return x * lax.rsqrt(ms + RMS_EPS) * g


def _rope(x, cos, sin_signed):
    lane = lax.broadcasted_iota(jnp.int32, (1, LANES), 1)
    first_half = (lane % HEAD_DIM) < HEAD_DIM // 2
    half = HEAD_DIM // 2
    outs = []
    for c in range(x.shape[1] // LANES):
        blk = x[:, c * LANES:(c + 1) * LANES]
        partner = jnp.where(first_half, pltpu.roll(blk, LANES - half, 1), pltpu.roll(blk, half, 1))
        outs.append(blk * cos + partner * sin_signed)
    return jnp.concatenate(outs, axis=1)


def _pool_project(pooled, poolw_ref, pscale):
    n_groups, gc, _ = poolw_ref.shape
    outs = [jnp.dot(pooled[:, g * gc:(g + 1) * gc].astype(BF16), poolw_ref[g], preferred_element_type=F32)
            for g in range(n_groups)]
    return jnp.concatenate(outs, axis=1) * pscale


def _mod_kernel(cp_ref, cs_ref, w_ref, b_ref, op_ref, os_ref):
    w = w_ref[...].astype(BF16)
    for c_ref, o_ref in ((cp_ref, op_ref), (cs_ref, os_ref)):
        c = c_ref[...]
        a = (c * jax.nn.sigmoid(c)).astype(BF16)
        o_ref[...] = jnp.dot(a, w, preferred_element_type=F32) + b_ref[...]


def _mod_call(c_prompt, c_sample, w_ada, b_ada):
    d, n_out = w_ada.shape
    rows_p, rows_s = c_prompt.shape[0], c_sample.shape[0]
    full = lambda j: (0, 0)
    col = lambda j: (0, j)
    return pl.pallas_call(
        _mod_kernel,
        grid=(n_out // d,),
        in_specs=[pl.BlockSpec((rows_p, d), full), pl.BlockSpec((rows_s, d), full),
                  pl.BlockSpec((d, d), col), pl.BlockSpec((1, d), col)],
        out_specs=[pl.BlockSpec((rows_p, d), col), pl.BlockSpec((rows_s, d), col)],
        out_shape=[jax.ShapeDtypeStruct((rows_p, n_out), F32), jax.ShapeDtypeStruct((rows_s, n_out), F32)],
        compiler_params=_params("arbitrary"),
        name="adaln_mod",
    )(c_prompt, c_sample, w_ada, b_ada.reshape(1, n_out))


def _pre_prompt_kernel(x_ref, mod_ref, g_ref, win_ref, poolw_ref, pscale_ref, cos_ref, sin_ref, cost_ref, sint_ref,
                       k_ref, v_ref, qt_ref, kb_ref, vt_ref, pool_ref, utail_ref,
                       ubuf, s2buf, s4buf, s8buf, *, tm, d_model, pw, aw, q_scale):
    i = pl.program_id(1)
    halo = POOL_HALO
    gc = pw // len(POOL_WINDOWS)
    bufs = (ubuf, s2buf, s4buf, s8buf)

    @pl.when(i == 0)
    def _():
        for buf in bufs:
            buf[0:halo, :] = jnp.zeros((halo, buf.shape[1]), F32)

    x = x_ref[0]
    mod = mod_ref[0]
    shift, scale = mod[:, 0:d_model], mod[:, d_model:2 * d_model]
    h = _rms(x, g_ref[...]) * (1.0 + scale) + shift
    proj = jnp.dot(h.astype(BF16), win_ref[...], preferred_element_type=F32)
    u = proj[:, 0:pw]
    cos_t, sin_t, cos_r, sin_r = cost_ref[0], sint_ref[0], cos_ref[...], sin_ref[...]
    cos = cos_t * cos_r - sin_t * sin_r
    sin_signed = sin_t * cos_r + cos_t * sin_r
    q = _rope(proj[:, pw:pw + aw], cos, sin_signed)
    k = _rope(proj[:, pw + aw:pw + 2 * aw], cos, sin_signed)
    v = proj[:, pw + 2 * aw:pw + 3 * aw]

    k_ref[0] = k
    v_ref[0] = v
    kb_ref[0] = k.astype(BF16)
    qt_ref[0] = (q * q_scale).T.astype(BF16)
    ones_pad = jnp.where(lax.broadcasted_iota(jnp.int32, (BF16_ROWS, MOBA_BLOCK), 0) == 0, 1.0, 0.0).astype(BF16)
    for r in range(tm // MOBA_BLOCK):
        vt = v[r * MOBA_BLOCK:(r + 1) * MOBA_BLOCK, :].T.astype(BF16)
        for hd in range(aw // HEAD_DIM):
            vt_ref[0, r, hd * V_ROWS:hd * V_ROWS + HEAD_DIM, :] = vt[hd * HEAD_DIM:(hd + 1) * HEAD_DIM, :]
            vt_ref[0, r, hd * V_ROWS + HEAD_DIM:(hd + 1) * V_ROWS, :] = ones_pad

    ubuf[halo:halo + tm, :] = u
    s2 = u + ubuf[halo - 1:halo - 1 + tm, :]
    s2buf[halo:halo + tm, :] = s2[:, gc:]
    s4 = s2[:, gc:] + s2buf[halo - 2:halo - 2 + tm, :]
    s4buf[halo:halo + tm, :] = s4[:, gc:]
    s8 = s4[:, gc:] + s4buf[halo - 4:halo - 4 + tm, :]
    s8buf[halo:halo + tm, :] = s8[:, gc:]
    s16 = s8[:, gc:] + s8buf[halo - 8:halo - 8 + tm, :]
    wins = (s2[:, 0:gc], s4[:, 0:gc], s8[:, 0:gc], s16)
    pos1 = (i * tm + 1 + lax.broadcasted_iota(jnp.int32, (tm, 1), 0)).astype(F32)
    pooled = jnp.concatenate(
        [wins[g] / jnp.minimum(float(w), pos1) - u[:, g * gc:(g + 1) * gc] for g, w in enumerate(POOL_WINDOWS)],
        axis=1)
    pool_ref[0] = _pool_project(pooled, poolw_ref, pscale_ref[...]).astype(BF16)

    for buf in bufs:
        buf[0:halo, :] = buf[tm:tm + halo, :]
    utail_ref[0] = ubuf[0:halo, :]


def _pre_prompt_call(x, mod, g_pre, w_in_b, pool_w_b, pool_scale, rope, *, pw, aw):
    b, t, d = x.shape
    tm = min(PRE_TILE, t)
    nb = t // MOBA_BLOCK
    gc = pw // len(POOL_WINDOWS)
    in_w = w_in_b.shape[1]
    vt_rows = (aw // HEAD_DIM) * V_ROWS
    kern = functools.partial(_pre_prompt_kernel, tm=tm, d_model=d, pw=pw, aw=aw,
                             q_scale=HEAD_DIM ** -0.5 * LOG2E)
    row_tile = lambda bi, i: (bi, i, 0)
    const2 = lambda bi, i: (0, 0)
    return pl.pallas_call(
        kern,
        grid=(b, t // tm),
        in_specs=[pl.BlockSpec((1, tm, d), row_tile),
                  pl.BlockSpec((1, 1, mod.shape[2]), lambda bi, i: (bi, 0, 0)),
                  _resident((1, d), const2),
                  _resident((d, in_w), const2),
                  _resident(pool_w_b.shape, lambda bi, i: (0, 0, 0)),
                  _resident((1, pw), const2),
                  _resident((tm, LANES), const2),
                  _resident((tm, LANES), const2),
                  pl.BlockSpec((1, 1, LANES), lambda bi, i: (i, 0, 0)),
                  pl.BlockSpec((1, 1, LANES), lambda bi, i: (i, 0, 0))],
        out_specs=[pl.BlockSpec((1, tm, aw), row_tile),
                   pl.BlockSpec((1, tm, aw), row_tile),
                   pl.BlockSpec((1, aw, tm), lambda bi, i: (bi, 0, i)),
                   pl.BlockSpec((1, tm, aw), row_tile),
                   pl.BlockSpec((1, tm // MOBA_BLOCK, vt_rows, MOBA_BLOCK), lambda bi, i: (bi, i, 0, 0)),
                   pl.BlockSpec((1, tm, pw), row_tile),
                   pl.BlockSpec((1, POOL_HALO, pw), lambda bi, i: (bi, 0, 0))],
        out_shape=[jax.ShapeDtypeStruct((b, t, aw), F32),
                   jax.ShapeDtypeStruct((b, t, aw), F32),
                   jax.ShapeDtypeStruct((b, aw, t), BF16),
                   jax.ShapeDtypeStruct((b, t, aw), BF16),
                   jax.ShapeDtypeStruct((b, nb, vt_rows, MOBA_BLOCK), BF16),
                   jax.ShapeDtypeStruct((b, t, pw), BF16),
                   jax.ShapeDtypeStruct((b, POOL_HALO, pw), F32)],
        scratch_shapes=[pltpu.VMEM((tm + POOL_HALO, pw), F32),
                        pltpu.VMEM((tm + POOL_HALO, pw - gc), F32),
                        pltpu.VMEM((tm + POOL_HALO, pw - 2 * gc), F32),
                        pltpu.VMEM((tm + POOL_HALO, pw - 3 * gc), F32)],
        compiler_params=_params("arbitrary", "arbitrary"),
        name="pre_prompt",
    )(x, mod, g_pre, w_in_b, pool_w_b, pool_scale, *rope)


def _moba_kernel(pt_ref, qt_ref, kb_ref, vt_ref, qs_ref, kn_ref, vn_ref, *refs, nb, n_heads, n_pages, page_size):
    del pt_ref
    k_pages, v_pages = refs[:n_pages], refs[n_pages:2 * n_pages]
    o_ref, os_ref, km_ref, ind_ref, q2_ref, m_ref, acc_ref = refs[2 * n_pages:]

    i = pl.program_id(2)
    tq = MOBA_BLOCK
    pair_w = 2 * HEAD_DIM

    @pl.when(i == 0)
    def _():
        lane = lax.broadcasted_iota(jnp.int32, (MOBA_BLOCK, LANES), 1)
        for n in range(nb):
            km_ref[n:n + 1, :] = jnp.mean(kb_ref[0, n].astype(F32), axis=0, keepdims=True)
            ind_ref[n] = jnp.where(lane == n, 1.0, 0.0).astype(BF16)
        for h in range(n_heads):
            q2_ref[h, pair_w + nb:, :] = jnp.zeros((pair_w - nb, tq), BF16)

    sample_weights = _sample_weights(qs_ref[0], kn_ref[0], k_pages, page_size)

    blk_row = lax.broadcasted_iota(jnp.int32, (nb, tq), 0)
    blk_row_f = blk_row.astype(F32)
    fully_past = blk_row < i
    key_row = lax.broadcasted_iota(jnp.int32, (MOBA_BLOCK, tq), 0)
    qry_col = lax.broadcasted_iota(jnp.int32, (MOBA_BLOCK, tq), 1)
    causal = key_row <= qry_col
    pair_row = lax.broadcasted_iota(jnp.int32, (pair_w, tq), 0)

    for h in range(n_heads):
        lo = (h // 2) * pair_w
        mine = (pair_row >= HEAD_DIM) if h % 2 else (pair_row < HEAD_DIM)
        q2 = jnp.where(mine, qt_ref[0, lo:lo + pair_w, :], jnp.zeros((), BF16))
        q2_ref[h, 0:pair_w, :] = q2

        gate = jnp.dot(km_ref[:, lo:lo + pair_w].astype(BF16), q2, preferred_element_type=F32)
        gate = jnp.where(fully_past, gate, -jnp.inf)
        chosen = jnp.zeros((nb, tq), jnp.bool_)
        for _ in range(MOBA_TOPK):
            best = jnp.max(gate, axis=0, keepdims=True)
            first = jnp.min(jnp.where(gate == best, blk_row_f, float(nb)), axis=0, keepdims=True)
            pick = blk_row_f == first
            chosen = jnp.logical_or(chosen, pick)
            gate = jnp.where(pick, -jnp.inf, gate)
        q2_ref[h, pair_w:pair_w + nb, :] = jnp.where(jnp.logical_and(chosen, fully_past), 0.0, MASKED).astype(BF16)

    def scores(js, h):
        lo = (h // 2) * pair_w
        return [jnp.dot(jnp.concatenate([kb_ref[0, j, :, lo:lo + pair_w], ind_ref[j]], axis=1), q2_ref[h],
                        preferred_element_type=F32) for j in js]

    def own_scores(js, h):
        lo = (h // 2) * pair_w
        return [jnp.dot(kb_ref[0, j, :, lo:lo + pair_w], q2_ref[h, 0:pair_w, :], preferred_element_type=F32)
                for j in js]

    def all_heads(js, consume, scores=scores):
        items = [(js[r:r + SOFTMAX_BLOCKS], h) for r in range(0, len(js), SOFTMAX_BLOCKS) for h in range(n_heads)]
        look = min(SCORE_LOOKAHEAD, len(items))
        ahead = {k: scores(*items[k]) for k in range(look)}
        for k, (sub, h) in enumerate(items):
            if k + look < len(items):
                ahead[k + look] = scores(*items[k + look])
            consume(sub, h, ahead.pop(k))

    def values(j, h):
        return vt_ref[0, j, h * V_ROWS:(h + 1) * V_ROWS, :]

    def first_block(_, h, ss):
        s = jnp.where(causal, ss[0], -jnp.inf)
        m0 = jnp.max(s, axis=0, keepdims=True)
        m_ref[h] = m0
        acc_ref[h] = jnp.dot(values(i, h), jnp.exp2(s - m0).astype(BF16), preferred_element_type=F32)

    all_heads([i], first_block, own_scores)
    os_ref[0] = _sample_values(*sample_weights, vn_ref[0], v_pages, page_size)

    def online_update(js, h, ss, m_old, acc_old):
        m_new = m_old
        for s in ss:
            m_new = jnp.maximum(m_new, jnp.max(s, axis=0, keepdims=True))
        pv = None
        for j, s in zip(js, ss):
            d = jnp.dot(values(j, h), jnp.exp2(s - m_new).astype(BF16), preferred_element_type=F32)
            pv = d if pv is None else pv + d
        return m_new, jnp.exp2(m_old - m_new) * acc_old + pv

    def past_blocks(js, h, ss):
        m_ref[h], acc_ref[h] = online_update(js, h, ss, m_ref[h], acc_ref[h])

    size, top = 1, i
    while size < PAST_GROUP:
        @pl.when((i & size) != 0)
        def _(size=size, top=top):
            all_heads([top - size + r for r in range(size)], past_blocks)

        top = top - (i & size)
        size *= 2

    def group_body(t, carry):
        all_heads([PAST_GROUP * t + r for r in range(PAST_GROUP)], past_blocks)
        return carry

    lax.fori_loop(0, i // PAST_GROUP, group_body, 0)

    for h in range(n_heads):
        acc = acc_ref[h]
        o_ref[0, h * HEAD_DIM:(h + 1) * HEAD_DIM, :] = (acc[0:HEAD_DIM] / acc[HEAD_DIM:HEAD_DIM + 1]).astype(BF16)


def _moba_call(qt, kb4, vt4, page_table, q_s, k_s, v_s, cache_kt, cache_vt):
    b, aw, t = qt.shape
    nb = t // MOBA_BLOCK
    n_heads = aw // HEAD_DIM
    n_seq, n_pages = page_table.shape
    _, _, dh, page_size = cache_kt.shape
    hg = n_heads // HEAD_GROUPS
    gw = hg * HEAD_DIM
    assert nb <= LANES and hg % 2 == 0
    assert dh == HEAD_DIM and MOBA_BLOCK % page_size == 0 and (n_pages * page_size) % MOBA_BLOCK == 0
    assert n_seq <= b * HEAD_GROUPS * nb
    kern = functools.partial(_moba_kernel, nb=nb, n_heads=hg, n_pages=n_pages, page_size=page_size)

    def seq(bi, g, i):
        return jnp.minimum((bi * HEAD_GROUPS + g) * nb + i, n_seq - 1)

    tok = pl.BlockSpec((1, 1, aw), lambda bi, g, i, pt: (seq(bi, g, i), 0, 0))

    def page_spec(p):
        return pl.BlockSpec((None, n_heads, dh, page_size), lambda bi, g, i, pt: (pt[seq(bi, g, i), p], 0, 0, 0))

    pages = [page_spec(p) for p in range(n_pages)]
    q_tile = pl.BlockSpec((1, gw, MOBA_BLOCK), lambda bi, g, i, pt: (bi, g, i))
    return pl.pallas_call(
        kern,
        grid_spec=pltpu.PrefetchScalarGridSpec(
            num_scalar_prefetch=1,
            grid=(b, HEAD_GROUPS, nb),
            in_specs=[q_tile,
                      pl.BlockSpec((1, nb, MOBA_BLOCK, gw), lambda bi, g, i, pt: (bi, 0, 0, g),
                                   pipeline_mode=pl.Buffered(1)),
                      pl.BlockSpec((1, nb, hg * V_ROWS, MOBA_BLOCK), lambda bi, g, i, pt: (bi, 0, g, 0),
                                   pipeline_mode=pl.Buffered(1)),
                      tok, tok, tok] + pages + pages,
            out_specs=[q_tile, tok],
            scratch_shapes=[pltpu.VMEM((nb, gw), F32),
                            pltpu.VMEM((nb, MOBA_BLOCK, LANES), BF16),
                            pltpu.VMEM((hg, 4 * HEAD_DIM, MOBA_BLOCK), BF16),
                            pltpu.VMEM((hg, 1, MOBA_BLOCK), F32),
                            pltpu.VMEM((hg, V_ROWS, MOBA_BLOCK), F32)]),
        out_shape=[jax.ShapeDtypeStruct((b, aw, t), BF16), jax.ShapeDtypeStruct((n_seq, 1, aw), F32)],
        compiler_params=_params("arbitrary", "arbitrary", "arbitrary"),
        name="moba",
    )(page_table, qt, kb4, vt4, q_s, k_s, v_s, *([cache_kt] * n_pages), *([cache_vt] * n_pages))


def _post_kernel(*refs, tm, d_model, d_ff, att_transposed, per_row_state):
    (x_ref, pool_ref, att_ref, mod_ref, wop_ref, woa_ref, gpost_ref, gpre2_ref, gpost2_ref,
     wup_ref, cw_ref, cb_ref, wdn_ref) = refs[:13]
    if per_row_state:
        st_ref, y_ref, stn_ref, actbuf = refs[13:]
    else:
        y_ref, ctail_ref, carry, upbuf, actbuf = refs[13:]
        i = pl.program_id(1)

        @pl.when(i == 0)
        def _():
            carry[...] = jnp.zeros(carry.shape, F32)

    d = d_model
    x = x_ref[0]
    mod = mod_ref[0]
    gate1 = mod[:, 2 * d:3 * d]
    shift2, scale2, gate2 = mod[:, 3 * d:4 * d], mod[:, 4 * d:5 * d], mod[:, 5 * d:6 * d]

    mix = jnp.dot(pool_ref[0], wop_ref[...], preferred_element_type=F32)
    if att_transposed:
        mix += lax.dot_general(att_ref[0], woa_ref[...], (((0,), (0,)), ((), ())), preferred_element_type=F32)
    else:
        mix += jnp.dot(att_ref[0], woa_ref[...], preferred_element_type=F32)
    x1 = x + gate1 * _rms(mix, gpost_ref[...])
    h2 = (_rms(x1, gpre2_ref[...]) * (1.0 + scale2) + shift2).astype(BF16)

    fc = d_ff // FFN_CHUNKS

    def up_proj(c):
        out = []
        for base in (0, d_ff):
            cols = slice(base + c * fc, base + (c + 1) * fc)
            out.append((cols, jnp.dot(h2, wup_ref[:, cols], preferred_element_type=F32)))
        return out

    def conv_act(c, ups):
        halves = []
        for half, (cols, up) in enumerate(ups):
            if per_row_state:
                prev2, prev1 = st_ref[0, :, 0, cols], st_ref[0, :, 1, cols]
                stn_ref[0, :, 0, cols] = prev1
                stn_ref[0, :, 1, cols] = up
            else:
                buf = upbuf.at[(2 * c + half) % upbuf.shape[0]]
                buf[0:SUBLANES, :] = carry[:, cols]
                buf[SUBLANES:SUBLANES + tm, :] = up
                prev1 = buf[SUBLANES - 1:SUBLANES - 1 + tm, :]
                prev2 = buf[SUBLANES - 2:SUBLANES - 2 + tm, :]
                carry[:, cols] = buf[tm:tm + SUBLANES, :]
            halves.append(cb_ref[:, cols] + cw_ref[0:1, cols] * prev2 + cw_ref[1:2, cols] * prev1
                          + cw_ref[2:3, cols] * up)
        a, g = halves
        actbuf[:, c * fc:(c + 1) * fc] = (a * jax.nn.sigmoid(a) * g).astype(BF16)

    split_at = [((s + 1) * FFN_CHUNKS) // FFN_DOWN_SPLITS for s in range(FFN_DOWN_SPLITS)]
    ahead = {c: up_proj(c) for c in range(min(FFN_LOOKAHEAD, FFN_CHUNKS))}
    ffn, done = None, 0
    for c in range(FFN_CHUNKS):
        if c + FFN_LOOKAHEAD < FFN_CHUNKS:
            ahead[c + FFN_LOOKAHEAD] = up_proj(c + FFN_LOOKAHEAD)
        conv_act(c, ahead.pop(c))
        if c + 1 in split_at:
            rows = slice(done * fc, (c + 1) * fc)
            part = jnp.dot(actbuf[:, rows], wdn_ref[rows, :], preferred_element_type=F32)
            ffn = part if ffn is None else ffn + part
            done = c + 1
    y_ref[0] = x1 + gate2 * _rms(ffn, gpost2_ref[...])
    if not per_row_state:
        ctail_ref[0] = carry[...]


def _post_call(x, pool_o, att, mod, w, *, att_transposed, state=None):
    b, t, d = x.shape
    pw = pool_o.shape[2]
    aw = w["w_out_a"].shape[0]
    d_ff = w["w_down"].shape[0]
    per_row = state is not None
    tm = t if per_row else min(POST_TILE, t)
    assert t % tm == 0 and d_ff % (FFN_CHUNKS * LANES) == 0 and CONV_WIDTH == 3
    fc = d_ff // FFN_CHUNKS
    kern = functools.partial(_post_kernel, tm=tm, d_model=d, d_ff=d_ff, att_transposed=att_transposed,
                             per_row_state=per_row)
    row_tile = lambda bi, i: (bi, i, 0)
    const2 = lambda bi, i: (0, 0)
    mod_rows = mod.shape[1]
    in_specs = [pl.BlockSpec((1, tm, d), row_tile),
                pl.BlockSpec((1, tm, pw), row_tile),
                (pl.BlockSpec((1, aw, tm), lambda bi, i: (bi, 0, i)) if att_transposed
                 else pl.BlockSpec((1, tm, aw), row_tile)),
                pl.BlockSpec((1, mod_rows, mod.shape[2]), (row_tile if mod_rows > 1 else lambda bi, i: (bi, 0, 0))),
                _resident((pw, d), const2),
                _resident((aw, d), const2),
                _resident((1, d), const2),
                _resident((1, d), const2),
                _resident((1, d), const2),
                _resident((d, 2 * d_ff), const2),
                _resident((CONV_WIDTH, 2 * d_ff), const2),
                _resident((1, 2 * d_ff), const2),
                _resident((d_ff, d), const2)]
    args = [x, pool_o, att, mod, w["w_out_p"], w["w_out_a"], w["g_mix_post"], w["g_ffn_pre"], w["g_ffn_post"],
            w["w_up"], w["conv_w"], w["conv_b"], w["w_down"]]
    if per_row:
        state_block = pl.BlockSpec((1, tm, CONV_WIDTH - 1, 2 * d_ff), lambda bi, i: (bi, i, 0, 0))
        in_specs.append(state_block)
        args.append(state)
        out_specs = [pl.BlockSpec((1, tm, d), row_tile), state_block]
        out_shape = [jax.ShapeDtypeStruct((b, t, d), F32), jax.ShapeDtypeStruct(state.shape, F32)]
        scratch = []
    else:
        out_specs = [pl.BlockSpec((1, tm, d), row_tile),
                     pl.BlockSpec((1, SUBLANES, 2 * d_ff), lambda bi, i: (bi, 0, 0))]
        out_shape = [jax.ShapeDtypeStruct((b, t, d), F32), jax.ShapeDtypeStruct((b, SUBLANES, 2 * d_ff), F32)]
        scratch = [pltpu.VMEM((SUBLANES, 2 * d_ff), F32),
                   pltpu.VMEM((4, tm + SUBLANES, fc), F32)]
    scratch.append(pltpu.VMEM((tm, d_ff), BF16))
    return pl.pallas_call(
        kern,
        grid=(b, t // tm),
        in_specs=in_specs,
        out_specs=out_specs,
        out_shape=out_shape,
        scratch_shapes=scratch,
        compiler_params=_params("arbitrary", "arbitrary"),
        name="post_sample" if per_row else "post_prompt",
    )(*args)


def _pre_sample_kernel(x_ref, mod_ref, g_ref, win_ref, poolw_ref, pscale_ref, cos_ref, sin_ref, st_ref,
                       stn_ref, q_ref, k_ref, v_ref, pool_ref, *, d_model, pw, aw, pos0):
    gc = pw // len(POOL_WINDOWS)
    x = x_ref[...]
    mod = mod_ref[...]
    shift, scale = mod[:, 0:d_model], mod[:, d_model:2 * d_model]
    h = _rms(x, g_ref[...]) * (1.0 + scale) + shift
    proj = jnp.dot(h.astype(BF16), win_ref[...], preferred_element_type=F32)
    u = proj[:, 0:pw]
    cos, sin_signed = cos_ref[...], sin_ref[...]
    q_ref[...] = _rope(proj[:, pw:pw + aw], cos, sin_signed)
    k_ref[...] = _rope(proj[:, pw + aw:pw + 2 * aw], cos, sin_signed)
    v_ref[...] = proj[:, pw + 2 * aw:pw + 3 * aw]

    pooled = []
    for g, w in enumerate(POOL_WINDOWS):
        cols = slice(g * gc, (g + 1) * gc)
        win = u[:, cols]
        for r in range(POOL_STATE - (w - 1), POOL_STATE):
            win = win + st_ref[r, :, cols]
        pooled.append(win / float(min(w, pos0 + 1)) - u[:, cols])
    pool_ref[...] = _pool_project(jnp.concatenate(pooled, axis=1), poolw_ref, pscale_ref[...]).astype(BF16)

    for r in range(POOL_STATE - 1):
        stn_ref[r] = st_ref[r + 1]
    stn_ref[POOL_STATE - 1] = u


def _pre_sample_call(x, mod, g_pre, w_in_b, pool_w_b, pool_scale, cos_t, sin_t, state_pool, *, pw, aw, pos0):
    rows, d = x.shape
    kern = functools.partial(_pre_sample_kernel, d_model=d, pw=pw, aw=aw, pos0=pos0)
    return pl.pallas_call(
        kern,
        out_shape=[jax.ShapeDtypeStruct((POOL_STATE, rows, pw), F32),
                   jax.ShapeDtypeStruct((rows, aw), F32),
                   jax.ShapeDtypeStruct((rows, aw), F32),
                   jax.ShapeDtypeStruct((rows, aw), F32),
                   jax.ShapeDtypeStruct((rows, pw), BF16)],
        compiler_params=pltpu.CompilerParams(vmem_limit_bytes=VMEM_LIMIT_BYTES),
        name="pre_sample",
    )(x, mod, g_pre, w_in_b, pool_w_b, pool_scale, cos_t, sin_t, state_pool)


def _own_lanes(n_heads):
    aw = n_heads * HEAD_DIM
    head_row = lax.broadcasted_iota(jnp.int32, (n_heads, aw), 0)
    head_of_lane = lax.broadcasted_iota(jnp.int32, (n_heads, aw), 1) // HEAD_DIM
    return head_row == head_of_lane


def _sample_weights(q, k_new, k_refs, page_size):
    n_pages = len(k_refs)
    n_heads = k_refs[0].shape[0]
    aw = n_heads * HEAD_DIM
    ppb = MOBA_BLOCK // page_size
    n_blocks = n_pages // ppb

    qbd = jnp.where(_own_lanes(n_heads), q * (HEAD_DIM ** -0.5 * LOG2E), 0.0)
    qbd_b = qbd.astype(BF16)

    scores = []
    for n in range(n_blocks):
        kt = jnp.concatenate([k_refs[p][...].reshape(aw, page_size).astype(BF16)
                              for p in range(n * ppb, (n + 1) * ppb)], axis=1)
        s = jnp.dot(qbd_b, kt, preferred_element_type=F32)
        scores += [s[:, r * page_size:(r + 1) * page_size] for r in range(ppb)]

    pages_of = lambda n: range(n * ppb, (n + 1) * ppb)
    gates = [sum(jnp.sum(scores[p], axis=-1, keepdims=True) for p in pages_of(n)) for n in range(n_blocks)]
    peaks = [functools.reduce(jnp.maximum, [jnp.max(scores[p], axis=-1, keepdims=True) for p in pages_of(n)])
             for n in range(n_blocks)]
    chosen = []
    for n in range(n_blocks):
        beaten = jnp.zeros((n_heads, 1), jnp.int32)
        for m in range(n_blocks):
            if m != n:
                wins = (gates[m] >= gates[n]) if m < n else (gates[m] > gates[n])
                beaten = beaten + wins.astype(jnp.int32)
        chosen.append(beaten < MOBA_TOPK)

    s_new = jnp.sum(qbd * k_new, axis=-1, keepdims=True)
    m = s_new
    for n in range(n_blocks):
        m = jnp.maximum(m, jnp.where(chosen[n], peaks[n], -jnp.inf))
    p_new = jnp.exp2(s_new - m)
    weights = [jnp.exp2(jnp.where(chosen[p // ppb], scores[p], -jnp.inf) - m) for p in range(n_pages)]
    inv_l = 1.0 / (p_new + sum(jnp.sum(w, axis=-1, keepdims=True) for w in weights))
    return weights, p_new, inv_l


def _sample_values(weights, p_new, inv_l, v_new, v_refs, page_size):
    n_pages = len(v_refs)
    n_heads = v_refs[0].shape[0]
    partial = []
    for h in range(n_heads):
        acc = None
        for p in range(n_pages):
            term = v_refs[p][h] * weights[p][h:h + 1, :]
            acc = term if acc is None else acc + term
        partial.append(acc)
    partial = jnp.concatenate(partial, axis=0)
    ones = jnp.ones((SUBLANES, page_size), BF16)
    high = partial.astype(BF16)
    low = (partial - high.astype(F32)).astype(BF16)
    contract_lanes = (((1,), (1,)), ((), ()))
    past = (lax.dot_general(ones, high, contract_lanes, preferred_element_type=F32)
            + lax.dot_general(ones, low, contract_lanes, preferred_element_type=F32))[0:1, :]
    own_lanes = _own_lanes(n_heads)
    inv_row = jnp.sum(jnp.where(own_lanes, inv_l, 0.0), axis=0, keepdims=True)
    own_row = jnp.sum(jnp.where(own_lanes, p_new, 0.0), axis=0, keepdims=True)
    return (past + own_row * v_new) * inv_row


def _rope_tables(positions):
    half = HEAD_DIM // 2
    lane = jnp.arange(LANES, dtype=jnp.int32)
    inv = ROPE_THETA ** (-(lane % half).astype(F32) / half)
    sign = jnp.where((lane % HEAD_DIM) < half, -1.0, 1.0).astype(F32)
    ang = positions.astype(F32)[:, None] * inv[None, :]
    return jnp.cos(ang), jnp.sin(ang) * sign[None, :]


def kernel(x_prompt, x_sample, c_prompt, c_sample, cache_k, cache_v, page_table, state_pool, state_conv, w_ada, b_ada, g_mix_pre, g_mix_post, w_in, pool_w, pool_scale, w_out, g_ffn_pre, g_ffn_post, w_up, conv_w, conv_b, w_down):
    depth = w_ada.shape[0]
    b, t, d = x_prompt.shape
    n_seq, dec_t, _ = x_sample.shape
    n_pages = page_table.shape[1]
    page_size, n_heads = cache_k.shape[2], cache_k.shape[3]
    past_len = n_pages * page_size
    aw = n_heads * HEAD_DIM
    pw = state_pool.shape[3]
    d_ff = w_down.shape[1]
    assert dec_t == 1 and t % MOBA_BLOCK == 0 and t % PRE_TILE == 0 and pw == d - aw
    assert (pw // len(POOL_WINDOWS)) % LANES == 0 and w_in.shape[2] == pw + 3 * aw

    tm = min(PRE_TILE, t)
    cos_r, sin_r = _rope_tables(jnp.arange(tm, dtype=jnp.int32))
    cos_t, sin_t = _rope_tables(tm * jnp.arange(t // tm, dtype=jnp.int32))
    rope_p = (cos_r, sin_r, cos_t[:, None, :], sin_t[:, None, :])
    cos_s, sin_s = _rope_tables(past_len + jnp.arange(dec_t, dtype=jnp.int32))

    y_p, y_s = x_prompt, x_sample.reshape(1, n_seq, d)
    outs = [[] for _ in range(8)]
    for l in range(depth):
        w = dict(w_out_p=w_out[l, :pw].astype(BF16), w_out_a=w_out[l, pw:].astype(BF16),
                 g_mix_post=g_mix_post[l].reshape(1, d), g_ffn_pre=g_ffn_pre[l].reshape(1, d),
                 g_ffn_post=g_ffn_post[l].reshape(1, d), w_up=w_up[l].astype(BF16), conv_w=conv_w[l],
                 conv_b=conv_b[l].reshape(1, 2 * d_ff), w_down=w_down[l].astype(BF16))
        g_pre = g_mix_pre[l].reshape(1, d)
        w_in_b = w_in[l].astype(BF16)
        pool_w_b = pool_w[l].astype(BF16)
        pscale = pool_scale[l].reshape(1, pw)

        mod_p, mod_s = _mod_call(c_prompt, c_sample, w_ada[l], b_ada[l])
        mod_p, mod_s = mod_p.reshape(b, 1, 6 * d), mod_s.reshape(1, n_seq, 6 * d)

        k_p, v_p, qt, kb, vt4, pool_p, utail = _pre_prompt_call(
            y_p, mod_p, g_pre, w_in_b, pool_w_b, pscale, rope_p, pw=pw, aw=aw)
        pool_state_s, q_s, k_s, v_s, pool_s = _pre_sample_call(
            y_s[0], mod_s[0], g_pre, w_in_b, pool_w_b, pscale, cos_s, sin_s, jnp.swapaxes(state_pool[l], 0, 1),
            pw=pw, aw=aw, pos0=past_len)
        tok = lambda a: a.reshape(n_seq, 1, aw)
        att_t, att_s = _moba_call(qt, kb.reshape(b, t // MOBA_BLOCK, MOBA_BLOCK, aw), vt4,
                                  page_table, tok(q_s), tok(k_s), tok(v_s),
                                  jnp.transpose(cache_k[l], (0, 2, 3, 1)), jnp.transpose(cache_v[l], (0, 2, 3, 1)))
        y_p, ctail = _post_call(y_p, pool_p, att_t, mod_p, w, att_transposed=True)
        y_s, conv_state_s = _post_call(y_s, pool_s.reshape(1, n_seq, pw), att_s.reshape(1, n_seq, aw).astype(BF16),
                                       mod_s, w, att_transposed=False, state=state_conv[l][None])

        outs[0].append(k_p.reshape(b, t, n_heads, HEAD_DIM))
        outs[1].append(v_p.reshape(b, t, n_heads, HEAD_DIM))
        outs[2].append(utail[:, POOL_HALO - POOL_STATE:])
        outs[3].append(ctail[:, SUBLANES - (CONV_WIDTH - 1):])
        outs[4].append(k_s.reshape(n_seq, dec_t, n_heads, HEAD_DIM))
        outs[5].append(v_s.reshape(n_seq, dec_t, n_heads, HEAD_DIM))
        outs[6].append(jnp.swapaxes(pool_state_s, 0, 1))
        outs[7].append(conv_state_s[0])
    stacked = [jnp.stack(o) for o in outs]
    return (y_p, y_s.reshape(n_seq, dec_t, d), *stacked)
```

```python
import functools
import math

import jax
import jax.numpy as jnp
from jax import lax
from jax.experimental import pallas as pl
from jax.experimental.pallas import tpu as pltpu

HEAD_DIM = 64
MOBA_BLOCK = 256
MOBA_TOPK = 3
POOL_WINDOWS = (2, 4, 8, 16)
POOL_STATE = max(POOL_WINDOWS) - 1
CONV_WIDTH = 3
ROPE_THETA = 10000.0
RMS_EPS = 1e-6

LANES = 128
SUBLANES = 8
VMEM_LIMIT_BYTES = 56 * 1024 * 1024

BF16_ROWS = 16
V_ROWS = HEAD_DIM + BF16_ROWS

POOL_HALO = 16
PRE_TILE = 512
POST_TILE = 512
FFN_CHUNKS = 11
FFN_LOOKAHEAD = 2
FFN_DOWN_SPLITS = 2
HEAD_GROUPS = 2
PAST_GROUP = 8
SOFTMAX_BLOCKS = 1
SCORE_LOOKAHEAD = 6

F32 = jnp.float32
BF16 = jnp.bfloat16
LOG2E = math.log2(math.e)
MASKED = -1e30


def _params(*semantics):
    return pltpu.CompilerParams(dimension_semantics=semantics, vmem_limit_bytes=VMEM_LIMIT_BYTES)


def _resident(shape, index_map):
    return pl.BlockSpec(shape, index_map, pipeline_mode=pl.Buffered(1))


def _rms(x, g):
    ms = jnp.mean(x * x, axis=-1, keepdims=True)
    return x * lax.rsqrt(ms + RMS_EPS) * g


def _rope(x, cos, sin_signed):
    lane = lax.broadcasted_iota(jnp.int32, (1, LANES), 1)
    first_half = (lane % HEAD_DIM) < HEAD_DIM // 2
    half = HEAD_DIM // 2
    outs = []
    for c in range(x.shape[1] // LANES):
        blk = x[:, c * LANES:(c + 1) * LANES]
        partner = jnp.where(first_half, pltpu.roll(blk, LANES - half, 1), pltpu.roll(blk, half, 1))
        outs.append(blk * cos + partner * sin_signed)
    return jnp.concatenate(outs, axis=1)


def _pool_project(pooled, poolw_ref, pscale):
    n_groups, gc, _ = poolw_ref.shape
    outs = [jnp.dot(pooled[:, g * gc:(g + 1) * gc].astype(BF16), poolw_ref[g], preferred_element_type=F32)
            for g in range(n_groups)]
    return jnp.concatenate(outs, axis=1) * pscale


def _mod_kernel(cp_ref, cs_ref, w_ref, b_ref, op_ref, os_ref):
    w = w_ref[...].astype(BF16)
    for c_ref, o_ref in ((cp_ref, op_ref), (cs_ref, os_ref)):
        c = c_ref[...]
        a = (c * jax.nn.sigmoid(c)).astype(BF16)
        o_ref[...] = jnp.dot(a, w, preferred_element_type=F32) + b_ref[...]


def _mod_call(c_prompt, c_sample, w_ada, b_ada):
    d, n_out = w_ada.shape
    rows_p, rows_s = c_prompt.shape[0], c_sample.shape[0]
    full = lambda j: (0, 0)
    col = lambda j: (0, j)
    return pl.pallas_call(
        _mod_kernel,
        grid=(n_out // d,),
        in_specs=[pl.BlockSpec((rows_p, d), full), pl.BlockSpec((rows_s, d), full),
                  pl.BlockSpec((d, d), col), pl.BlockSpec((1, d), col)],
        out_specs=[pl.BlockSpec((rows_p, d), col), pl.BlockSpec((rows_s, d), col)],
        out_shape=[jax.ShapeDtypeStruct((rows_p, n_out), F32), jax.ShapeDtypeStruct((rows_s, n_out), F32)],
        compiler_params=_params("arbitrary"),
        name="adaln_mod",
    )(c_prompt, c_sample, w_ada, b_ada.reshape(1, n_out))


def _pre_prompt_kernel(x_ref, mod_ref, g_ref, win_ref, poolw_ref, pscale_ref, cos_ref, sin_ref, cost_ref, sint_ref,
                       k_ref, v_ref, qt_ref, kb_ref, vt_ref, pool_ref, utail_ref,
                       ubuf, s2buf, s4buf, s8buf, *, tm, d_model, pw, aw, q_scale):
    i = pl.program_id(1)
    halo = POOL_HALO
    gc = pw // len(POOL_WINDOWS)
    bufs = (ubuf, s2buf, s4buf, s8buf)

    @pl.when(i == 0)
    def _():
        for buf in bufs:
            buf[0:halo, :] = jnp.zeros((halo, buf.shape[1]), F32)

    x = x_ref[0]
    mod = mod_ref[0]
    shift, scale = mod[:, 0:d_model], mod[:, d_model:2 * d_model]
    h = _rms(x, g_ref[...]) * (1.0 + scale) + shift
    proj = jnp.dot(h.astype(BF16), win_ref[...], preferred_element_type=F32)
    u = proj[:, 0:pw]
    cos_t, sin_t, cos_r, sin_r = cost_ref[0], sint_ref[0], cos_ref[...], sin_ref[...]
    cos = cos_t * cos_r - sin_t * sin_r
    sin_signed = sin_t * cos_r + cos_t * sin_r
    q = _rope(proj[:, pw:pw + aw], cos, sin_signed)
    k = _rope(proj[:, pw + aw:pw + 2 * aw], cos, sin_signed)
    v = proj[:, pw + 2 * aw:pw + 3 * aw]

    k_ref[0] = k
    v_ref[0] = v
    kb_ref[0] = k.astype(BF16)
    qt_ref[0] = (q * q_scale).T.astype(BF16)
    ones_pad = jnp.where(lax.broadcasted_iota(jnp.int32, (BF16_ROWS, MOBA_BLOCK), 0) == 0, 1.0, 0.0).astype(BF16)
    for r in range(tm // MOBA_BLOCK):
        vt = v[r * MOBA_BLOCK:(r + 1) * MOBA_BLOCK, :].T.astype(BF16)
        for hd in range(aw // HEAD_DIM):
            vt_ref[0, r, hd * V_ROWS:hd * V_ROWS + HEAD_DIM, :] = vt[hd * HEAD_DIM:(hd + 1) * HEAD_DIM, :]
            vt_ref[0, r, hd * V_ROWS + HEAD_DIM:(hd + 1) * V_ROWS, :] = ones_pad

    ubuf[halo:halo + tm, :] = u
    s2 = u + ubuf[halo - 1:halo - 1 + tm, :]
    s2buf[halo:halo + tm, :] = s2[:, gc:]
    s4 = s2[:, gc:] + s2buf[halo - 2:halo - 2 + tm, :]
    s4buf[halo:halo + tm, :] = s4[:, gc:]
    s8 = s4[:, gc:] + s4buf[halo - 4:halo - 4 + tm, :]
    s8buf[halo:halo + tm, :] = s8[:, gc:]
    s16 = s8[:, gc:] + s8buf[halo - 8:halo - 8 + tm, :]
    wins = (s2[:, 0:gc], s4[:, 0:gc], s8[:, 0:gc], s16)
    pos1 = (i * tm + 1 + lax.broadcasted_iota(jnp.int32, (tm, 1), 0)).astype(F32)
    pooled = jnp.concatenate(
        [wins[g] / jnp.minimum(float(w), pos1) - u[:, g * gc:(g + 1) * gc] for g, w in enumerate(POOL_WINDOWS)],
        axis=1)
    pool_ref[0] = _pool_project(pooled, poolw_ref, pscale_ref[...]).astype(BF16)

    for buf in bufs:
        buf[0:halo, :] = buf[tm:tm + halo, :]
    utail_ref[0] = ubuf[0:halo, :]


def _pre_prompt_call(x, mod, g_pre, w_in_b, pool_w_b, pool_scale, rope, *, pw, aw):
    b, t, d = x.shape
    tm = min(PRE_TILE, t)
    nb = t // MOBA_BLOCK
    gc = pw // len(POOL_WINDOWS)
    in_w = w_in_b.shape[1]
    vt_rows = (aw // HEAD_DIM) * V_ROWS
    kern = functools.partial(_pre_prompt_kernel, tm=tm, d_model=d, pw=pw, aw=aw,
                             q_scale=HEAD_DIM ** -0.5 * LOG2E)
    row_tile = lambda bi, i: (bi, i, 0)
    const2 = lambda bi, i: (0, 0)
    return pl.pallas_call(
        kern,
        grid=(b, t // tm),
        in_specs=[pl.BlockSpec((1, tm, d), row_tile),
                  pl.BlockSpec((1, 1, mod.shape[2]), lambda bi, i: (bi, 0, 0)),
                  _resident((1, d), const2),
                  _resident((d, in_w), const2),
                  _resident(pool_w_b.shape, lambda bi, i: (0, 0, 0)),
                  _resident((1, pw), const2),
                  _resident((tm, LANES), const2),
                  _resident((tm, LANES), const2),
                  pl.BlockSpec((1, 1, LANES), lambda bi, i: (i, 0, 0)),
                  pl.BlockSpec((1, 1, LANES), lambda bi, i: (i, 0, 0))],
        out_specs=[pl.BlockSpec((1, tm, aw), row_tile),
                   pl.BlockSpec((1, tm, aw), row_tile),
                   pl.BlockSpec((1, aw, tm), lambda bi, i: (bi, 0, i)),
                   pl.BlockSpec((1, tm, aw), row_tile),
                   pl.BlockSpec((1, tm // MOBA_BLOCK, vt_rows, MOBA_BLOCK), lambda bi, i: (bi, i, 0, 0)),
                   pl.BlockSpec((1, tm, pw), row_tile),
                   pl.BlockSpec((1, POOL_HALO, pw), lambda bi, i: (bi, 0, 0))],
        out_shape=[jax.ShapeDtypeStruct((b, t, aw), F32),
                   jax.ShapeDtypeStruct((b, t, aw), F32),
                   jax.ShapeDtypeStruct((b, aw, t), BF16),
                   jax.ShapeDtypeStruct((b, t, aw), BF16),
                   jax.ShapeDtypeStruct((b, nb, vt_rows, MOBA_BLOCK), BF16),
                   jax.ShapeDtypeStruct((b, t, pw), BF16),
                   jax.ShapeDtypeStruct((b, POOL_HALO, pw), F32)],
        scratch_shapes=[pltpu.VMEM((tm + POOL_HALO, pw), F32),
                        pltpu.VMEM((tm + POOL_HALO, pw - gc), F32),
                        pltpu.VMEM((tm + POOL_HALO, pw - 2 * gc), F32),
                        pltpu.VMEM((tm + POOL_HALO, pw - 3 * gc), F32)],
        compiler_params=_params("arbitrary", "arbitrary"),
        name="pre_prompt",
    )(x, mod, g_pre, w_in_b, pool_w_b, pool_scale, *rope)


def _moba_kernel(pt_ref, qt_ref, kb_ref, vt_ref, qs_ref, kn_ref, vn_ref, *refs, nb, n_heads, n_pages, page_size):
    del pt_ref
    k_pages, v_pages = refs[:n_pages], refs[n_pages:2 * n_pages]
    o_ref, os_ref, km_ref, ind_ref, q2_ref, m_ref, acc_ref = refs[2 * n_pages:]

    i = pl.program_id(2)
    tq = MOBA_BLOCK
    pair_w = 2 * HEAD_DIM

    @pl.when(i == 0)
    def _():
        lane = lax.broadcasted_iota(jnp.int32, (MOBA_BLOCK, LANES), 1)
        for n in range(nb):
            km_ref[n:n + 1, :] = jnp.mean(kb_ref[0, n].astype(F32), axis=0, keepdims=True)
            ind_ref[n] = jnp.where(lane == n, 1.0, 0.0).astype(BF16)
        for h in range(n_heads):
            q2_ref[h, pair_w + nb:, :] = jnp.zeros((pair_w - nb, tq), BF16)

    sample_weights = _sample_weights(qs_ref[0], kn_ref[0], k_pages, page_size)

    blk_row = lax.broadcasted_iota(jnp.int32, (nb, tq), 0)
    blk_row_f = blk_row.astype(F32)
    fully_past = blk_row < i
    key_row = lax.broadcasted_iota(jnp.int32, (MOBA_BLOCK, tq), 0)
    qry_col = lax.broadcasted_iota(jnp.int32, (MOBA_BLOCK, tq), 1)
    causal = key_row <= qry_col
    pair_row = lax.broadcasted_iota(jnp.int32, (pair_w, tq), 0)

    for h in range(n_heads):
        lo = (h // 2) * pair_w
        mine = (pair_row >= HEAD_DIM) if h % 2 else (pair_row < HEAD_DIM)
        q2 = jnp.where(mine, qt_ref[0, lo:lo + pair_w, :], jnp.zeros((), BF16))
        q2_ref[h, 0:pair_w, :] = q2

        gate = jnp.dot(km_ref[:, lo:lo + pair_w].astype(BF16), q2, preferred_element_type=F32)
        gate = jnp.where(fully_past, gate, -jnp.inf)
        chosen = jnp.zeros((nb, tq), jnp.bool_)
        for _ in range(MOBA_TOPK):
            best = jnp.max(gate, axis=0, keepdims=True)
            first = jnp.min(jnp.where(gate == best, blk_row_f, float(nb)), axis=0, keepdims=True)
            pick = blk_row_f == first
            chosen = jnp.logical_or(chosen, pick)
            gate = jnp.where(pick, -jnp.inf, gate)
        q2_ref[h, pair_w:pair_w + nb, :] = jnp.where(jnp.logical_and(chosen, fully_past), 0.0, MASKED).astype(BF16)

    def scores(js, h):
        lo = (h // 2) * pair_w
        return [jnp.dot(jnp.concatenate([kb_ref[0, j, :, lo:lo + pair_w], ind_ref[j]], axis=1), q2_ref[h],
                        preferred_element_type=F32) for j in js]

    def own_scores(js, h):
        lo = (h // 2) * pair_w
        return [jnp.dot(kb_ref[0, j, :, lo:lo + pair_w], q2_ref[h, 0:pair_w, :], preferred_element_type=F32)
                for j in js]

    def all_heads(js, consume, scores=scores):
        items = [(js[r:r + SOFTMAX_BLOCKS], h) for r in range(0, len(js), SOFTMAX_BLOCKS) for h in range(n_heads)]
        look = min(SCORE_LOOKAHEAD, len(items))
        ahead = {k: scores(*items[k]) for k in range(look)}
        for k, (sub, h) in enumerate(items):
            if k + look < len(items):
                ahead[k + look] = scores(*items[k + look])
            consume(sub, h, ahead.pop(k))

    def values(j, h):
        return vt_ref[0, j, h * V_ROWS:(h + 1) * V_ROWS, :]

    def first_block(_, h, ss):
        s = jnp.where(causal, ss[0], -jnp.inf)
        m0 = jnp.max(s, axis=0, keepdims=True)
        m_ref[h] = m0
        acc_ref[h] = jnp.dot(values(i, h), jnp.exp2(s - m0).astype(BF16), preferred_element_type=F32)

    all_heads([i], first_block, own_scores)
    os_ref[0] = _sample_values(*sample_weights, vn_ref[0], v_pages, page_size)

    def online_update(js, h, ss, m_old, acc_old):
        m_new = m_old
        for s in ss:
            m_new = jnp.maximum(m_new, jnp.max(s, axis=0, keepdims=True))
        pv = None
        for j, s in zip(js, ss):
            d = jnp.dot(values(j, h), jnp.exp2(s - m_new).astype(BF16), preferred_element_type=F32)
            pv = d if pv is None else pv + d
        return m_new, jnp.exp2(m_old - m_new) * acc_old + pv

    def past_blocks(js, h, ss):
        m_ref[h], acc_ref[h] = online_update(js, h, ss, m_ref[h], acc_ref[h])

    size, top = 1, i
    while size < PAST_GROUP:
        @pl.when((i & size) != 0)
        def _(size=size, top=top):
            all_heads([top - size + r for r in range(size)], past_blocks)

        top = top - (i & size)
        size *= 2

    def group_body(t, carry):
        all_heads([PAST_GROUP * t + r for r in range(PAST_GROUP)], past_blocks)
        return carry

    lax.fori_loop(0, i // PAST_GROUP, group_body, 0)

    for h in range(n_heads):
        acc = acc_ref[h]
        o_ref[0, h * HEAD_DIM:(h + 1) * HEAD_DIM, :] = (acc[0:HEAD_DIM] / acc[HEAD_DIM:HEAD_DIM + 1]).astype(BF16)


def _moba_call(qt, kb4, vt4, page_table, q_s, k_s, v_s, cache_kt, cache_vt):
    b, aw, t = qt.shape
    nb = t // MOBA_BLOCK
    n_heads = aw // HEAD_DIM
    n_seq, n_pages = page_table.shape
    _, _, dh, page_size = cache_kt.shape
    hg = n_heads // HEAD_GROUPS
    gw = hg * HEAD_DIM
    assert nb <= LANES and hg % 2 == 0
    assert dh == HEAD_DIM and MOBA_BLOCK % page_size == 0 and (n_pages * page_size) % MOBA_BLOCK == 0
    assert n_seq <= b * HEAD_GROUPS * nb
    kern = functools.partial(_moba_kernel, nb=nb, n_heads=hg, n_pages=n_pages, page_size=page_size)

    def seq(bi, g, i):
        return jnp.minimum((bi * HEAD_GROUPS + g) * nb + i, n_seq - 1)

    tok = pl.BlockSpec((1, 1, aw), lambda bi, g, i, pt: (seq(bi, g, i), 0, 0))

    def page_spec(p):
        return pl.BlockSpec((None, n_heads, dh, page_size), lambda bi, g, i, pt: (pt[seq(bi, g, i), p], 0, 0, 0))

    pages = [page_spec(p) for p in range(n_pages)]
    q_tile = pl.BlockSpec((1, gw, MOBA_BLOCK), lambda bi, g, i, pt: (bi, g, i))
    return pl.pallas_call(
        kern,
        grid_spec=pltpu.PrefetchScalarGridSpec(
            num_scalar_prefetch=1,
            grid=(b, HEAD_GROUPS, nb),
            in_specs=[q_tile,
                      pl.BlockSpec((1, nb, MOBA_BLOCK, gw), lambda bi, g, i, pt: (bi, 0, 0, g),
                                   pipeline_mode=pl.Buffered(1)),
                      pl.BlockSpec((1, nb, hg * V_ROWS, MOBA_BLOCK), lambda bi, g, i, pt: (bi, 0, g, 0),
                                   pipeline_mode=pl.Buffered(1)),
                      tok, tok, tok] + pages + pages,
            out_specs=[q_tile, tok],
            scratch_shapes=[pltpu.VMEM((nb, gw), F32),
                            pltpu.VMEM((nb, MOBA_BLOCK, LANES), BF16),
                            pltpu.VMEM((hg, 4 * HEAD_DIM, MOBA_BLOCK), BF16),
                            pltpu.VMEM((hg, 1, MOBA_BLOCK), F32),
                            pltpu.VMEM((hg, V_ROWS, MOBA_BLOCK), F32)]),
        out_shape=[jax.ShapeDtypeStruct((b, aw, t), BF16), jax.ShapeDtypeStruct((n_seq, 1, aw), F32)],
        compiler_params=_params("arbitrary", "arbitrary", "arbitrary"),
        name="moba",
    )(page_table, qt, kb4, vt4, q_s, k_s, v_s, *([cache_kt] * n_pages), *([cache_vt] * n_pages))


def _post_kernel(*refs, tm, d_model, d_ff, att_transposed, per_row_state):
    (x_ref, pool_ref, att_ref, mod_ref, wop_ref, woa_ref, gpost_ref, gpre2_ref, gpost2_ref,
     wup_ref, cw_ref, cb_ref, wdn_ref) = refs[:13]
    if per_row_state:
        st_ref, y_ref, stn_ref, actbuf = refs[13:]
    else:
        y_ref, ctail_ref, carry, upbuf, actbuf = refs[13:]
        i = pl.program_id(1)

        @pl.when(i == 0)
        def _():
            carry[...] = jnp.zeros(carry.shape, F32)

    d = d_model
    x = x_ref[0]
    mod = mod_ref[0]
    gate1 = mod[:, 2 * d:3 * d]
    shift2, scale2, gate2 = mod[:, 3 * d:4 * d], mod[:, 4 * d:5 * d], mod[:, 5 * d:6 * d]

    mix = jnp.dot(pool_ref[0], wop_ref[...], preferred_element_type=F32)
    if att_transposed:
        mix += lax.dot_general(att_ref[0], woa_ref[...], (((0,), (0,)), ((), ())), preferred_element_type=F32)
    else:
        mix += jnp.dot(att_ref[0], woa_ref[...], preferred_element_type=F32)
    x1 = x + gate1 * _rms(mix, gpost_ref[...])
    h2 = (_rms(x1, gpre2_ref[...]) * (1.0 + scale2) + shift2).astype(BF16)

    fc = d_ff // FFN_CHUNKS

    def up_proj(c):
        out = []
        for base in (0, d_ff):
            cols = slice(base + c * fc, base + (c + 1) * fc)
            out.append((cols, jnp.dot(h2, wup_ref[:, cols], preferred_element_type=F32)))
        return out

    def conv_act(c, ups):
        halves = []
        for half, (cols, up) in enumerate(ups):
            if per_row_state:
                prev2, prev1 = st_ref[0, :, 0, cols], st_ref[0, :, 1, cols]
                stn_ref[0, :, 0, cols] = prev1
                stn_ref[0, :, 1, cols] = up
            else:
                buf = upbuf.at[(2 * c + half) % upbuf.shape[0]]
                buf[0:SUBLANES, :] = carry[:, cols]
                buf[SUBLANES:SUBLANES + tm, :] = up
                prev1 = buf[SUBLANES - 1:SUBLANES - 1 + tm, :]
                prev2 = buf[SUBLANES - 2:SUBLANES - 2 + tm, :]
                carry[:, cols] = buf[tm:tm + SUBLANES, :]
            halves.append(cb_ref[:, cols] + cw_ref[0:1, cols] * prev2 + cw_ref[1:2, cols] * prev1
                          + cw_ref[2:3, cols] * up)
        a, g = halves
        actbuf[:, c * fc:(c + 1) * fc] = (a * jax.nn.sigmoid(a) * g).astype(BF16)

    split_at = [((s + 1) * FFN_CHUNKS) // FFN_DOWN_SPLITS for s in range(FFN_DOWN_SPLITS)]
    ahead = {c: up_proj(c) for c in range(min(FFN_LOOKAHEAD, FFN_CHUNKS))}
    ffn, done = None, 0
    for c in range(FFN_CHUNKS):
        if c + FFN_LOOKAHEAD < FFN_CHUNKS:
            ahead[c + FFN_LOOKAHEAD] = up_proj(c + FFN_LOOKAHEAD)
        conv_act(c, ahead.pop(c))
        if c + 1 in split_at:
            rows = slice(done * fc, (c + 1) * fc)
            part = jnp.dot(actbuf[:, rows], wdn_ref[rows, :], preferred_element_type=F32)
            ffn = part if ffn is None else ffn + part
            done = c + 1
    y_ref[0] = x1 + gate2 * _rms(ffn, gpost2_ref[...])
    if not per_row_state:
        ctail_ref[0] = carry[...]


def _post_call(x, pool_o, att, mod, w, *, att_transposed, state=None):
    b, t, d = x.shape
    pw = pool_o.shape[2]
    aw = w["w_out_a"].shape[0]
    d_ff = w["w_down"].shape[0]
    per_row = state is not None
    tm = t if per_row else min(POST_TILE, t)
    assert t % tm == 0 and d_ff % (FFN_CHUNKS * LANES) == 0 and CONV_WIDTH == 3
    fc = d_ff // FFN_CHUNKS
    kern = functools.partial(_post_kernel, tm=tm, d_model=d, d_ff=d_ff, att_transposed=att_transposed,
                             per_row_state=per_row)
    row_tile = lambda bi, i: (bi, i, 0)
    const2 = lambda bi, i: (0, 0)
    mod_rows = mod.shape[1]
    in_specs = [pl.BlockSpec((1, tm, d), row_tile),
                pl.BlockSpec((1, tm, pw), row_tile),
                (pl.BlockSpec((1, aw, tm), lambda bi, i: (bi, 0, i)) if att_transposed
                 else pl.BlockSpec((1, tm, aw), row_tile)),
                pl.BlockSpec((1, mod_rows, mod.shape[2]), (row_tile if mod_rows > 1 else lambda bi, i: (bi, 0, 0))),
                _resident((pw, d), const2),
                _resident((aw, d), const2),
                _resident((1, d), const2),
                _resident((1, d), const2),
                _resident((1, d), const2),
                _resident((d, 2 * d_ff), const2),
                _resident((CONV_WIDTH, 2 * d_ff), const2),
                _resident((1, 2 * d_ff), const2),
                _resident((d_ff, d), const2)]
    args = [x, pool_o, att, mod, w["w_out_p"], w["w_out_a"], w["g_mix_post"], w["g_ffn_pre"], w["g_ffn_post"],
            w["w_up"], w["conv_w"], w["conv_b"], w["w_down"]]
    if per_row:
        state_block = pl.BlockSpec((1, tm, CONV_WIDTH - 1, 2 * d_ff), lambda bi, i: (bi, i, 0, 0))
        in_specs.append(state_block)
        args.append(state)
        out_specs = [pl.BlockSpec((1, tm, d), row_tile), state_block]
        out_shape = [jax.ShapeDtypeStruct((b, t, d), F32), jax.ShapeDtypeStruct(state.shape, F32)]
        scratch = []
    else:
        out_specs = [pl.BlockSpec((1, tm, d), row_tile),
                     pl.BlockSpec((1, SUBLANES, 2 * d_ff), lambda bi, i: (bi, 0, 0))]
        out_shape = [jax.ShapeDtypeStruct((b, t, d), F32), jax.ShapeDtypeStruct((b, SUBLANES, 2 * d_ff), F32)]
        scratch = [pltpu.VMEM((SUBLANES, 2 * d_ff), F32),
                   pltpu.VMEM((4, tm + SUBLANES, fc), F32)]
    scratch.append(pltpu.VMEM((tm, d_ff), BF16))
    return pl.pallas_call(
        kern,
        grid=(b, t // tm),
        in_specs=in_specs,
        out_specs=out_specs,
        out_shape=out_shape,
        scratch_shapes=scratch,
        compiler_params=_params("arbitrary", "arbitrary"),
        name="post_sample" if per_row else "post_prompt",
    )(*args)


def _pre_sample_kernel(x_ref, mod_ref, g_ref, win_ref, poolw_ref, pscale_ref, cos_ref, sin_ref, st_ref,
                       stn_ref, q_ref, k_ref, v_ref, pool_ref, *, d_model, pw, aw, pos0):
    gc = pw // len(POOL_WINDOWS)
    x = x_ref[...]
    mod = mod_ref[...]
    shift, scale = mod[:, 0:d_model], mod[:, d_model:2 * d_model]
    h = _rms(x, g_ref[...]) * (1.0 + scale) + shift
    proj = jnp.dot(h.astype(BF16), win_ref[...], preferred_element_type=F32)
    u = proj[:, 0:pw]
    cos, sin_signed = cos_ref[...], sin_ref[...]
    q_ref[...] = _rope(proj[:, pw:pw + aw], cos, sin_signed)
    k_ref[...] = _rope(proj[:, pw + aw:pw + 2 * aw], cos, sin_signed)
    v_ref[...] = proj[:, pw + 2 * aw:pw + 3 * aw]

    pooled = []
    for g, w in enumerate(POOL_WINDOWS):
        cols = slice(g * gc, (g + 1) * gc)
        win = u[:, cols]
        for r in range(POOL_STATE - (w - 1), POOL_STATE):
            win = win + st_ref[r, :, cols]
        pooled.append(win / float(min(w, pos0 + 1)) - u[:, cols])
    pool_ref[...] = _pool_project(jnp.concatenate(pooled, axis=1), poolw_ref, pscale_ref[...]).astype(BF16)

    for r in range(POOL_STATE - 1):
        stn_ref[r] = st_ref[r + 1]
    stn_ref[POOL_STATE - 1] = u


def _pre_sample_call(x, mod, g_pre, w_in_b, pool_w_b, pool_scale, cos_t, sin_t, state_pool, *, pw, aw, pos0):
    rows, d = x.shape
    kern = functools.partial(_pre_sample_kernel, d_model=d, pw=pw, aw=aw, pos0=pos0)
    return pl.pallas_call(
        kern,
        out_shape=[jax.ShapeDtypeStruct((POOL_STATE, rows, pw), F32),
                   jax.ShapeDtypeStruct((rows, aw), F32),
                   jax.ShapeDtypeStruct((rows, aw), F32),
                   jax.ShapeDtypeStruct((rows, aw), F32),
                   jax.ShapeDtypeStruct((rows, pw), BF16)],
        compiler_params=pltpu.CompilerParams(vmem_limit_bytes=VMEM_LIMIT_BYTES),
        name="pre_sample",
    )(x, mod, g_pre, w_in_b, pool_w_b, pool_scale, cos_t, sin_t, state_pool)


def _own_lanes(n_heads):
    aw = n_heads * HEAD_DIM
    head_row = lax.broadcasted_iota(jnp.int32, (n_heads, aw), 0)
    head_of_lane = lax.broadcasted_iota(jnp.int32, (n_heads, aw), 1) // HEAD_DIM
    return head_row == head_of_lane


def _sample_weights(q, k_new, k_refs, page_size):
    n_pages = len(k_refs)
    n_heads = k_refs[0].shape[0]
    aw = n_heads * HEAD_DIM
    ppb = MOBA_BLOCK // page_size
    n_blocks = n_pages // ppb

    qbd = jnp.where(_own_lanes(n_heads), q * (HEAD_DIM ** -0.5 * LOG2E), 0.0)
    qbd_b = qbd.astype(BF16)

    scores = []
    for n in range(n_blocks):
        kt = jnp.concatenate([k_refs[p][...].reshape(aw, page_size).astype(BF16)
                              for p in range(n * ppb, (n + 1) * ppb)], axis=1)
        s = jnp.dot(qbd_b, kt, preferred_element_type=F32)
        scores += [s[:, r * page_size:(r + 1) * page_size] for r in range(ppb)]

    pages_of = lambda n: range(n * ppb, (n + 1) * ppb)
    gates = [sum(jnp.sum(scores[p], axis=-1, keepdims=True) for p in pages_of(n)) for n in range(n_blocks)]
    peaks = [functools.reduce(jnp.maximum, [jnp.max(scores[p], axis=-1, keepdims=True) for p in pages_of(n)])
             for n in range(n_blocks)]
    chosen = []
    for n in range(n_blocks):
        beaten = jnp.zeros((n_heads, 1), jnp.int32)
        for m in range(n_blocks):
            if m != n:
                wins = (gates[m] >= gates[n]) if m < n else (gates[m] > gates[n])
                beaten = beaten + wins.astype(jnp.int32)
        chosen.append(beaten < MOBA_TOPK)

    s_new = jnp.sum(qbd * k_new, axis=-1, keepdims=True)
    m = s_new
    for n in range(n_blocks):
        m = jnp.maximum(m, jnp.where(chosen[n], peaks[n], -jnp.inf))
    p_new = jnp.exp2(s_new - m)
    weights = [jnp.exp2(jnp.where(chosen[p // ppb], scores[p], -jnp.inf) - m) for p in range(n_pages)]
    inv_l = 1.0 / (p_new + sum(jnp.sum(w, axis=-1, keepdims=True) for w in weights))
    return weights, p_new, inv_l


def _sample_values(weights, p_new, inv_l, v_new, v_refs, page_size):
    n_pages = len(v_refs)
    n_heads = v_refs[0].shape[0]
    partial = []
    for h in range(n_heads):
        acc = None
        for p in range(n_pages):
            term = v_refs[p][h] * weights[p][h:h + 1, :]
            acc = term if acc is None else acc + term
        partial.append(acc)
    partial = jnp.concatenate(partial, axis=0)
    ones = jnp.ones((SUBLANES, page_size), BF16)
    high = partial.astype(BF16)
    low = (partial - high.astype(F32)).astype(BF16)
    contract_lanes = (((1,), (1,)), ((), ()))
    past = (lax.dot_general(ones, high, contract_lanes, preferred_element_type=F32)
            + lax.dot_general(ones, low, contract_lanes, preferred_element_type=F32))[0:1, :]
    own_lanes = _own_lanes(n_heads)
    inv_row = jnp.sum(jnp.where(own_lanes, inv_l, 0.0), axis=0, keepdims=True)
    own_row = jnp.sum(jnp.where(own_lanes, p_new, 0.0), axis=0, keepdims=True)
    return (past + own_row * v_new) * inv_row


def _rope_tables(positions):
    half = HEAD_DIM // 2
    lane = jnp.arange(LANES, dtype=jnp.int32)
    inv = ROPE_THETA ** (-(lane % half).astype(F32) / half)
    sign = jnp.where((lane % HEAD_DIM) < half, -1.0, 1.0).astype(F32)
    ang = positions.astype(F32)[:, None] * inv[None, :]
    return jnp.cos(ang), jnp.sin(ang) * sign[None, :]


def kernel(x_prompt, x_sample, c_prompt, c_sample, cache_k, cache_v, page_table, state_pool, state_conv, w_ada, b_ada, g_mix_pre, g_mix_post, w_in, pool_w, pool_scale, w_out, g_ffn_pre, g_ffn_post, w_up, conv_w, conv_b, w_down):
    depth = w_ada.shape[0]
    b, t, d = x_prompt.shape
    n_seq, dec_t, _ = x_sample.shape
    n_pages = page_table.shape[1]
    page_size, n_heads = cache_k.shape[2], cache_k.shape[3]
    past_len = n_pages * page_size
    aw = n_heads * HEAD_DIM
    pw = state_pool.shape[3]
    d_ff = w_down.shape[1]
    assert dec_t == 1 and t % MOBA_BLOCK == 0 and t % PRE_TILE == 0 and pw == d - aw
    assert (pw // len(POOL_WINDOWS)) % LANES == 0 and w_in.shape[2] == pw + 3 * aw

    tm = min(PRE_TILE, t)
    cos_r, sin_r = _rope_tables(jnp.arange(tm, dtype=jnp.int32))
    cos_t, sin_t = _rope_tables(tm * jnp.arange(t // tm, dtype=jnp.int32))
    rope_p = (cos_r, sin_r, cos_t[:, None, :], sin_t[:, None, :])
    cos_s, sin_s = _rope_tables(past_len + jnp.arange(dec_t, dtype=jnp.int32))

    y_p, y_s = x_prompt, x_sample.reshape(1, n_seq, d)
    outs = [[] for _ in range(8)]
    for l in range(depth):
        w = dict(w_out_p=w_out[l, :pw].astype(BF16), w_out_a=w_out[l, pw:].astype(BF16),
                 g_mix_post=g_mix_post[l].reshape(1, d), g_ffn_pre=g_ffn_pre[l].reshape(1, d),
                 g_ffn_post=g_ffn_post[l].reshape(1, d), w_up=w_up[l].astype(BF16), conv_w=conv_w[l],
                 conv_b=conv_b[l].reshape(1, 2 * d_ff), w_down=w_down[l].astype(BF16))
        g_pre = g_mix_pre[l].reshape(1, d)
        w_in_b = w_in[l].astype(BF16)
        pool_w_b = pool_w[l].astype(BF16)
        pscale = pool_scale[l].reshape(1, pw)

        mod_p, mod_s = _mod_call(c_prompt, c_sample, w_ada[l], b_ada[l])
        mod_p, mod_s = mod_p.reshape(b, 1, 6 * d), mod_s.reshape(1, n_seq, 6 * d)

        k_p, v_p, qt, kb, vt4, pool_p, utail = _pre_prompt_call(
            y_p, mod_p, g_pre, w_in_b, pool_w_b, pscale, rope_p, pw=pw, aw=aw)
        pool_state_s, q_s, k_s, v_s, pool_s = _pre_sample_call(
            y_s[0], mod_s[0], g_pre, w_in_b, pool_w_b, pscale, cos_s, sin_s, jnp.swapaxes(state_pool[l], 0, 1),
            pw=pw, aw=aw, pos0=past_len)
        tok = lambda a: a.reshape(n_seq, 1, aw)
        att_t, att_s = _moba_call(qt, kb.reshape(b, t // MOBA_BLOCK, MOBA_BLOCK, aw), vt4,
                                  page_table, tok(q_s), tok(k_s), tok(v_s),
                                  jnp.transpose(cache_k[l], (0, 2, 3, 1)), jnp.transpose(cache_v[l], (0, 2, 3, 1)))
        y_p, ctail = _post_call(y_p, pool_p, att_t, mod_p, w, att_transposed=True)
        y_s, conv_state_s = _post_call(y_s, pool_s.reshape(1, n_seq, pw), att_s.reshape(1, n_seq, aw).astype(BF16),
                                       mod_s, w, att_transposed=False, state=state_conv[l][None])

        outs[0].append(k_p.reshape(b, t, n_heads, HEAD_DIM))
        outs[1].append(v_p.reshape(b, t, n_heads, HEAD_DIM))
        outs[2].append(utail[:, POOL_HALO - POOL_STATE:])
        outs[3].append(ctail[:, SUBLANES - (CONV_WIDTH - 1):])
        outs[4].append(k_s.reshape(n_seq, dec_t, n_heads, HEAD_DIM))
        outs[5].append(v_s.reshape(n_seq, dec_t, n_heads, HEAD_DIM))
        outs[6].append(jnp.swapaxes(pool_state_s, 0, 1))
        outs[7].append(conv_state_s[0])
    stacked = [jnp.stack(o) for o in outs]
    return (y_p, y_s.reshape(n_seq, dec_t, d), *stacked)
```

```python
import functools
import math

import jax
import jax.numpy as jnp
from jax import lax
from jax.experimental import pallas as pl
from jax.experimental.pallas import tpu as pltpu

HEAD_DIM = 64
MOBA_BLOCK = 256
MOBA_TOPK = 3
POOL_WINDOWS = (2, 4, 8, 16)
POOL_STATE = max(POOL_WINDOWS) - 1
CONV_WIDTH = 3
ROPE_THETA = 10000.0
RMS_EPS = 1e-6

LANES = 128
SUBLANES = 8
VMEM_LIMIT_BYTES = 56 * 1024 * 1024

BF16_ROWS = 16
V_ROWS = HEAD_DIM + BF16_ROWS

POOL_HALO = 16
PRE_TILE = 256
SAMPLE_SEQS = 2
POST_TILE = 512
SAMPLE_ROWS = 32
FFN_CHUNKS = 11
FFN_LOOKAHEAD = 2
FFN_DOWN_SPLITS = 2
PAST_GROUP = 8
SOFTMAX_BLOCKS = 1
SCORE_LOOKAHEAD = 6

F32 = jnp.float32
BF16 = jnp.bfloat16
LOG2E = math.log2(math.e)
MASKED = -1e30


def _params(*semantics):
    return pltpu.CompilerParams(dimension_semantics=semantics, vmem_limit_bytes=VMEM_LIMIT_BYTES)


def _resident(shape, index_map):
    return pl.BlockSpec(shape, index_map, pipeline_mode=pl.Buffered(1))


def _rms(x, g):
    ms = jnp.mean(x * x, axis=-1, keepdims=True)
    return x * lax.rsqrt(ms + RMS_EPS) * g


def _rope(x, cos, sin_signed):
    lane = lax.broadcasted_iota(jnp.int32, (1, LANES), 1)
    first_half = (lane % HEAD_DIM) < HEAD_DIM // 2
    half = HEAD_DIM // 2
    outs = []
    for c in range(x.shape[1] // LANES):
        blk = x[:, c * LANES:(c + 1) * LANES]
        partner = jnp.where(first_half, pltpu.roll(blk, LANES - half, 1), pltpu.roll(blk, half, 1))
        outs.append(blk * cos + partner * sin_signed)
    return jnp.concatenate(outs, axis=1)


def _pool_project(pooled, poolw_ref, pscale):
    n_groups, gc, _ = poolw_ref.shape
    outs = [jnp.dot(pooled[:, g * gc:(g + 1) * gc].astype(BF16), poolw_ref[g], preferred_element_type=F32)
            for g in range(n_groups)]
    return jnp.concatenate(outs, axis=1) * pscale


def _mod_kernel(cp_ref, cs_ref, wa_ref, wb_ref, b_ref, op_ref, os_ref):
    half = wa_ref.shape[1]
    acts = [(c_ref[...] * jax.nn.sigmoid(c_ref[...])).astype(BF16) for c_ref in (cp_ref, cs_ref)]
    for r, w_ref in enumerate((wa_ref, wb_ref)):
        cols = slice(r * half, (r + 1) * half)
        w = w_ref[...].astype(BF16)
        for a, o_ref in zip(acts, (op_ref, os_ref)):
            o_ref[:, cols] = jnp.dot(a, w, preferred_element_type=F32) + b_ref[:, cols]


def _mod_call(c_prompt, c_sample, w_ada, b_ada):
    d, n_out = w_ada.shape
    rows_p, rows_s = c_prompt.shape[0], c_sample.shape[0]
    full = lambda j: (0, 0)
    col = lambda j: (0, j)
    halves = [pl.BlockSpec((d, d // 2), lambda j, r=r: (0, 2 * j + r)) for r in range(2)]
    return pl.pallas_call(
        _mod_kernel,
        grid=(n_out // d,),
        in_specs=[pl.BlockSpec((rows_p, d), full), pl.BlockSpec((rows_s, d), full), *halves,
                  pl.BlockSpec((1, d), col)],
        out_specs=[pl.BlockSpec((rows_p, d), col), pl.BlockSpec((rows_s, d), col)],
        out_shape=[jax.ShapeDtypeStruct((rows_p, n_out), F32), jax.ShapeDtypeStruct((rows_s, n_out), F32)],
        compiler_params=_params("arbitrary"),
        name="adaln_mod",
    )(c_prompt, c_sample, w_ada, w_ada, b_ada.reshape(1, n_out))


def _pre_prompt_kernel(pt_ref, x_ref, mod_ref, g_ref, win_ref, poolw_ref, pscale_ref, cos_ref, sin_ref, cost_ref,
                       sint_ref, qs_ref, kn_ref, vn_ref, ck_hbm, cv_hbm,
                       k_ref, v_ref, qt_ref, kb_ref, vt_ref, pool_ref, utail_ref, os_ref,
                       ubuf, s2buf, s4buf, s8buf, kbuf, vbuf, page_sem, *, tm, d_model, pw, aw, q_scale, n_seq):
    i = pl.program_id(1)
    halo = POOL_HALO
    gc = pw // len(POOL_WINDOWS)
    bufs = (ubuf, s2buf, s4buf, s8buf)
    _, seqs_per_step, n_pages, _, _, page_size = kbuf.shape
    step = pl.program_id(0) * pl.num_programs(1) + i
    n_steps = pl.num_programs(0) * pl.num_programs(1)
    slot = step % 2

    def page_copies(s, into):
        copies = []
        group = jnp.minimum(s, n_seq // seqs_per_step - 1)
        for j in range(seqs_per_step):
            seq = group * seqs_per_step + j
            for hbm, buf, which in ((ck_hbm, kbuf, 0), (cv_hbm, vbuf, 1)):
                for p in range(n_pages):
                    copies.append(pltpu.make_async_copy(hbm.at[pt_ref[seq, p]], buf.at[into, j, p],
                                                        page_sem.at[into, which]))
        return copies

    @pl.when(step == 0)
    def _():
        for copy in page_copies(step, slot):
            copy.start()

    @pl.when(step + 1 < n_steps)
    def _():
        for copy in page_copies(step + 1, 1 - slot):
            copy.start()

    for copy in page_copies(step, slot):
        copy.wait()

    @pl.when(i == 0)
    def _():
        for buf in bufs:
            buf[0:halo, :] = jnp.zeros((halo, buf.shape[1]), F32)

    sample_weights = [_sample_weights(qs_ref[j], kn_ref[j], [kbuf.at[slot, j, p] for p in range(n_pages)], page_size)
                      for j in range(seqs_per_step)]

    x = x_ref[0]
    mod = mod_ref[0]
    shift, scale = mod[:, 0:d_model], mod[:, d_model:2 * d_model]
    h = _rms(x, g_ref[...]) * (1.0 + scale) + shift
    proj = jnp.dot(h.astype(BF16), win_ref[...], preferred_element_type=F32)
    u = proj[:, 0:pw]
    cos_t, sin_t, cos_r, sin_r = cost_ref[0], sint_ref[0], cos_ref[...], sin_ref[...]
    cos = cos_t * cos_r - sin_t * sin_r
    sin_signed = sin_t * cos_r + cos_t * sin_r
    q = _rope(proj[:, pw:pw + aw], cos, sin_signed)
    k = _rope(proj[:, pw + aw:pw + 2 * aw], cos, sin_signed)
    v = proj[:, pw + 2 * aw:pw + 3 * aw]

    k_ref[0] = k
    v_ref[0] = v
    kb_ref[0] = k.astype(BF16)
    qt_ref[0] = (q * q_scale).T.astype(BF16)
    ones_pad = jnp.where(lax.broadcasted_iota(jnp.int32, (BF16_ROWS, MOBA_BLOCK), 0) == 0, 1.0, 0.0).astype(BF16)
    for r in range(tm // MOBA_BLOCK):
        vt = v[r * MOBA_BLOCK:(r + 1) * MOBA_BLOCK, :].T.astype(BF16)
        for hd in range(aw // HEAD_DIM):
            vt_ref[0, r, hd * V_ROWS:hd * V_ROWS + HEAD_DIM, :] = vt[hd * HEAD_DIM:(hd + 1) * HEAD_DIM, :]
            vt_ref[0, r, hd * V_ROWS + HEAD_DIM:(hd + 1) * V_ROWS, :] = ones_pad

    ubuf[halo:halo + tm, :] = u
    s2 = u + ubuf[halo - 1:halo - 1 + tm, :]
    s2buf[halo:halo + tm, :] = s2[:, gc:]
    s4 = s2[:, gc:] + s2buf[halo - 2:halo - 2 + tm, :]
    s4buf[halo:halo + tm, :] = s4[:, gc:]
    s8 = s4[:, gc:] + s4buf[halo - 4:halo - 4 + tm, :]
    s8buf[halo:halo + tm, :] = s8[:, gc:]
    s16 = s8[:, gc:] + s8buf[halo - 8:halo - 8 + tm, :]
    wins = (s2[:, 0:gc], s4[:, 0:gc], s8[:, 0:gc], s16)
    pos1 = (i * tm + 1 + lax.broadcasted_iota(jnp.int32, (tm, 1), 0)).astype(F32)
    pooled = jnp.concatenate(
        [wins[g] / jnp.minimum(float(w), pos1) - u[:, g * gc:(g + 1) * gc] for g, w in enumerate(POOL_WINDOWS)],
        axis=1)
    pool_ref[0] = _pool_project(pooled, poolw_ref, pscale_ref[...]).astype(BF16)

    for buf in bufs:
        buf[0:halo, :] = buf[tm:tm + halo, :]
    utail_ref[0] = ubuf[0:halo, :]

    for j in range(seqs_per_step):
        os_ref[j] = _sample_values(*sample_weights[j], vn_ref[j], [vbuf.at[slot, j, p] for p in range(n_pages)],
                                   page_size)


def _pre_prompt_call(x, mod, g_pre, w_in_b, pool_w_b, pool_scale, rope, page_table, q_s, k_s, v_s, cache_kt, cache_vt,
                     *, pw, aw):
    b, t, d = x.shape
    tm = min(PRE_TILE, t)
    n_tiles = t // tm
    nb = t // MOBA_BLOCK
    gc = pw // len(POOL_WINDOWS)
    in_w = w_in_b.shape[1]
    vt_rows = (aw // HEAD_DIM) * V_ROWS
    n_seq, n_pages = page_table.shape
    _, n_heads, dh, page_size = cache_kt.shape
    spp = SAMPLE_SEQS
    assert dh == HEAD_DIM and MOBA_BLOCK % page_size == 0 and (n_pages * page_size) % MOBA_BLOCK == 0
    assert n_seq % spp == 0 and n_seq // spp <= b * n_tiles
    kern = functools.partial(_pre_prompt_kernel, tm=tm, d_model=d, pw=pw, aw=aw,
                             q_scale=HEAD_DIM ** -0.5 * LOG2E, n_seq=n_seq)
    row_tile = lambda bi, i, pt: (bi, i, 0)
    const2 = lambda bi, i, pt: (0, 0)
    tok = pl.BlockSpec((spp, 1, aw), lambda bi, i, pt: (jnp.minimum(bi * n_tiles + i, n_seq // spp - 1), 0, 0))
    whole_cache = pl.BlockSpec(memory_space=pl.ANY)
    page_slots = pltpu.VMEM((2, spp, n_pages, n_heads, dh, page_size), F32)
    return pl.pallas_call(
        kern,
        grid_spec=pltpu.PrefetchScalarGridSpec(
            num_scalar_prefetch=1,
            grid=(b, n_tiles),
            in_specs=[pl.BlockSpec((1, tm, d), row_tile),
                      pl.BlockSpec((1, 1, mod.shape[2]), lambda bi, i, pt: (bi, 0, 0)),
                      _resident((1, d), const2),
                      _resident((d, in_w), const2),
                      _resident(pool_w_b.shape, lambda bi, i, pt: (0, 0, 0)),
                      _resident((1, pw), const2),
                      _resident((tm, LANES), const2),
                      _resident((tm, LANES), const2),
                      pl.BlockSpec((1, 1, LANES), lambda bi, i, pt: (i, 0, 0)),
                      pl.BlockSpec((1, 1, LANES), lambda bi, i, pt: (i, 0, 0)),
                      tok, tok, tok, whole_cache, whole_cache],
            out_specs=[pl.BlockSpec((1, tm, aw), row_tile),
                       pl.BlockSpec((1, tm, aw), row_tile),
                       pl.BlockSpec((1, aw, tm), lambda bi, i, pt: (bi, 0, i)),
                       pl.BlockSpec((1, tm, aw), row_tile),
                       pl.BlockSpec((1, tm // MOBA_BLOCK, vt_rows, MOBA_BLOCK), lambda bi, i, pt: (bi, i, 0, 0)),
                       pl.BlockSpec((1, tm, pw), row_tile),
                       pl.BlockSpec((1, POOL_HALO, pw), lambda bi, i, pt: (bi, 0, 0)),
                       tok],
            scratch_shapes=[pltpu.VMEM((tm + POOL_HALO, pw), F32),
                            pltpu.VMEM((tm + POOL_HALO, pw - gc), F32),
                            pltpu.VMEM((tm + POOL_HALO, pw - 2 * gc), F32),
                            pltpu.VMEM((tm + POOL_HALO, pw - 3 * gc), F32),
                            page_slots, page_slots,
                            pltpu.SemaphoreType.DMA((2, 2))]),
        out_shape=[jax.ShapeDtypeStruct((b, t, aw), F32),
                   jax.ShapeDtypeStruct((b, t, aw), F32),
                   jax.ShapeDtypeStruct((b, aw, t), BF16),
                   jax.ShapeDtypeStruct((b, t, aw), BF16),
                   jax.ShapeDtypeStruct((b, nb, vt_rows, MOBA_BLOCK), BF16),
                   jax.ShapeDtypeStruct((b, t, pw), BF16),
                   jax.ShapeDtypeStruct((b, POOL_HALO, pw), F32),
                   jax.ShapeDtypeStruct((n_seq, 1, aw), F32)],
        compiler_params=_params("arbitrary", "arbitrary"),
        name="pre_prompt",
    )(page_table, x, mod, g_pre, w_in_b, pool_w_b, pool_scale, *rope, q_s, k_s, v_s, cache_kt, cache_vt)


def _moba_kernel(qt_ref, kb_ref, vt_ref, o_ref, km_ref, ind_ref, q2_ref, m_ref, acc_ref, *, nb, n_heads):
    i = pl.program_id(1)
    tq = MOBA_BLOCK
    pair_w = 2 * HEAD_DIM

    @pl.when(i == 0)
    def _():
        lane = lax.broadcasted_iota(jnp.int32, (MOBA_BLOCK, LANES), 1)
        for n in range(nb):
            km_ref[n:n + 1, :] = jnp.mean(kb_ref[0, n].astype(F32), axis=0, keepdims=True)
            ind_ref[n] = jnp.where(lane == n, 1.0, 0.0).astype(BF16)
        for h in range(n_heads):
            q2_ref[h, pair_w + nb:, :] = jnp.zeros((pair_w - nb, tq), BF16)

    blk_row = lax.broadcasted_iota(jnp.int32, (nb, tq), 0)
    blk_row_f = blk_row.astype(F32)
    fully_past = blk_row < i
    key_row = lax.broadcasted_iota(jnp.int32, (MOBA_BLOCK, tq), 0)
    qry_col = lax.broadcasted_iota(jnp.int32, (MOBA_BLOCK, tq), 1)
    causal = key_row <= qry_col
    pair_row = lax.broadcasted_iota(jnp.int32, (pair_w, tq), 0)

    for h in range(n_heads):
        lo = (h // 2) * pair_w
        mine = (pair_row >= HEAD_DIM) if h % 2 else (pair_row < HEAD_DIM)
        q2 = jnp.where(mine, qt_ref[0, lo:lo + pair_w, :], jnp.zeros((), BF16))
        q2_ref[h, 0:pair_w, :] = q2

        gate = jnp.dot(km_ref[:, lo:lo + pair_w].astype(BF16), q2, preferred_element_type=F32)
        gate = jnp.where(fully_past, gate, -jnp.inf)
        chosen = jnp.zeros((nb, tq), jnp.bool_)
        for _ in range(MOBA_TOPK):
            best = jnp.max(gate, axis=0, keepdims=True)
            first = jnp.min(jnp.where(gate == best, blk_row_f, float(nb)), axis=0, keepdims=True)
            pick = blk_row_f == first
            chosen = jnp.logical_or(chosen, pick)
            gate = jnp.where(pick, -jnp.inf, gate)
        q2_ref[h, pair_w:pair_w + nb, :] = jnp.where(jnp.logical_and(chosen, fully_past), 0.0, MASKED).astype(BF16)

    def scores(js, h):
        lo = (h // 2) * pair_w
        return [jnp.dot(jnp.concatenate([kb_ref[0, j, :, lo:lo + pair_w], ind_ref[j]], axis=1), q2_ref[h],
                        preferred_element_type=F32) for j in js]

    def own_scores(js, h):
        lo = (h // 2) * pair_w
        return [jnp.dot(kb_ref[0, j, :, lo:lo + pair_w], q2_ref[h, 0:pair_w, :], preferred_element_type=F32)
                for j in js]

    def all_heads(js, consume, scores=scores):
        items = [(js[r:r + SOFTMAX_BLOCKS], h) for r in range(0, len(js), SOFTMAX_BLOCKS) for h in range(n_heads)]
        look = min(SCORE_LOOKAHEAD, len(items))
        ahead = {k: scores(*items[k]) for k in range(look)}
        for k, (sub, h) in enumerate(items):
            if k + look < len(items):
                ahead[k + look] = scores(*items[k + look])
            consume(sub, h, ahead.pop(k))

    def values(j, h):
        return vt_ref[0, j, h * V_ROWS:(h + 1) * V_ROWS, :]

    def first_block(_, h, ss):
        s = jnp.where(causal, ss[0], -jnp.inf)
        m0 = jnp.max(s, axis=0, keepdims=True)
        m_ref[h] = m0
        acc_ref[h] = jnp.dot(values(i, h), jnp.exp2(s - m0).astype(BF16), preferred_element_type=F32)

    all_heads([i], first_block, own_scores)

    def online_update(js, h, ss, m_old, acc_old):
        m_new = m_old
        for s in ss:
            m_new = jnp.maximum(m_new, jnp.max(s, axis=0, keepdims=True))
        pv = None
        for j, s in zip(js, ss):
            d = jnp.dot(values(j, h), jnp.exp2(s - m_new).astype(BF16), preferred_element_type=F32)
            pv = d if pv is None else pv + d
        return m_new, jnp.exp2(m_old - m_new) * acc_old + pv

    def past_blocks(js, h, ss):
        m_ref[h], acc_ref[h] = online_update(js, h, ss, m_ref[h], acc_ref[h])

    size, top = 1, i
    while size < PAST_GROUP:
        @pl.when((i & size) != 0)
        def _(size=size, top=top):
            all_heads([top - size + r for r in range(size)], past_blocks)

        top = top - (i & size)
        size *= 2

    def group_body(t, carry):
        all_heads([PAST_GROUP * t + r for r in range(PAST_GROUP)], past_blocks)
        return carry

    lax.fori_loop(0, i // PAST_GROUP, group_body, 0)

    for h in range(n_heads):
        acc = acc_ref[h]
        o_ref[0, h * HEAD_DIM:(h + 1) * HEAD_DIM, :] = (acc[0:HEAD_DIM] / acc[HEAD_DIM:HEAD_DIM + 1]).astype(BF16)


def _moba_call(qt, kb4, vt4):
    b, aw, t = qt.shape
    nb = t // MOBA_BLOCK
    n_heads = aw // HEAD_DIM
    assert nb <= LANES and n_heads % 2 == 0
    kern = functools.partial(_moba_kernel, nb=nb, n_heads=n_heads)
    per_batch = lambda bi, i: (bi, 0, 0, 0)
    q_tile = pl.BlockSpec((1, aw, MOBA_BLOCK), lambda bi, i: (bi, 0, i))
    return pl.pallas_call(
        kern,
        grid=(b, nb),
        in_specs=[q_tile,
                  pl.BlockSpec((1, nb, MOBA_BLOCK, aw), per_batch, pipeline_mode=pl.Buffered(1)),
                  pl.BlockSpec((1, nb, n_heads * V_ROWS, MOBA_BLOCK), per_batch, pipeline_mode=pl.Buffered(1))],
        out_specs=q_tile,
        out_shape=jax.ShapeDtypeStruct((b, aw, t), BF16),
        scratch_shapes=[pltpu.VMEM((nb, aw), F32),
                        pltpu.VMEM((nb, MOBA_BLOCK, LANES), BF16),
                        pltpu.VMEM((n_heads, 4 * HEAD_DIM, MOBA_BLOCK), BF16),
                        pltpu.VMEM((n_heads, 1, MOBA_BLOCK), F32),
                        pltpu.VMEM((n_heads, V_ROWS, MOBA_BLOCK), F32)],
        compiler_params=_params("arbitrary", "arbitrary"),
        name="moba",
    )(qt, kb4, vt4)


def _post_kernel(*refs, tm, d_model, d_ff, att_transposed, per_row_state):
    (x_ref, pool_ref, att_ref, mod_ref, wop_ref, woa_ref, gpost_ref, gpre2_ref, gpost2_ref,
     wup_ref, cw_ref, cb_ref, wdn_ref) = refs[:13]
    if per_row_state:
        st_ref, y_ref, stn_ref, actbuf = refs[13:]
    else:
        y_ref, ctail_ref, carry, upbuf, actbuf = refs[13:]
        i = pl.program_id(1)

        @pl.when(i == 0)
        def _():
            carry[...] = jnp.zeros(carry.shape, F32)

    d = d_model
    x = x_ref[0]
    mod = mod_ref[0]
    gate1 = mod[:, 2 * d:3 * d]
    shift2, scale2, gate2 = mod[:, 3 * d:4 * d], mod[:, 4 * d:5 * d], mod[:, 5 * d:6 * d]

    mix = jnp.dot(pool_ref[0], wop_ref[...], preferred_element_type=F32)
    if att_transposed:
        mix += lax.dot_general(att_ref[0], woa_ref[...], (((0,), (0,)), ((), ())), preferred_element_type=F32)
    else:
        mix += jnp.dot(att_ref[0], woa_ref[...], preferred_element_type=F32)
    x1 = x + gate1 * _rms(mix, gpost_ref[...])
    h2 = (_rms(x1, gpre2_ref[...]) * (1.0 + scale2) + shift2).astype(BF16)

    fc = d_ff // FFN_CHUNKS

    def up_proj(c):
        out = []
        for base in (0, d_ff):
            cols = slice(base + c * fc, base + (c + 1) * fc)
            out.append((cols, jnp.dot(h2, wup_ref[:, cols], preferred_element_type=F32)))
        return out

    def conv_act(c, ups):
        halves = []
        for half, (cols, up) in enumerate(ups):
            if per_row_state:
                prev2, prev1 = st_ref[0, :, 0, cols], st_ref[0, :, 1, cols]
                stn_ref[0, :, 0, cols] = prev1
                stn_ref[0, :, 1, cols] = up
            else:
                buf = upbuf.at[(2 * c + half) % upbuf.shape[0]]
                buf[0:SUBLANES, :] = carry[:, cols]
                buf[SUBLANES:SUBLANES + tm, :] = up
                prev1 = buf[SUBLANES - 1:SUBLANES - 1 + tm, :]
                prev2 = buf[SUBLANES - 2:SUBLANES - 2 + tm, :]
                carry[:, cols] = buf[tm:tm + SUBLANES, :]
            halves.append(cb_ref[:, cols] + cw_ref[0:1, cols] * prev2 + cw_ref[1:2, cols] * prev1
                          + cw_ref[2:3, cols] * up)
        a, g = halves
        actbuf[:, c * fc:(c + 1) * fc] = (a * jax.nn.sigmoid(a) * g).astype(BF16)

    split_at = [((s + 1) * FFN_CHUNKS) // FFN_DOWN_SPLITS for s in range(FFN_DOWN_SPLITS)]
    ahead = {c: up_proj(c) for c in range(min(FFN_LOOKAHEAD, FFN_CHUNKS))}
    ffn, done = None, 0
    for c in range(FFN_CHUNKS):
        if c + FFN_LOOKAHEAD < FFN_CHUNKS:
            ahead[c + FFN_LOOKAHEAD] = up_proj(c + FFN_LOOKAHEAD)
        conv_act(c, ahead.pop(c))
        if c + 1 in split_at:
            rows = slice(done * fc, (c + 1) * fc)
            part = jnp.dot(actbuf[:, rows], wdn_ref[rows, :], preferred_element_type=F32)
            ffn = part if ffn is None else ffn + part
            done = c + 1
    y_ref[0] = x1 + gate2 * _rms(ffn, gpost2_ref[...])
    if not per_row_state:
        ctail_ref[0] = carry[...]


def _post_call(x, pool_o, att, mod, w, *, att_transposed, state=None):
    b, t, d = x.shape
    pw = pool_o.shape[2]
    aw = w["w_out_a"].shape[0]
    d_ff = w["w_down"].shape[0]
    per_row = state is not None
    tm = t if per_row else min(POST_TILE, t)
    assert t % tm == 0 and d_ff % (FFN_CHUNKS * LANES) == 0 and CONV_WIDTH == 3
    fc = d_ff // FFN_CHUNKS
    kern = functools.partial(_post_kernel, tm=tm, d_model=d, d_ff=d_ff, att_transposed=att_transposed,
                             per_row_state=per_row)
    row_tile = lambda bi, i: (bi, i, 0)
    const2 = lambda bi, i: (0, 0)
    mod_rows = mod.shape[1]
    in_specs = [pl.BlockSpec((1, tm, d), row_tile),
                pl.BlockSpec((1, tm, pw), row_tile),
                (pl.BlockSpec((1, aw, tm), lambda bi, i: (bi, 0, i)) if att_transposed
                 else pl.BlockSpec((1, tm, aw), row_tile)),
                pl.BlockSpec((1, mod_rows, mod.shape[2]), (row_tile if mod_rows > 1 else lambda bi, i: (bi, 0, 0))),
                _resident((pw, d), const2),
                _resident((aw, d), const2),
                _resident((1, d), const2),
                _resident((1, d), const2),
                _resident((1, d), const2),
                _resident((d, 2 * d_ff), const2),
                _resident((CONV_WIDTH, 2 * d_ff), const2),
                _resident((1, 2 * d_ff), const2),
                _resident((d_ff, d), const2)]
    args = [x, pool_o, att, mod, w["w_out_p"], w["w_out_a"], w["g_mix_post"], w["g_ffn_pre"], w["g_ffn_post"],
            w["w_up"], w["conv_w"], w["conv_b"], w["w_down"]]
    if per_row:
        state_block = pl.BlockSpec((1, tm, CONV_WIDTH - 1, 2 * d_ff), lambda bi, i: (bi, i, 0, 0))
        in_specs.append(state_block)
        args.append(state)
        out_specs = [pl.BlockSpec((1, tm, d), row_tile), state_block]
        out_shape = [jax.ShapeDtypeStruct((b, t, d), F32), jax.ShapeDtypeStruct(state.shape, F32)]
        scratch = []
    else:
        out_specs = [pl.BlockSpec((1, tm, d), row_tile),
                     pl.BlockSpec((1, SUBLANES, 2 * d_ff), lambda bi, i: (bi, 0, 0))]
        out_shape = [jax.ShapeDtypeStruct((b, t, d), F32), jax.ShapeDtypeStruct((b, SUBLANES, 2 * d_ff), F32)]
        scratch = [pltpu.VMEM((SUBLANES, 2 * d_ff), F32),
                   pltpu.VMEM((4, tm + SUBLANES, fc), F32)]
    scratch.append(pltpu.VMEM((tm, d_ff), BF16))
    return pl.pallas_call(
        kern,
        grid=(b, t // tm),
        in_specs=in_specs,
        out_specs=out_specs,
        out_shape=out_shape,
        scratch_shapes=scratch,
        compiler_params=_params("arbitrary", "arbitrary"),
        name="post_sample" if per_row else "post_prompt",
    )(*args)


def _pre_sample_kernel(x_ref, mod_ref, g_ref, win_ref, poolw_ref, pscale_ref, cos_ref, sin_ref, st_ref,
                       stn_ref, q_ref, k_ref, v_ref, pool_ref, *, d_model, pw, aw, pos0):
    gc = pw // len(POOL_WINDOWS)
    x = x_ref[...]
    mod = mod_ref[...]
    shift, scale = mod[:, 0:d_model], mod[:, d_model:2 * d_model]
    h = (_rms(x, g_ref[...]) * (1.0 + scale) + shift).astype(BF16)
    project = lambda lo, hi: jnp.dot(h, win_ref[:, lo:hi], preferred_element_type=F32)
    u = project(0, pw)
    cos, sin_signed = cos_ref[...], sin_ref[...]
    q_ref[...] = _rope(project(pw, pw + aw), cos, sin_signed)
    k_ref[...] = _rope(project(pw + aw, pw + 2 * aw), cos, sin_signed)
    v_ref[...] = project(pw + 2 * aw, pw + 3 * aw)

    pooled = []
    for g, w in enumerate(POOL_WINDOWS):
        cols = slice(g * gc, (g + 1) * gc)
        win = u[:, cols]
        for r in range(POOL_STATE - (w - 1), POOL_STATE):
            win = win + st_ref[r, :, cols]
        pooled.append(win / float(min(w, pos0 + 1)) - u[:, cols])
    pool_ref[...] = _pool_project(jnp.concatenate(pooled, axis=1), poolw_ref, pscale_ref[...]).astype(BF16)

    for r in range(POOL_STATE - 1):
        stn_ref[r] = st_ref[r + 1]
    stn_ref[POOL_STATE - 1] = u


def _pre_sample_call(x, mod, g_pre, w_in_b, pool_w_b, pool_scale, cos_t, sin_t, state_pool, *, pw, aw, pos0):
    rows, d = x.shape
    tr = SAMPLE_ROWS if rows % SAMPLE_ROWS == 0 else rows
    kern = functools.partial(_pre_sample_kernel, d_model=d, pw=pw, aw=aw, pos0=pos0)
    const2 = lambda i: (0, 0)
    seqs = lambda i: (i, 0)
    return pl.pallas_call(
        kern,
        grid=(rows // tr,),
        in_specs=[pl.BlockSpec((tr, d), seqs),
                  pl.BlockSpec((tr, 2 * d), seqs),
                  _resident((1, d), const2),
                  _resident(w_in_b.shape, const2),
                  _resident(pool_w_b.shape, lambda i: (0, 0, 0)),
                  _resident((1, pw), const2),
                  _resident((1, LANES), const2),
                  _resident((1, LANES), const2),
                  pl.BlockSpec((POOL_STATE, tr, pw), lambda i: (0, i, 0))],
        out_specs=[pl.BlockSpec((POOL_STATE, tr, pw), lambda i: (0, i, 0)),
                   pl.BlockSpec((tr, aw), seqs),
                   pl.BlockSpec((tr, aw), seqs),
                   pl.BlockSpec((tr, aw), seqs),
                   pl.BlockSpec((tr, pw), seqs)],
        out_shape=[jax.ShapeDtypeStruct((POOL_STATE, rows, pw), F32),
                   jax.ShapeDtypeStruct((rows, aw), F32),
                   jax.ShapeDtypeStruct((rows, aw), F32),
                   jax.ShapeDtypeStruct((rows, aw), F32),
                   jax.ShapeDtypeStruct((rows, pw), BF16)],
        compiler_params=_params("arbitrary"),
        name="pre_sample",
    )(x, mod, g_pre, w_in_b, pool_w_b, pool_scale, cos_t, sin_t, state_pool)


def _own_lanes(n_heads):
    aw = n_heads * HEAD_DIM
    head_row = lax.broadcasted_iota(jnp.int32, (n_heads, aw), 0)
    head_of_lane = lax.broadcasted_iota(jnp.int32, (n_heads, aw), 1) // HEAD_DIM
    return head_row == head_of_lane


def _sample_weights(q, k_new, k_refs, page_size):
    n_pages = len(k_refs)
    n_heads = k_refs[0].shape[0]
    aw = n_heads * HEAD_DIM
    ppb = MOBA_BLOCK // page_size
    n_blocks = n_pages // ppb

    qbd = jnp.where(_own_lanes(n_heads), q * (HEAD_DIM ** -0.5 * LOG2E), 0.0)
    qbd_b = qbd.astype(BF16)

    scores = []
    for n in range(n_blocks):
        kt = jnp.concatenate([k_refs[p][...].reshape(aw, page_size).astype(BF16)
                              for p in range(n * ppb, (n + 1) * ppb)], axis=1)
        s = jnp.dot(qbd_b, kt, preferred_element_type=F32)
        scores += [s[:, r * page_size:(r + 1) * page_size] for r in range(ppb)]

    pages_of = lambda n: range(n * ppb, (n + 1) * ppb)
    gates = [sum(jnp.sum(scores[p], axis=-1, keepdims=True) for p in pages_of(n)) for n in range(n_blocks)]
    peaks = [functools.reduce(jnp.maximum, [jnp.max(scores[p], axis=-1, keepdims=True) for p in pages_of(n)])
             for n in range(n_blocks)]
    chosen = []
    for n in range(n_blocks):
        beaten = jnp.zeros((n_heads, 1), jnp.int32)
        for m in range(n_blocks):
            if m != n:
                wins = (gates[m] >= gates[n]) if m < n else (gates[m] > gates[n])
                beaten = beaten + wins.astype(jnp.int32)
        chosen.append(beaten < MOBA_TOPK)

    s_new = jnp.sum(qbd * k_new, axis=-1, keepdims=True)
    m = s_new
    for n in range(n_blocks):
        m = jnp.maximum(m, jnp.where(chosen[n], peaks[n], -jnp.inf))
    p_new = jnp.exp2(s_new - m)
    weights = [jnp.exp2(jnp.where(chosen[p // ppb], scores[p], -jnp.inf) - m) for p in range(n_pages)]
    inv_l = 1.0 / (p_new + sum(jnp.sum(w, axis=-1, keepdims=True) for w in weights))
    return weights, p_new, inv_l


def _sample_values(weights, p_new, inv_l, v_new, v_refs, page_size):
    n_pages = len(v_refs)
    n_heads = v_refs[0].shape[0]
    partial = []
    for h in range(n_heads):
        acc = None
        for p in range(n_pages):
            term = v_refs[p][h] * weights[p][h:h + 1, :]
            acc = term if acc is None else acc + term
        partial.append(acc)
    partial = jnp.concatenate(partial, axis=0)
    ones = jnp.ones((SUBLANES, page_size), BF16)
    high = partial.astype(BF16)
    low = (partial - high.astype(F32)).astype(BF16)
    contract_lanes = (((1,), (1,)), ((), ()))
    past = (lax.dot_general(ones, high, contract_lanes, preferred_element_type=F32)
            + lax.dot_general(ones, low, contract_lanes, preferred_element_type=F32))[0:1, :]
    own_lanes = _own_lanes(n_heads)
    inv_row = jnp.sum(jnp.where(own_lanes, inv_l, 0.0), axis=0, keepdims=True)
    own_row = jnp.sum(jnp.where(own_lanes, p_new, 0.0), axis=0, keepdims=True)
    return (past + own_row * v_new) * inv_row


def _rope_tables(positions):
    half = HEAD_DIM // 2
    lane = jnp.arange(LANES, dtype=jnp.int32)
    inv = ROPE_THETA ** (-(lane % half).astype(F32) / half)
    sign = jnp.where((lane % HEAD_DIM) < half, -1.0, 1.0).astype(F32)
    ang = positions.astype(F32)[:, None] * inv[None, :]
    return jnp.cos(ang), jnp.sin(ang) * sign[None, :]


def kernel(x_prompt, x_sample, c_prompt, c_sample, cache_k, cache_v, page_table, state_pool, state_conv, w_ada, b_ada, g_mix_pre, g_mix_post, w_in, pool_w, pool_scale, w_out, g_ffn_pre, g_ffn_post, w_up, conv_w, conv_b, w_down):
    depth = w_ada.shape[0]
    b, t, d = x_prompt.shape
    n_seq, dec_t, _ = x_sample.shape
    n_pages = page_table.shape[1]
    page_size, n_heads = cache_k.shape[2], cache_k.shape[3]
    past_len = n_pages * page_size
    aw = n_heads * HEAD_DIM
    pw = state_pool.shape[3]
    d_ff = w_down.shape[1]
    assert dec_t == 1 and t % MOBA_BLOCK == 0 and t % PRE_TILE == 0 and pw == d - aw
    assert (pw // len(POOL_WINDOWS)) % LANES == 0 and w_in.shape[2] == pw + 3 * aw

    tm = min(PRE_TILE, t)
    cos_r, sin_r = _rope_tables(jnp.arange(tm, dtype=jnp.int32))
    cos_t, sin_t = _rope_tables(tm * jnp.arange(t // tm, dtype=jnp.int32))
    rope_p = (cos_r, sin_r, cos_t[:, None, :], sin_t[:, None, :])
    cos_s, sin_s = _rope_tables(past_len + jnp.arange(dec_t, dtype=jnp.int32))

    y_p, y_s = x_prompt, x_sample.reshape(1, n_seq, d)
    outs = [[] for _ in range(8)]
    for l in range(depth):
        w = dict(w_out_p=w_out[l, :pw].astype(BF16), w_out_a=w_out[l, pw:].astype(BF16),
                 g_mix_post=g_mix_post[l].reshape(1, d), g_ffn_pre=g_ffn_pre[l].reshape(1, d),
                 g_ffn_post=g_ffn_post[l].reshape(1, d), w_up=w_up[l].astype(BF16), conv_w=conv_w[l],
                 conv_b=conv_b[l].reshape(1, 2 * d_ff), w_down=w_down[l].astype(BF16))
        g_pre = g_mix_pre[l].reshape(1, d)
        w_in_b = w_in[l].astype(BF16)
        pool_w_b = pool_w[l].astype(BF16)
        pscale = pool_scale[l].reshape(1, pw)

        mod_p, mod_s = _mod_call(c_prompt, c_sample, w_ada[l], b_ada[l])
        mod_p, mod_s = mod_p.reshape(b, 1, 6 * d), mod_s.reshape(1, n_seq, 6 * d)

        pool_state_s, q_s, k_s, v_s, pool_s = _pre_sample_call(
            y_s[0], mod_s[0], g_pre, w_in_b, pool_w_b, pscale, cos_s, sin_s, jnp.swapaxes(state_pool[l], 0, 1),
            pw=pw, aw=aw, pos0=past_len)
        tok = lambda a: a.reshape(n_seq, 1, aw)
        k_p, v_p, qt, kb, vt4, pool_p, utail, att_s = _pre_prompt_call(
            y_p, mod_p, g_pre, w_in_b, pool_w_b, pscale, rope_p, page_table, tok(q_s), tok(k_s), tok(v_s),
            jnp.transpose(cache_k[l], (0, 2, 3, 1)), jnp.transpose(cache_v[l], (0, 2, 3, 1)), pw=pw, aw=aw)
        att_t = _moba_call(qt, kb.reshape(b, t // MOBA_BLOCK, MOBA_BLOCK, aw), vt4)
        y_p, ctail = _post_call(y_p, pool_p, att_t, mod_p, w, att_transposed=True)
        y_s, conv_state_s = _post_call(y_s, pool_s.reshape(1, n_seq, pw), att_s.reshape(1, n_seq, aw).astype(BF16),
                                       mod_s, w, att_transposed=False, state=state_conv[l][None])

        outs[0].append(k_p.reshape(b, t, n_heads, HEAD_DIM))
        outs[1].append(v_p.reshape(b, t, n_heads, HEAD_DIM))
        outs[2].append(utail[:, POOL_HALO - POOL_STATE:])
        outs[3].append(ctail[:, SUBLANES - (CONV_WIDTH - 1):])
        outs[4].append(k_s.reshape(n_seq, dec_t, n_heads, HEAD_DIM))
        outs[5].append(v_s.reshape(n_seq, dec_t, n_heads, HEAD_DIM))
        outs[6].append(jnp.swapaxes(pool_state_s, 0, 1))
        outs[7].append(conv_state_s[0])
    stacked = [jnp.stack(o) for o in outs]
    return (y_p, y_s.reshape(n_seq, dec_t, d), *stacked)
```

```python
import functools
import math

import jax
import jax.numpy as jnp
from jax import lax
from jax.experimental import pallas as pl
from jax.experimental.pallas import tpu as pltpu

HEAD_DIM = 64
MOBA_BLOCK = 256
MOBA_TOPK = 3
POOL_WINDOWS = (2, 4, 8, 16)
POOL_STATE = max(POOL_WINDOWS) - 1
CONV_WIDTH = 3
ROPE_THETA = 10000.0
RMS_EPS = 1e-6

LANES = 128
SUBLANES = 8
VMEM_LIMIT_BYTES = 56 * 1024 * 1024

BF16_ROWS = 16
V_ROWS = HEAD_DIM + BF16_ROWS

POOL_HALO = 16
PRE_TILE = 256
SAMPLE_EARLY_SHARE = 0.5
POST_TILE = 512
SAMPLE_ROWS = 32
FFN_CHUNKS = 11
FFN_LOOKAHEAD = 2
FFN_DOWN_SPLITS = 2
PAST_GROUP = 8
SOFTMAX_BLOCKS = 1
SCORE_LOOKAHEAD = 6

F32 = jnp.float32
BF16 = jnp.bfloat16
LOG2E = math.log2(math.e)
MASKED = -1e30


def _params(*semantics):
    return pltpu.CompilerParams(dimension_semantics=semantics, vmem_limit_bytes=VMEM_LIMIT_BYTES)


def _resident(shape, index_map):
    return pl.BlockSpec(shape, index_map, pipeline_mode=pl.Buffered(1))


def _rms(x, g):
    ms = jnp.mean(x * x, axis=-1, keepdims=True)
    return x * lax.rsqrt(ms + RMS_EPS) * g


def _rope(x, cos, sin_signed):
    lane = lax.broadcasted_iota(jnp.int32, (1, LANES), 1)
    first_half = (lane % HEAD_DIM) < HEAD_DIM // 2
    half = HEAD_DIM // 2
    outs = []
    for c in range(x.shape[1] // LANES):
        blk = x[:, c * LANES:(c + 1) * LANES]
        partner = jnp.where(first_half, pltpu.roll(blk, LANES - half, 1), pltpu.roll(blk, half, 1))
        outs.append(blk * cos + partner * sin_signed)
    return jnp.concatenate(outs, axis=1)


def _pool_project(pooled, poolw_ref, pscale):
    n_groups, gc, _ = poolw_ref.shape
    outs = [jnp.dot(pooled[:, g * gc:(g + 1) * gc].astype(BF16), poolw_ref[g], preferred_element_type=F32)
            for g in range(n_groups)]
    return jnp.concatenate(outs, axis=1) * pscale


def _mod_kernel(cp_ref, cs_ref, wa_ref, wb_ref, b_ref, op_ref, os_ref):
    half = wa_ref.shape[1]
    acts = [(c_ref[...] * jax.nn.sigmoid(c_ref[...])).astype(BF16) for c_ref in (cp_ref, cs_ref)]
    for r, w_ref in enumerate((wa_ref, wb_ref)):
        cols = slice(r * half, (r + 1) * half)
        w = w_ref[...].astype(BF16)
        for a, o_ref in zip(acts, (op_ref, os_ref)):
            o_ref[:, cols] = jnp.dot(a, w, preferred_element_type=F32) + b_ref[:, cols]


def _mod_call(c_prompt, c_sample, w_ada, b_ada):
    d, n_out = w_ada.shape
    rows_p, rows_s = c_prompt.shape[0], c_sample.shape[0]
    full = lambda j: (0, 0)
    col = lambda j: (0, j)
    halves = [pl.BlockSpec((d, d // 2), lambda j, r=r: (0, 2 * j + r)) for r in range(2)]
    return pl.pallas_call(
        _mod_kernel,
        grid=(n_out // d,),
        in_specs=[pl.BlockSpec((rows_p, d), full), pl.BlockSpec((rows_s, d), full), *halves,
                  pl.BlockSpec((1, d), col)],
        out_specs=[pl.BlockSpec((rows_p, d), col), pl.BlockSpec((rows_s, d), col)],
        out_shape=[jax.ShapeDtypeStruct((rows_p, n_out), F32), jax.ShapeDtypeStruct((rows_s, n_out), F32)],
        compiler_params=_params("arbitrary"),
        name="adaln_mod",
    )(c_prompt, c_sample, w_ada, w_ada, b_ada.reshape(1, n_out))


def _pre_prompt_kernel(pt_ref, x_ref, mod_ref, g_ref, win_ref, poolw_ref, pscale_ref, cos_ref, sin_ref, cost_ref,
                       sint_ref, qs_ref, kn_ref, vn_ref, ck_hbm, cv_hbm,
                       k_ref, v_ref, qt_ref, kb_ref, vt_ref, pool_ref, utail_ref, os_ref,
                       ubuf, s2buf, s4buf, s8buf, kbuf, vbuf, page_sem, *, tm, d_model, pw, aw, q_scale, first_seq,
                       n_seq):
    i = pl.program_id(1)
    halo = POOL_HALO
    gc = pw // len(POOL_WINDOWS)
    bufs = (ubuf, s2buf, s4buf, s8buf)
    page_size = kbuf.shape[-1]
    step = pl.program_id(0) * pl.num_programs(1) + i
    k_pages, v_pages = _stage_pages(pt_ref, ck_hbm, cv_hbm, kbuf, vbuf, page_sem, step,
                                    pl.num_programs(0) * pl.num_programs(1), first_seq, n_seq)
    seqs_per_step = len(k_pages)

    @pl.when(i == 0)
    def _():
        for buf in bufs:
            buf[0:halo, :] = jnp.zeros((halo, buf.shape[1]), F32)

    sample_weights = [_sample_weights(qs_ref[j], kn_ref[j], k_pages[j], page_size) for j in range(seqs_per_step)]

    x = x_ref[0]
    mod = mod_ref[0]
    shift, scale = mod[:, 0:d_model], mod[:, d_model:2 * d_model]
    h = _rms(x, g_ref[...]) * (1.0 + scale) + shift
    proj = jnp.dot(h.astype(BF16), win_ref[...], preferred_element_type=F32)
    u = proj[:, 0:pw]
    cos_t, sin_t, cos_r, sin_r = cost_ref[0], sint_ref[0], cos_ref[...], sin_ref[...]
    cos = cos_t * cos_r - sin_t * sin_r
    sin_signed = sin_t * cos_r + cos_t * sin_r
    q = _rope(proj[:, pw:pw + aw], cos, sin_signed)
    k = _rope(proj[:, pw + aw:pw + 2 * aw], cos, sin_signed)
    v = proj[:, pw + 2 * aw:pw + 3 * aw]

    k_ref[0] = k
    v_ref[0] = v
    kb_ref[0] = k.astype(BF16)
    qt_ref[0] = (q * q_scale).T.astype(BF16)
    ones_pad = jnp.where(lax.broadcasted_iota(jnp.int32, (BF16_ROWS, MOBA_BLOCK), 0) == 0, 1.0, 0.0).astype(BF16)
    for r in range(tm // MOBA_BLOCK):
        vt = v[r * MOBA_BLOCK:(r + 1) * MOBA_BLOCK, :].T.astype(BF16)
        for hd in range(aw // HEAD_DIM):
            vt_ref[0, r, hd * V_ROWS:hd * V_ROWS + HEAD_DIM, :] = vt[hd * HEAD_DIM:(hd + 1) * HEAD_DIM, :]
            vt_ref[0, r, hd * V_ROWS + HEAD_DIM:(hd + 1) * V_ROWS, :] = ones_pad

    ubuf[halo:halo + tm, :] = u
    s2 = u + ubuf[halo - 1:halo - 1 + tm, :]
    s2buf[halo:halo + tm, :] = s2[:, gc:]
    s4 = s2[:, gc:] + s2buf[halo - 2:halo - 2 + tm, :]
    s4buf[halo:halo + tm, :] = s4[:, gc:]
    s8 = s4[:, gc:] + s4buf[halo - 4:halo - 4 + tm, :]
    s8buf[halo:halo + tm, :] = s8[:, gc:]
    s16 = s8[:, gc:] + s8buf[halo - 8:halo - 8 + tm, :]
    wins = (s2[:, 0:gc], s4[:, 0:gc], s8[:, 0:gc], s16)
    pos1 = (i * tm + 1 + lax.broadcasted_iota(jnp.int32, (tm, 1), 0)).astype(F32)
    pooled = jnp.concatenate(
        [wins[g] / jnp.minimum(float(w), pos1) - u[:, g * gc:(g + 1) * gc] for g, w in enumerate(POOL_WINDOWS)],
        axis=1)
    pool_ref[0] = _pool_project(pooled, poolw_ref, pscale_ref[...]).astype(BF16)

    for buf in bufs:
        buf[0:halo, :] = buf[tm:tm + halo, :]
    utail_ref[0] = ubuf[0:halo, :]

    for j in range(seqs_per_step):
        os_ref[j] = _sample_values(*sample_weights[j], vn_ref[j], v_pages[j], page_size)


def _sample_specs(n_steps, step_of, first_seq, n_seq, page_table, cache_kt):
    _, n_pages = page_table.shape
    _, n_heads, dh, page_size = cache_kt.shape
    assert dh == HEAD_DIM and MOBA_BLOCK % page_size == 0 and (n_pages * page_size) % MOBA_BLOCK == 0
    assert 0 < n_seq <= n_steps
    tok = pl.BlockSpec((1, 1, n_heads * dh),
                       lambda *idx: (first_seq + jnp.minimum(step_of(*idx[:-1]), n_seq - 1), 0, 0))
    whole_cache = pl.BlockSpec(memory_space=pl.ANY)
    page_slots = pltpu.VMEM((2, 1, n_pages, n_heads, dh, page_size), F32)
    return tok, whole_cache, [page_slots, page_slots, pltpu.SemaphoreType.DMA((2, 2))]


def _pre_prompt_call(x, mod, g_pre, w_in_b, pool_w_b, pool_scale, rope, page_table, q_s, k_s, v_s, cache_kt, cache_vt,
                     *, pw, aw, first_seq, n_seq):
    b, t, d = x.shape
    tm = min(PRE_TILE, t)
    n_tiles = t // tm
    nb = t // MOBA_BLOCK
    gc = pw // len(POOL_WINDOWS)
    in_w = w_in_b.shape[1]
    vt_rows = (aw // HEAD_DIM) * V_ROWS
    kern = functools.partial(_pre_prompt_kernel, tm=tm, d_model=d, pw=pw, aw=aw,
                             q_scale=HEAD_DIM ** -0.5 * LOG2E, first_seq=first_seq, n_seq=n_seq)
    row_tile = lambda bi, i, pt: (bi, i, 0)
    const2 = lambda bi, i, pt: (0, 0)
    tok, whole_cache, page_scratch = _sample_specs(b * n_tiles, lambda bi, i: bi * n_tiles + i, first_seq, n_seq,
                                                   page_table, cache_kt)
    tok_out = pl.BlockSpec(tok.block_shape, lambda bi, i, pt: (jnp.minimum(bi * n_tiles + i, n_seq - 1), 0, 0))
    return pl.pallas_call(
        kern,
        grid_spec=pltpu.PrefetchScalarGridSpec(
            num_scalar_prefetch=1,
            grid=(b, n_tiles),
            in_specs=[pl.BlockSpec((1, tm, d), row_tile),
                      pl.BlockSpec((1, 1, mod.shape[2]), lambda bi, i, pt: (bi, 0, 0)),
                      _resident((1, d), const2),
                      _resident((d, in_w), const2),
                      _resident(pool_w_b.shape, lambda bi, i, pt: (0, 0, 0)),
                      _resident((1, pw), const2),
                      _resident((tm, LANES), const2),
                      _resident((tm, LANES), const2),
                      pl.BlockSpec((1, 1, LANES), lambda bi, i, pt: (i, 0, 0)),
                      pl.BlockSpec((1, 1, LANES), lambda bi, i, pt: (i, 0, 0)),
                      tok, tok, tok, whole_cache, whole_cache],
            out_specs=[pl.BlockSpec((1, tm, aw), row_tile),
                       pl.BlockSpec((1, tm, aw), row_tile),
                       pl.BlockSpec((1, aw, tm), lambda bi, i, pt: (bi, 0, i)),
                       pl.BlockSpec((1, tm, aw), row_tile),
                       pl.BlockSpec((1, tm // MOBA_BLOCK, vt_rows, MOBA_BLOCK), lambda bi, i, pt: (bi, i, 0, 0)),
                       pl.BlockSpec((1, tm, pw), row_tile),
                       pl.BlockSpec((1, POOL_HALO, pw), lambda bi, i, pt: (bi, 0, 0)),
                       tok_out],
            scratch_shapes=[pltpu.VMEM((tm + POOL_HALO, pw), F32),
                            pltpu.VMEM((tm + POOL_HALO, pw - gc), F32),
                            pltpu.VMEM((tm + POOL_HALO, pw - 2 * gc), F32),
                            pltpu.VMEM((tm + POOL_HALO, pw - 3 * gc), F32),
                            *page_scratch]),
        out_shape=[jax.ShapeDtypeStruct((b, t, aw), F32),
                   jax.ShapeDtypeStruct((b, t, aw), F32),
                   jax.ShapeDtypeStruct((b, aw, t), BF16),
                   jax.ShapeDtypeStruct((b, t, aw), BF16),
                   jax.ShapeDtypeStruct((b, nb, vt_rows, MOBA_BLOCK), BF16),
                   jax.ShapeDtypeStruct((b, t, pw), BF16),
                   jax.ShapeDtypeStruct((b, POOL_HALO, pw), F32),
                   jax.ShapeDtypeStruct((n_seq, 1, aw), F32)],
        compiler_params=_params("arbitrary", "arbitrary"),
        name="pre_prompt",
    )(page_table, x, mod, g_pre, w_in_b, pool_w_b, pool_scale, *rope, q_s, k_s, v_s, cache_kt, cache_vt)


def _moba_kernel(pt_ref, qt_ref, kb_ref, vt_ref, qs_ref, kn_ref, vn_ref, ck_hbm, cv_hbm, o_ref, os_ref,
                 km_ref, ind_ref, q2_ref, m_ref, acc_ref, kbuf, vbuf, page_sem, *, nb, n_heads, first_seq, n_seq):
    i = pl.program_id(1)
    tq = MOBA_BLOCK
    pair_w = 2 * HEAD_DIM
    page_size = kbuf.shape[-1]
    k_pages, v_pages = _stage_pages(pt_ref, ck_hbm, cv_hbm, kbuf, vbuf, page_sem, pl.program_id(0) * nb + i,
                                    pl.num_programs(0) * nb, first_seq, n_seq)

    @pl.when(i == 0)
    def _():
        lane = lax.broadcasted_iota(jnp.int32, (MOBA_BLOCK, LANES), 1)
        for n in range(nb):
            km_ref[n:n + 1, :] = jnp.mean(kb_ref[0, n].astype(F32), axis=0, keepdims=True)
            ind_ref[n] = jnp.where(lane == n, 1.0, 0.0).astype(BF16)
        for h in range(n_heads):
            q2_ref[h, pair_w + nb:, :] = jnp.zeros((pair_w - nb, tq), BF16)

    sample_weights = _sample_weights(qs_ref[0], kn_ref[0], k_pages[0], page_size)

    blk_row = lax.broadcasted_iota(jnp.int32, (nb, tq), 0)
    blk_row_f = blk_row.astype(F32)
    fully_past = blk_row < i
    key_row = lax.broadcasted_iota(jnp.int32, (MOBA_BLOCK, tq), 0)
    qry_col = lax.broadcasted_iota(jnp.int32, (MOBA_BLOCK, tq), 1)
    causal = key_row <= qry_col
    pair_row = lax.broadcasted_iota(jnp.int32, (pair_w, tq), 0)

    for h in range(n_heads):
        lo = (h // 2) * pair_w
        mine = (pair_row >= HEAD_DIM) if h % 2 else (pair_row < HEAD_DIM)
        q2 = jnp.where(mine, qt_ref[0, lo:lo + pair_w, :], jnp.zeros((), BF16))
        q2_ref[h, 0:pair_w, :] = q2

        gate = jnp.dot(km_ref[:, lo:lo + pair_w].astype(BF16), q2, preferred_element_type=F32)
        gate = jnp.where(fully_past, gate, -jnp.inf)
        chosen = jnp.zeros((nb, tq), jnp.bool_)
        for _ in range(MOBA_TOPK):
            best = jnp.max(gate, axis=0, keepdims=True)
            first = jnp.min(jnp.where(gate == best, blk_row_f, float(nb)), axis=0, keepdims=True)
            pick = blk_row_f == first
            chosen = jnp.logical_or(chosen, pick)
            gate = jnp.where(pick, -jnp.inf, gate)
        q2_ref[h, pair_w:pair_w + nb, :] = jnp.where(jnp.logical_and(chosen, fully_past), 0.0, MASKED).astype(BF16)

    def scores(js, h):
        lo = (h // 2) * pair_w
        return [jnp.dot(jnp.concatenate([kb_ref[0, j, :, lo:lo + pair_w], ind_ref[j]], axis=1), q2_ref[h],
                        preferred_element_type=F32) for j in js]

    def own_scores(js, h):
        lo = (h // 2) * pair_w
        return [jnp.dot(kb_ref[0, j, :, lo:lo + pair_w], q2_ref[h, 0:pair_w, :], preferred_element_type=F32)
                for j in js]

    def all_heads(js, consume, scores=scores):
        items = [(js[r:r + SOFTMAX_BLOCKS], h) for r in range(0, len(js), SOFTMAX_BLOCKS) for h in range(n_heads)]
        look = min(SCORE_LOOKAHEAD, len(items))
        ahead = {k: scores(*items[k]) for k in range(look)}
        for k, (sub, h) in enumerate(items):
            if k + look < len(items):
                ahead[k + look] = scores(*items[k + look])
            consume(sub, h, ahead.pop(k))

    def values(j, h):
        return vt_ref[0, j, h * V_ROWS:(h + 1) * V_ROWS, :]

    def first_block(_, h, ss):
        s = jnp.where(causal, ss[0], -jnp.inf)
        m0 = jnp.max(s, axis=0, keepdims=True)
        m_ref[h] = m0
        acc_ref[h] = jnp.dot(values(i, h), jnp.exp2(s - m0).astype(BF16), preferred_element_type=F32)

    all_heads([i], first_block, own_scores)
    os_ref[0] = _sample_values(*sample_weights, vn_ref[0], v_pages[0], page_size)

    def online_update(js, h, ss, m_old, acc_old):
        m_new = m_old
        for s in ss:
            m_new = jnp.maximum(m_new, jnp.max(s, axis=0, keepdims=True))
        pv = None
        for j, s in zip(js, ss):
            d = jnp.dot(values(j, h), jnp.exp2(s - m_new).astype(BF16), preferred_element_type=F32)
            pv = d if pv is None else pv + d
        return m_new, jnp.exp2(m_old - m_new) * acc_old + pv

    def past_blocks(js, h, ss):
        m_ref[h], acc_ref[h] = online_update(js, h, ss, m_ref[h], acc_ref[h])

    size, top = 1, i
    while size < PAST_GROUP:
        @pl.when((i & size) != 0)
        def _(size=size, top=top):
            all_heads([top - size + r for r in range(size)], past_blocks)

        top = top - (i & size)
        size *= 2

    def group_body(t, carry):
        all_heads([PAST_GROUP * t + r for r in range(PAST_GROUP)], past_blocks)
        return carry

    lax.fori_loop(0, i // PAST_GROUP, group_body, 0)

    for h in range(n_heads):
        acc = acc_ref[h]
        o_ref[0, h * HEAD_DIM:(h + 1) * HEAD_DIM, :] = (acc[0:HEAD_DIM] / acc[HEAD_DIM:HEAD_DIM + 1]).astype(BF16)


def _moba_call(qt, kb4, vt4, page_table, q_s, k_s, v_s, cache_kt, cache_vt, *, first_seq, n_seq):
    b, aw, t = qt.shape
    nb = t // MOBA_BLOCK
    n_heads = aw // HEAD_DIM
    assert nb <= LANES and n_heads % 2 == 0
    kern = functools.partial(_moba_kernel, nb=nb, n_heads=n_heads, first_seq=first_seq, n_seq=n_seq)
    per_batch = lambda bi, i, pt: (bi, 0, 0, 0)
    q_tile = pl.BlockSpec((1, aw, MOBA_BLOCK), lambda bi, i, pt: (bi, 0, i))
    tok, whole_cache, page_scratch = _sample_specs(b * nb, lambda bi, i: bi * nb + i, first_seq, n_seq,
                                                   page_table, cache_kt)
    tok_out = pl.BlockSpec(tok.block_shape, lambda bi, i, pt: (jnp.minimum(bi * nb + i, n_seq - 1), 0, 0))
    return pl.pallas_call(
        kern,
        grid_spec=pltpu.PrefetchScalarGridSpec(
            num_scalar_prefetch=1,
            grid=(b, nb),
            in_specs=[q_tile,
                      pl.BlockSpec((1, nb, MOBA_BLOCK, aw), per_batch, pipeline_mode=pl.Buffered(1)),
                      pl.BlockSpec((1, nb, n_heads * V_ROWS, MOBA_BLOCK), per_batch, pipeline_mode=pl.Buffered(1)),
                      tok, tok, tok, whole_cache, whole_cache],
            out_specs=[q_tile, tok_out],
            scratch_shapes=[pltpu.VMEM((nb, aw), F32),
                            pltpu.VMEM((nb, MOBA_BLOCK, LANES), BF16),
                            pltpu.VMEM((n_heads, 4 * HEAD_DIM, MOBA_BLOCK), BF16),
                            pltpu.VMEM((n_heads, 1, MOBA_BLOCK), F32),
                            pltpu.VMEM((n_heads, V_ROWS, MOBA_BLOCK), F32),
                            *page_scratch]),
        out_shape=[jax.ShapeDtypeStruct((b, aw, t), BF16), jax.ShapeDtypeStruct((n_seq, 1, aw), F32)],
        compiler_params=_params("arbitrary", "arbitrary"),
        name="moba",
    )(page_table, qt, kb4, vt4, q_s, k_s, v_s, cache_kt, cache_vt)


def _post_kernel(*refs, tm, d_model, d_ff, att_transposed, per_row_state):
    (x_ref, pool_ref, att_ref, mod_ref, wop_ref, woa_ref, gpost_ref, gpre2_ref, gpost2_ref,
     wup_ref, cw_ref, cb_ref, wdn_ref) = refs[:13]
    if per_row_state:
        st_ref, y_ref, stn_ref, actbuf = refs[13:]
    else:
        y_ref, ctail_ref, carry, upbuf, actbuf = refs[13:]
        i = pl.program_id(1)

        @pl.when(i == 0)
        def _():
            carry[...] = jnp.zeros(carry.shape, F32)

    d = d_model
    x = x_ref[0]
    mod = mod_ref[0]
    gate1 = mod[:, 2 * d:3 * d]
    shift2, scale2, gate2 = mod[:, 3 * d:4 * d], mod[:, 4 * d:5 * d], mod[:, 5 * d:6 * d]

    mix = jnp.dot(pool_ref[0], wop_ref[...], preferred_element_type=F32)
    if att_transposed:
        mix += lax.dot_general(att_ref[0], woa_ref[...], (((0,), (0,)), ((), ())), preferred_element_type=F32)
    else:
        mix += jnp.dot(att_ref[0], woa_ref[...], preferred_element_type=F32)
    x1 = x + gate1 * _rms(mix, gpost_ref[...])
    h2 = (_rms(x1, gpre2_ref[...]) * (1.0 + scale2) + shift2).astype(BF16)

    fc = d_ff // FFN_CHUNKS

    def up_proj(c):
        out = []
        for base in (0, d_ff):
            cols = slice(base + c * fc, base + (c + 1) * fc)
            out.append((cols, jnp.dot(h2, wup_ref[:, cols], preferred_element_type=F32)))
        return out

    def conv_act(c, ups):
        halves = []
        for half, (cols, up) in enumerate(ups):
            if per_row_state:
                prev2, prev1 = st_ref[0, :, 0, cols], st_ref[0, :, 1, cols]
                stn_ref[0, :, 0, cols] = prev1
                stn_ref[0, :, 1, cols] = up
            else:
                buf = upbuf.at[(2 * c + half) % upbuf.shape[0]]
                buf[0:SUBLANES, :] = carry[:, cols]
                buf[SUBLANES:SUBLANES + tm, :] = up
                prev1 = buf[SUBLANES - 1:SUBLANES - 1 + tm, :]
                prev2 = buf[SUBLANES - 2:SUBLANES - 2 + tm, :]
                carry[:, cols] = buf[tm:tm + SUBLANES, :]
            halves.append(cb_ref[:, cols] + cw_ref[0:1, cols] * prev2 + cw_ref[1:2, cols] * prev1
                          + cw_ref[2:3, cols] * up)
        a, g = halves
        actbuf[:, c * fc:(c + 1) * fc] = (a * jax.nn.sigmoid(a) * g).astype(BF16)

    split_at = [((s + 1) * FFN_CHUNKS) // FFN_DOWN_SPLITS for s in range(FFN_DOWN_SPLITS)]
    ahead = {c: up_proj(c) for c in range(min(FFN_LOOKAHEAD, FFN_CHUNKS))}
    ffn, done = None, 0
    for c in range(FFN_CHUNKS):
        if c + FFN_LOOKAHEAD < FFN_CHUNKS:
            ahead[c + FFN_LOOKAHEAD] = up_proj(c + FFN_LOOKAHEAD)
        conv_act(c, ahead.pop(c))
        if c + 1 in split_at:
            rows = slice(done * fc, (c + 1) * fc)
            part = jnp.dot(actbuf[:, rows], wdn_ref[rows, :], preferred_element_type=F32)
            ffn = part if ffn is None else ffn + part
            done = c + 1
    y_ref[0] = x1 + gate2 * _rms(ffn, gpost2_ref[...])
    if not per_row_state:
        ctail_ref[0] = carry[...]


def _post_call(x, pool_o, att, mod, w, *, att_transposed, state=None):
    b, t, d = x.shape
    pw = pool_o.shape[2]
    aw = w["w_out_a"].shape[0]
    d_ff = w["w_down"].shape[0]
    per_row = state is not None
    tm = t if per_row else min(POST_TILE, t)
    assert t % tm == 0 and d_ff % (FFN_CHUNKS * LANES) == 0 and CONV_WIDTH == 3
    fc = d_ff // FFN_CHUNKS
    kern = functools.partial(_post_kernel, tm=tm, d_model=d, d_ff=d_ff, att_transposed=att_transposed,
                             per_row_state=per_row)
    row_tile = lambda bi, i: (bi, i, 0)
    const2 = lambda bi, i: (0, 0)
    mod_rows = mod.shape[1]
    in_specs = [pl.BlockSpec((1, tm, d), row_tile),
                pl.BlockSpec((1, tm, pw), row_tile),
                (pl.BlockSpec((1, aw, tm), lambda bi, i: (bi, 0, i)) if att_transposed
                 else pl.BlockSpec((1, tm, aw), row_tile)),
                pl.BlockSpec((1, mod_rows, mod.shape[2]), (row_tile if mod_rows > 1 else lambda bi, i: (bi, 0, 0))),
                _resident((pw, d), const2),
                _resident((aw, d), const2),
                _resident((1, d), const2),
                _resident((1, d), const2),
                _resident((1, d), const2),
                _resident((d, 2 * d_ff), const2),
                _resident((CONV_WIDTH, 2 * d_ff), const2),
                _resident((1, 2 * d_ff), const2),
                _resident((d_ff, d), const2)]
    args = [x, pool_o, att, mod, w["w_out_p"], w["w_out_a"], w["g_mix_post"], w["g_ffn_pre"], w["g_ffn_post"],
            w["w_up"], w["conv_w"], w["conv_b"], w["w_down"]]
    if per_row:
        state_block = pl.BlockSpec((1, tm, CONV_WIDTH - 1, 2 * d_ff), lambda bi, i: (bi, i, 0, 0))
        in_specs.append(state_block)
        args.append(state)
        out_specs = [pl.BlockSpec((1, tm, d), row_tile), state_block]
        out_shape = [jax.ShapeDtypeStruct((b, t, d), F32), jax.ShapeDtypeStruct(state.shape, F32)]
        scratch = []
    else:
        out_specs = [pl.BlockSpec((1, tm, d), row_tile),
                     pl.BlockSpec((1, SUBLANES, 2 * d_ff), lambda bi, i: (bi, 0, 0))]
        out_shape = [jax.ShapeDtypeStruct((b, t, d), F32), jax.ShapeDtypeStruct((b, SUBLANES, 2 * d_ff), F32)]
        scratch = [pltpu.VMEM((SUBLANES, 2 * d_ff), F32),
                   pltpu.VMEM((4, tm + SUBLANES, fc), F32)]
    scratch.append(pltpu.VMEM((tm, d_ff), BF16))
    return pl.pallas_call(
        kern,
        grid=(b, t // tm),
        in_specs=in_specs,
        out_specs=out_specs,
        out_shape=out_shape,
        scratch_shapes=scratch,
        compiler_params=_params("arbitrary", "arbitrary"),
        name="post_sample" if per_row else "post_prompt",
    )(*args)


def _pre_sample_kernel(x_ref, mod_ref, g_ref, win_ref, poolw_ref, pscale_ref, cos_ref, sin_ref, st_ref,
                       stn_ref, q_ref, k_ref, v_ref, pool_ref, *, d_model, pw, aw, pos0):
    gc = pw // len(POOL_WINDOWS)
    x = x_ref[...]
    mod = mod_ref[...]
    shift, scale = mod[:, 0:d_model], mod[:, d_model:2 * d_model]
    h = (_rms(x, g_ref[...]) * (1.0 + scale) + shift).astype(BF16)
    project = lambda lo, hi: jnp.dot(h, win_ref[:, lo:hi], preferred_element_type=F32)
    u = project(0, pw)
    cos, sin_signed = cos_ref[...], sin_ref[...]
    q_ref[...] = _rope(project(pw, pw + aw), cos, sin_signed)
    k_ref[...] = _rope(project(pw + aw, pw + 2 * aw), cos, sin_signed)
    v_ref[...] = project(pw + 2 * aw, pw + 3 * aw)

    pooled = []
    for g, w in enumerate(POOL_WINDOWS):
        cols = slice(g * gc, (g + 1) * gc)
        win = u[:, cols]
        for r in range(POOL_STATE - (w - 1), POOL_STATE):
            win = win + st_ref[r, :, cols]
        pooled.append(win / float(min(w, pos0 + 1)) - u[:, cols])
    pool_ref[...] = _pool_project(jnp.concatenate(pooled, axis=1), poolw_ref, pscale_ref[...]).astype(BF16)

    for r in range(POOL_STATE - 1):
        stn_ref[r] = st_ref[r + 1]
    stn_ref[POOL_STATE - 1] = u


def _pre_sample_call(x, mod, g_pre, w_in_b, pool_w_b, pool_scale, cos_t, sin_t, state_pool, *, pw, aw, pos0):
    rows, d = x.shape
    tr = SAMPLE_ROWS if rows % SAMPLE_ROWS == 0 else rows
    kern = functools.partial(_pre_sample_kernel, d_model=d, pw=pw, aw=aw, pos0=pos0)
    const2 = lambda i: (0, 0)
    seqs = lambda i: (i, 0)
    return pl.pallas_call(
        kern,
        grid=(rows // tr,),
        in_specs=[pl.BlockSpec((tr, d), seqs),
                  pl.BlockSpec((tr, 2 * d), seqs),
                  _resident((1, d), const2),
                  _resident(w_in_b.shape, const2),
                  _resident(pool_w_b.shape, lambda i: (0, 0, 0)),
                  _resident((1, pw), const2),
                  _resident((1, LANES), const2),
                  _resident((1, LANES), const2),
                  pl.BlockSpec((POOL_STATE, tr, pw), lambda i: (0, i, 0))],
        out_specs=[pl.BlockSpec((POOL_STATE, tr, pw), lambda i: (0, i, 0)),
                   pl.BlockSpec((tr, aw), seqs),
                   pl.BlockSpec((tr, aw), seqs),
                   pl.BlockSpec((tr, aw), seqs),
                   pl.BlockSpec((tr, pw), seqs)],
        out_shape=[jax.ShapeDtypeStruct((POOL_STATE, rows, pw), F32),
                   jax.ShapeDtypeStruct((rows, aw), F32),
                   jax.ShapeDtypeStruct((rows, aw), F32),
                   jax.ShapeDtypeStruct((rows, aw), F32),
                   jax.ShapeDtypeStruct((rows, pw), BF16)],
        compiler_params=_params("arbitrary"),
        name="pre_sample",
    )(x, mod, g_pre, w_in_b, pool_w_b, pool_scale, cos_t, sin_t, state_pool)


def _stage_pages(pt_ref, ck_hbm, cv_hbm, kbuf, vbuf, page_sem, step, n_steps, first_seq, n_seq):
    _, seqs_per_step, n_pages = kbuf.shape[:3]
    slot = step % 2

    def page_copies(s, into):
        copies = []
        group = jnp.minimum(s, n_seq // seqs_per_step - 1)
        for j in range(seqs_per_step):
            seq = first_seq + group * seqs_per_step + j
            for hbm, buf, which in ((ck_hbm, kbuf, 0), (cv_hbm, vbuf, 1)):
                for p in range(n_pages):
                    copies.append(pltpu.make_async_copy(hbm.at[pt_ref[seq, p]], buf.at[into, j, p],
                                                        page_sem.at[into, which]))
        return copies

    @pl.when(step == 0)
    def _():
        for copy in page_copies(step, slot):
            copy.start()

    @pl.when(step + 1 < n_steps)
    def _():
        for copy in page_copies(step + 1, 1 - slot):
            copy.start()

    for copy in page_copies(step, slot):
        copy.wait()
    return ([[kbuf.at[slot, j, p] for p in range(n_pages)] for j in range(seqs_per_step)],
            [[vbuf.at[slot, j, p] for p in range(n_pages)] for j in range(seqs_per_step)])


def _own_lanes(n_heads):
    aw = n_heads * HEAD_DIM
    head_row = lax.broadcasted_iota(jnp.int32, (n_heads, aw), 0)
    head_of_lane = lax.broadcasted_iota(jnp.int32, (n_heads, aw), 1) // HEAD_DIM
    return head_row == head_of_lane


def _sample_weights(q, k_new, k_refs, page_size):
    n_pages = len(k_refs)
    n_heads = k_refs[0].shape[0]
    aw = n_heads * HEAD_DIM
    ppb = MOBA_BLOCK // page_size
    n_blocks = n_pages // ppb

    qbd = jnp.where(_own_lanes(n_heads), q * (HEAD_DIM ** -0.5 * LOG2E), 0.0)
    qbd_b = qbd.astype(BF16)

    scores = []
    for n in range(n_blocks):
        kt = jnp.concatenate([k_refs[p][...].reshape(aw, page_size).astype(BF16)
                              for p in range(n * ppb, (n + 1) * ppb)], axis=1)
        s = jnp.dot(qbd_b, kt, preferred_element_type=F32)
        scores += [s[:, r * page_size:(r + 1) * page_size] for r in range(ppb)]

    pages_of = lambda n: range(n * ppb, (n + 1) * ppb)
    gates = [sum(jnp.sum(scores[p], axis=-1, keepdims=True) for p in pages_of(n)) for n in range(n_blocks)]
    peaks = [functools.reduce(jnp.maximum, [jnp.max(scores[p], axis=-1, keepdims=True) for p in pages_of(n)])
             for n in range(n_blocks)]
    chosen = []
    for n in range(n_blocks):
        beaten = jnp.zeros((n_heads, 1), jnp.int32)
        for m in range(n_blocks):
            if m != n:
                wins = (gates[m] >= gates[n]) if m < n else (gates[m] > gates[n])
                beaten = beaten + wins.astype(jnp.int32)
        chosen.append(beaten < MOBA_TOPK)

    s_new = jnp.sum(qbd * k_new, axis=-1, keepdims=True)
    m = s_new
    for n in range(n_blocks):
        m = jnp.maximum(m, jnp.where(chosen[n], peaks[n], -jnp.inf))
    p_new = jnp.exp2(s_new - m)
    weights = [jnp.exp2(jnp.where(chosen[p // ppb], scores[p], -jnp.inf) - m) for p in range(n_pages)]
    inv_l = 1.0 / (p_new + sum(jnp.sum(w, axis=-1, keepdims=True) for w in weights))
    return weights, p_new, inv_l


def _sample_values(weights, p_new, inv_l, v_new, v_refs, page_size):
    n_pages = len(v_refs)
    n_heads = v_refs[0].shape[0]
    partial = []
    for h in range(n_heads):
        acc = None
        for p in range(n_pages):
            term = v_refs[p][h] * weights[p][h:h + 1, :]
            acc = term if acc is None else acc + term
        partial.append(acc)
    partial = jnp.concatenate(partial, axis=0)
    ones = jnp.ones((SUBLANES, page_size), BF16)
    high = partial.astype(BF16)
    low = (partial - high.astype(F32)).astype(BF16)
    contract_lanes = (((1,), (1,)), ((), ()))
    past = (lax.dot_general(ones, high, contract_lanes, preferred_element_type=F32)
            + lax.dot_general(ones, low, contract_lanes, preferred_element_type=F32))[0:1, :]
    own_lanes = _own_lanes(n_heads)
    inv_row = jnp.sum(jnp.where(own_lanes, inv_l, 0.0), axis=0, keepdims=True)
    own_row = jnp.sum(jnp.where(own_lanes, p_new, 0.0), axis=0, keepdims=True)
    return (past + own_row * v_new) * inv_row


def _rope_tables(positions):
    half = HEAD_DIM // 2
    lane = jnp.arange(LANES, dtype=jnp.int32)
    inv = ROPE_THETA ** (-(lane % half).astype(F32) / half)
    sign = jnp.where((lane % HEAD_DIM) < half, -1.0, 1.0).astype(F32)
    ang = positions.astype(F32)[:, None] * inv[None, :]
    return jnp.cos(ang), jnp.sin(ang) * sign[None, :]


def kernel(x_prompt, x_sample, c_prompt, c_sample, cache_k, cache_v, page_table, state_pool, state_conv, w_ada, b_ada, g_mix_pre, g_mix_post, w_in, pool_w, pool_scale, w_out, g_ffn_pre, g_ffn_post, w_up, conv_w, conv_b, w_down):
    depth = w_ada.shape[0]
    b, t, d = x_prompt.shape
    n_seq, dec_t, _ = x_sample.shape
    n_pages = page_table.shape[1]
    page_size, n_heads = cache_k.shape[2], cache_k.shape[3]
    past_len = n_pages * page_size
    aw = n_heads * HEAD_DIM
    pw = state_pool.shape[3]
    d_ff = w_down.shape[1]
    assert dec_t == 1 and t % MOBA_BLOCK == 0 and t % PRE_TILE == 0 and pw == d - aw
    assert (pw // len(POOL_WINDOWS)) % LANES == 0 and w_in.shape[2] == pw + 3 * aw

    tm = min(PRE_TILE, t)
    cos_r, sin_r = _rope_tables(jnp.arange(tm, dtype=jnp.int32))
    cos_t, sin_t = _rope_tables(tm * jnp.arange(t // tm, dtype=jnp.int32))
    rope_p = (cos_r, sin_r, cos_t[:, None, :], sin_t[:, None, :])
    cos_s, sin_s = _rope_tables(past_len + jnp.arange(dec_t, dtype=jnp.int32))

    y_p, y_s = x_prompt, x_sample.reshape(1, n_seq, d)
    outs = [[] for _ in range(8)]
    for l in range(depth):
        w = dict(w_out_p=w_out[l, :pw].astype(BF16), w_out_a=w_out[l, pw:].astype(BF16),
                 g_mix_post=g_mix_post[l].reshape(1, d), g_ffn_pre=g_ffn_pre[l].reshape(1, d),
                 g_ffn_post=g_ffn_post[l].reshape(1, d), w_up=w_up[l].astype(BF16), conv_w=conv_w[l],
                 conv_b=conv_b[l].reshape(1, 2 * d_ff), w_down=w_down[l].astype(BF16))
        g_pre = g_mix_pre[l].reshape(1, d)
        w_in_b = w_in[l].astype(BF16)
        pool_w_b = pool_w[l].astype(BF16)
        pscale = pool_scale[l].reshape(1, pw)

        mod_p, mod_s = _mod_call(c_prompt, c_sample, w_ada[l], b_ada[l])
        mod_p, mod_s = mod_p.reshape(b, 1, 6 * d), mod_s.reshape(1, n_seq, 6 * d)

        pool_state_s, q_s, k_s, v_s, pool_s = _pre_sample_call(
            y_s[0], mod_s[0], g_pre, w_in_b, pool_w_b, pscale, cos_s, sin_s, jnp.swapaxes(state_pool[l], 0, 1),
            pw=pw, aw=aw, pos0=past_len)
        tok = lambda a: a.reshape(n_seq, 1, aw)
        sample = (page_table, tok(q_s), tok(k_s), tok(v_s),
                  jnp.transpose(cache_k[l], (0, 2, 3, 1)), jnp.transpose(cache_v[l], (0, 2, 3, 1)))
        n_early = max(1, min(n_seq - 1, int(n_seq * SAMPLE_EARLY_SHARE)))
        k_p, v_p, qt, kb, vt4, pool_p, utail, att_s0 = _pre_prompt_call(
            y_p, mod_p, g_pre, w_in_b, pool_w_b, pscale, rope_p, *sample, pw=pw, aw=aw, first_seq=0, n_seq=n_early)
        att_t, att_s1 = _moba_call(qt, kb.reshape(b, t // MOBA_BLOCK, MOBA_BLOCK, aw), vt4, *sample,
                                   first_seq=n_early, n_seq=n_seq - n_early)
        att_s = jnp.concatenate([att_s0, att_s1], axis=0)
        y_p, ctail = _post_call(y_p, pool_p, att_t, mod_p, w, att_transposed=True)
        y_s, conv_state_s = _post_call(y_s, pool_s.reshape(1, n_seq, pw), att_s.reshape(1, n_seq, aw).astype(BF16),
                                       mod_s, w, att_transposed=False, state=state_conv[l][None])

        outs[0].append(k_p.reshape(b, t, n_heads, HEAD_DIM))
        outs[1].append(v_p.reshape(b, t, n_heads, HEAD_DIM))
        outs[2].append(utail[:, POOL_HALO - POOL_STATE:])
        outs[3].append(ctail[:, SUBLANES - (CONV_WIDTH - 1):])
        outs[4].append(k_s.reshape(n_seq, dec_t, n_heads, HEAD_DIM))
        outs[5].append(v_s.reshape(n_seq, dec_t, n_heads, HEAD_DIM))
        outs[6].append(jnp.swapaxes(pool_state_s, 0, 1))
        outs[7].append(conv_state_s[0])
    stacked = [jnp.stack(o) for o in outs]
    return (y_p, y_s.reshape(n_seq, dec_t, d), *stacked)
```

```python
import functools
import math

import jax
import jax.numpy as jnp
from jax import lax
from jax.experimental import pallas as pl
from jax.experimental.pallas import tpu as pltpu

HEAD_DIM = 64
MOBA_BLOCK = 256
MOBA_TOPK = 3
POOL_WINDOWS = (2, 4, 8, 16)
POOL_STATE = max(POOL_WINDOWS) - 1
CONV_WIDTH = 3
ROPE_THETA = 10000.0
RMS_EPS = 1e-6

LANES = 128
SUBLANES = 8
VMEM_LIMIT_BYTES = 56 * 1024 * 1024

BF16_ROWS = 16
V_ROWS = HEAD_DIM + BF16_ROWS

POOL_HALO = 16
PRE_TILE = 256
SAMPLE_EARLY_SHARE = 0.5
POST_TILE = 512
SAMPLE_ROWS = 32
FFN_CHUNKS = 11
FFN_LOOKAHEAD = 2
FFN_DOWN_SPLITS = 2
PAST_GROUP = 8
SOFTMAX_BLOCKS = 1
SCORE_LOOKAHEAD = 5

F32 = jnp.float32
BF16 = jnp.bfloat16
LOG2E = math.log2(math.e)
MASKED = -1e30


def _params(*semantics):
    return pltpu.CompilerParams(dimension_semantics=semantics, vmem_limit_bytes=VMEM_LIMIT_BYTES)


def _resident(shape, index_map):
    return pl.BlockSpec(shape, index_map, pipeline_mode=pl.Buffered(1))


def _rms(x, g):
    ms = jnp.mean(x * x, axis=-1, keepdims=True)
    return x * lax.rsqrt(ms + RMS_EPS) * g


def _rope(x, cos, sin_signed):
    lane = lax.broadcasted_iota(jnp.int32, (1, LANES), 1)
    first_half = (lane % HEAD_DIM) < HEAD_DIM // 2
    half = HEAD_DIM // 2
    outs = []
    for c in range(x.shape[1] // LANES):
        blk = x[:, c * LANES:(c + 1) * LANES]
        partner = jnp.where(first_half, pltpu.roll(blk, LANES - half, 1), pltpu.roll(blk, half, 1))
        outs.append(blk * cos + partner * sin_signed)
    return jnp.concatenate(outs, axis=1)


def _pool_project(pooled, poolw_ref, pscale):
    n_groups, gc, _ = poolw_ref.shape
    outs = [jnp.dot(pooled[:, g * gc:(g + 1) * gc].astype(BF16), poolw_ref[g], preferred_element_type=F32)
            for g in range(n_groups)]
    return jnp.concatenate(outs, axis=1) * pscale


def _mod_kernel(cp_ref, cs_ref, wa_ref, wb_ref, b_ref, op_ref, os_ref):
    half = wa_ref.shape[1]
    acts = [(c_ref[...] * jax.nn.sigmoid(c_ref[...])).astype(BF16) for c_ref in (cp_ref, cs_ref)]
    for r, w_ref in enumerate((wa_ref, wb_ref)):
        cols = slice(r * half, (r + 1) * half)
        w = w_ref[...].astype(BF16)
        for a, o_ref in zip(acts, (op_ref, os_ref)):
            o_ref[:, cols] = jnp.dot(a, w, preferred_element_type=F32) + b_ref[:, cols]


def _mod_call(c_prompt, c_sample, w_ada, b_ada):
    d, n_out = w_ada.shape
    rows_p, rows_s = c_prompt.shape[0], c_sample.shape[0]
    full = lambda j: (0, 0)
    col = lambda j: (0, j)
    halves = [pl.BlockSpec((d, d // 2), lambda j, r=r: (0, 2 * j + r)) for r in range(2)]
    return pl.pallas_call(
        _mod_kernel,
        grid=(n_out // d,),
        in_specs=[pl.BlockSpec((rows_p, d), full), pl.BlockSpec((rows_s, d), full), *halves,
                  pl.BlockSpec((1, d), col)],
        out_specs=[pl.BlockSpec((rows_p, d), col), pl.BlockSpec((rows_s, d), col)],
        out_shape=[jax.ShapeDtypeStruct((rows_p, n_out), F32), jax.ShapeDtypeStruct((rows_s, n_out), F32)],
        compiler_params=_params("arbitrary"),
        name="adaln_mod",
    )(c_prompt, c_sample, w_ada, w_ada, b_ada.reshape(1, n_out))


def _pre_prompt_kernel(pt_ref, x_ref, mod_ref, g_ref, win_ref, poolw_ref, pscale_ref, cos_ref, sin_ref, cost_ref,
                       sint_ref, qs_ref, kn_ref, vn_ref, ck_hbm, cv_hbm,
                       k_ref, v_ref, qt_ref, kb_ref, vt_ref, pool_ref, utail_ref, os_ref,
                       ubuf, s2buf, s4buf, s8buf, kbuf, vbuf, page_sem, *, tm, d_model, pw, aw, q_scale, first_seq,
                       n_seq):
    i = pl.program_id(1)
    halo = POOL_HALO
    gc = pw // len(POOL_WINDOWS)
    bufs = (ubuf, s2buf, s4buf, s8buf)
    page_size = kbuf.shape[-1]
    step = pl.program_id(0) * pl.num_programs(1) + i
    k_pages, v_pages = _stage_pages(pt_ref, ck_hbm, cv_hbm, kbuf, vbuf, page_sem, step,
                                    pl.num_programs(0) * pl.num_programs(1), first_seq, n_seq)
    seqs_per_step = len(k_pages)

    @pl.when(i == 0)
    def _():
        for buf in bufs:
            buf[0:halo, :] = jnp.zeros((halo, buf.shape[1]), F32)

    sample_weights = [_sample_weights(qs_ref[j], kn_ref[j], k_pages[j], page_size) for j in range(seqs_per_step)]

    x = x_ref[0]
    mod = mod_ref[0]
    shift, scale = mod[:, 0:d_model], mod[:, d_model:2 * d_model]
    h = _rms(x, g_ref[...]) * (1.0 + scale) + shift
    proj = jnp.dot(h.astype(BF16), win_ref[...], preferred_element_type=F32)
    u = proj[:, 0:pw]
    cos_t, sin_t, cos_r, sin_r = cost_ref[0], sint_ref[0], cos_ref[...], sin_ref[...]
    cos = cos_t * cos_r - sin_t * sin_r
    sin_signed = sin_t * cos_r + cos_t * sin_r
    q = _rope(proj[:, pw:pw + aw], cos, sin_signed)
    k = _rope(proj[:, pw + aw:pw + 2 * aw], cos, sin_signed)
    v = proj[:, pw + 2 * aw:pw + 3 * aw]

    k_ref[0] = k
    v_ref[0] = v
    kb_ref[0] = k.astype(BF16)
    qt_ref[0] = (q * q_scale).T.astype(BF16)
    ones_pad = jnp.where(lax.broadcasted_iota(jnp.int32, (BF16_ROWS, MOBA_BLOCK), 0) == 0, 1.0, 0.0).astype(BF16)
    for r in range(tm // MOBA_BLOCK):
        vt = v[r * MOBA_BLOCK:(r + 1) * MOBA_BLOCK, :].T.astype(BF16)
        for hd in range(aw // HEAD_DIM):
            vt_ref[0, r, hd * V_ROWS:hd * V_ROWS + HEAD_DIM, :] = vt[hd * HEAD_DIM:(hd + 1) * HEAD_DIM, :]
            vt_ref[0, r, hd * V_ROWS + HEAD_DIM:(hd + 1) * V_ROWS, :] = ones_pad

    ubuf[halo:halo + tm, :] = u
    s2 = u + ubuf[halo - 1:halo - 1 + tm, :]
    s2buf[halo:halo + tm, :] = s2[:, gc:]
    s4 = s2[:, gc:] + s2buf[halo - 2:halo - 2 + tm, :]
    s4buf[halo:halo + tm, :] = s4[:, gc:]
    s8 = s4[:, gc:] + s4buf[halo - 4:halo - 4 + tm, :]
    s8buf[halo:halo + tm, :] = s8[:, gc:]
    s16 = s8[:, gc:] + s8buf[halo - 8:halo - 8 + tm, :]
    wins = (s2[:, 0:gc], s4[:, 0:gc], s8[:, 0:gc], s16)
    pos1 = (i * tm + 1 + lax.broadcasted_iota(jnp.int32, (tm, 1), 0)).astype(F32)
    pooled = jnp.concatenate(
        [wins[g] / jnp.minimum(float(w), pos1) - u[:, g * gc:(g + 1) * gc] for g, w in enumerate(POOL_WINDOWS)],
        axis=1)
    pool_ref[0] = _pool_project(pooled, poolw_ref, pscale_ref[...]).astype(BF16)

    for buf in bufs:
        buf[0:halo, :] = buf[tm:tm + halo, :]
    utail_ref[0] = ubuf[0:halo, :]

    for j in range(seqs_per_step):
        os_ref[j] = _sample_values(*sample_weights[j], vn_ref[j], v_pages[j], page_size)


def _sample_specs(n_steps, step_of, first_seq, n_seq, page_table, cache_kt):
    _, n_pages = page_table.shape
    _, n_heads, dh, page_size = cache_kt.shape
    assert dh == HEAD_DIM and MOBA_BLOCK % page_size == 0 and (n_pages * page_size) % MOBA_BLOCK == 0
    assert 0 < n_seq <= n_steps
    tok = pl.BlockSpec((1, 1, n_heads * dh),
                       lambda *idx: (first_seq + jnp.minimum(step_of(*idx[:-1]), n_seq - 1), 0, 0))
    whole_cache = pl.BlockSpec(memory_space=pl.ANY)
    page_slots = pltpu.VMEM((2, 1, n_pages, n_heads, dh, page_size), F32)
    return tok, whole_cache, [page_slots, page_slots, pltpu.SemaphoreType.DMA((2, 2))]


def _pre_prompt_call(x, mod, g_pre, w_in_b, pool_w_b, pool_scale, rope, page_table, q_s, k_s, v_s, cache_kt, cache_vt,
                     *, pw, aw, first_seq, n_seq):
    b, t, d = x.shape
    tm = min(PRE_TILE, t)
    n_tiles = t // tm
    nb = t // MOBA_BLOCK
    gc = pw // len(POOL_WINDOWS)
    in_w = w_in_b.shape[1]
    vt_rows = (aw // HEAD_DIM) * V_ROWS
    kern = functools.partial(_pre_prompt_kernel, tm=tm, d_model=d, pw=pw, aw=aw,
                             q_scale=HEAD_DIM ** -0.5 * LOG2E, first_seq=first_seq, n_seq=n_seq)
    row_tile = lambda bi, i, pt: (bi, i, 0)
    const2 = lambda bi, i, pt: (0, 0)
    tok, whole_cache, page_scratch = _sample_specs(b * n_tiles, lambda bi, i: bi * n_tiles + i, first_seq, n_seq,
                                                   page_table, cache_kt)
    tok_out = pl.BlockSpec(tok.block_shape, lambda bi, i, pt: (jnp.minimum(bi * n_tiles + i, n_seq - 1), 0, 0))
    return pl.pallas_call(
        kern,
        grid_spec=pltpu.PrefetchScalarGridSpec(
            num_scalar_prefetch=1,
            grid=(b, n_tiles),
            in_specs=[pl.BlockSpec((1, tm, d), row_tile),
                      pl.BlockSpec((1, 1, mod.shape[2]), lambda bi, i, pt: (bi, 0, 0)),
                      _resident((1, d), const2),
                      _resident((d, in_w), const2),
                      _resident(pool_w_b.shape, lambda bi, i, pt: (0, 0, 0)),
                      _resident((1, pw), const2),
                      _resident((tm, LANES), const2),
                      _resident((tm, LANES), const2),
                      pl.BlockSpec((1, 1, LANES), lambda bi, i, pt: (i, 0, 0)),
                      pl.BlockSpec((1, 1, LANES), lambda bi, i, pt: (i, 0, 0)),
                      tok, tok, tok, whole_cache, whole_cache],
            out_specs=[pl.BlockSpec((1, tm, aw), row_tile),
                       pl.BlockSpec((1, tm, aw), row_tile),
                       pl.BlockSpec((1, aw, tm), lambda bi, i, pt: (bi, 0, i)),
                       pl.BlockSpec((1, tm, aw), row_tile),
                       pl.BlockSpec((1, tm // MOBA_BLOCK, vt_rows, MOBA_BLOCK), lambda bi, i, pt: (bi, i, 0, 0)),
                       pl.BlockSpec((1, tm, pw), row_tile),
                       pl.BlockSpec((1, POOL_HALO, pw), lambda bi, i, pt: (bi, 0, 0)),
                       tok_out],
            scratch_shapes=[pltpu.VMEM((tm + POOL_HALO, pw), F32),
                            pltpu.VMEM((tm + POOL_HALO, pw - gc), F32),
                            pltpu.VMEM((tm + POOL_HALO, pw - 2 * gc), F32),
                            pltpu.VMEM((tm + POOL_HALO, pw - 3 * gc), F32),
                            *page_scratch]),
        out_shape=[jax.ShapeDtypeStruct((b, t, aw), F32),
                   jax.ShapeDtypeStruct((b, t, aw), F32),
                   jax.ShapeDtypeStruct((b, aw, t), BF16),
                   jax.ShapeDtypeStruct((b, t, aw), BF16),
                   jax.ShapeDtypeStruct((b, nb, vt_rows, MOBA_BLOCK), BF16),
                   jax.ShapeDtypeStruct((b, t, pw), BF16),
                   jax.ShapeDtypeStruct((b, POOL_HALO, pw), F32),
                   jax.ShapeDtypeStruct((n_seq, 1, aw), F32)],
        compiler_params=_params("arbitrary", "arbitrary"),
        name="pre_prompt",
    )(page_table, x, mod, g_pre, w_in_b, pool_w_b, pool_scale, *rope, q_s, k_s, v_s, cache_kt, cache_vt)


def _moba_kernel(pt_ref, qt_ref, kb_ref, vt_ref, qs_ref, kn_ref, vn_ref, ck_hbm, cv_hbm, o_ref, os_ref,
                 km_ref, ind_ref, q2_ref, m_ref, acc_ref, kbuf, vbuf, page_sem, *, nb, n_heads, first_seq, n_seq):
    i = pl.program_id(1)
    tq = MOBA_BLOCK
    pair_w = 2 * HEAD_DIM
    page_size = kbuf.shape[-1]
    k_pages, v_pages = _stage_pages(pt_ref, ck_hbm, cv_hbm, kbuf, vbuf, page_sem, pl.program_id(0) * nb + i,
                                    pl.num_programs(0) * nb, first_seq, n_seq)

    @pl.when(i == 0)
    def _():
        lane = lax.broadcasted_iota(jnp.int32, (MOBA_BLOCK, LANES), 1)
        for n in range(nb):
            km_ref[n:n + 1, :] = jnp.mean(kb_ref[0, n].astype(F32), axis=0, keepdims=True)
            ind_ref[n] = jnp.where(lane == n, 1.0, 0.0).astype(BF16)
        for h in range(n_heads):
            q2_ref[h, pair_w + nb:, :] = jnp.zeros((pair_w - nb, tq), BF16)

    sample_weights = _sample_weights(qs_ref[0], kn_ref[0], k_pages[0], page_size)

    blk_row = lax.broadcasted_iota(jnp.int32, (nb, tq), 0)
    blk_row_f = blk_row.astype(F32)
    fully_past = blk_row < i
    key_row = lax.broadcasted_iota(jnp.int32, (MOBA_BLOCK, tq), 0)
    qry_col = lax.broadcasted_iota(jnp.int32, (MOBA_BLOCK, tq), 1)
    causal = key_row <= qry_col
    pair_row = lax.broadcasted_iota(jnp.int32, (pair_w, tq), 0)

    for h in range(n_heads):
        lo = (h // 2) * pair_w
        mine = (pair_row >= HEAD_DIM) if h % 2 else (pair_row < HEAD_DIM)
        q2 = jnp.where(mine, qt_ref[0, lo:lo + pair_w, :], jnp.zeros((), BF16))
        q2_ref[h, 0:pair_w, :] = q2

        gate = jnp.dot(km_ref[:, lo:lo + pair_w].astype(BF16), q2, preferred_element_type=F32)
        gate = jnp.where(fully_past, gate, -jnp.inf)
        chosen = jnp.zeros((nb, tq), jnp.bool_)
        for _ in range(MOBA_TOPK):
            best = jnp.max(gate, axis=0, keepdims=True)
            first = jnp.min(jnp.where(gate == best, blk_row_f, float(nb)), axis=0, keepdims=True)
            pick = blk_row_f == first
            chosen = jnp.logical_or(chosen, pick)
            gate = jnp.where(pick, -jnp.inf, gate)
        q2_ref[h, pair_w:pair_w + nb, :] = jnp.where(jnp.logical_and(chosen, fully_past), 0.0, MASKED).astype(BF16)

    def scores(js, h):
        lo = (h // 2) * pair_w
        return [jnp.dot(jnp.concatenate([kb_ref[0, j, :, lo:lo + pair_w], ind_ref[j]], axis=1), q2_ref[h],
                        preferred_element_type=F32) for j in js]

    def own_scores(js, h):
        lo = (h // 2) * pair_w
        return [jnp.dot(kb_ref[0, j, :, lo:lo + pair_w], q2_ref[h, 0:pair_w, :], preferred_element_type=F32)
                for j in js]

    def all_heads(js, consume, scores=scores):
        items = [(js[r:r + SOFTMAX_BLOCKS], h) for r in range(0, len(js), SOFTMAX_BLOCKS) for h in range(n_heads)]
        look = min(SCORE_LOOKAHEAD, len(items))
        ahead = {k: scores(*items[k]) for k in range(look)}
        for k, (sub, h) in enumerate(items):
            if k + look < len(items):
                ahead[k + look] = scores(*items[k + look])
            consume(sub, h, ahead.pop(k))

    def values(j, h):
        return vt_ref[0, j, h * V_ROWS:(h + 1) * V_ROWS, :]

    def first_block(_, h, ss):
        s = jnp.where(causal, ss[0], -jnp.inf)
        m0 = jnp.max(s, axis=0, keepdims=True)
        m_ref[h] = m0
        acc_ref[h] = jnp.dot(values(i, h), jnp.exp2(s - m0).astype(BF16), preferred_element_type=F32)

    all_heads([i], first_block, own_scores)
    os_ref[0] = _sample_values(*sample_weights, vn_ref[0], v_pages[0], page_size)

    def online_update(js, h, ss, m_old, acc_old):
        m_new = m_old
        for s in ss:
            m_new = jnp.maximum(m_new, jnp.max(s, axis=0, keepdims=True))
        pv = None
        for j, s in zip(js, ss):
            d = jnp.dot(values(j, h), jnp.exp2(s - m_new).astype(BF16), preferred_element_type=F32)
            pv = d if pv is None else pv + d
        return m_new, jnp.exp2(m_old - m_new) * acc_old + pv

    def past_blocks(js, h, ss):
        m_ref[h], acc_ref[h] = online_update(js, h, ss, m_ref[h], acc_ref[h])

    size, top = 1, i
    while size < PAST_GROUP:
        @pl.when((i & size) != 0)
        def _(size=size, top=top):
            all_heads([top - size + r for r in range(size)], past_blocks)

        top = top - (i & size)
        size *= 2

    def group_body(t, carry):
        all_heads([PAST_GROUP * t + r for r in range(PAST_GROUP)], past_blocks)
        return carry

    lax.fori_loop(0, i // PAST_GROUP, group_body, 0)

    for h in range(n_heads):
        acc = acc_ref[h]
        o_ref[0, h * HEAD_DIM:(h + 1) * HEAD_DIM, :] = (acc[0:HEAD_DIM] / acc[HEAD_DIM:HEAD_DIM + 1]).astype(BF16)


def _moba_call(qt, kb4, vt4, page_table, q_s, k_s, v_s, cache_kt, cache_vt, *, first_seq, n_seq):
    b, aw, t = qt.shape
    nb = t // MOBA_BLOCK
    n_heads = aw // HEAD_DIM
    assert nb <= LANES and n_heads % 2 == 0
    kern = functools.partial(_moba_kernel, nb=nb, n_heads=n_heads, first_seq=first_seq, n_seq=n_seq)
    per_batch = lambda bi, i, pt: (bi, 0, 0, 0)
    q_tile = pl.BlockSpec((1, aw, MOBA_BLOCK), lambda bi, i, pt: (bi, 0, i))
    tok, whole_cache, page_scratch = _sample_specs(b * nb, lambda bi, i: bi * nb + i, first_seq, n_seq,
                                                   page_table, cache_kt)
    tok_out = pl.BlockSpec(tok.block_shape, lambda bi, i, pt: (jnp.minimum(bi * nb + i, n_seq - 1), 0, 0))
    return pl.pallas_call(
        kern,
        grid_spec=pltpu.PrefetchScalarGridSpec(
            num_scalar_prefetch=1,
            grid=(b, nb),
            in_specs=[q_tile,
                      pl.BlockSpec((1, nb, MOBA_BLOCK, aw), per_batch, pipeline_mode=pl.Buffered(1)),
                      pl.BlockSpec((1, nb, n_heads * V_ROWS, MOBA_BLOCK), per_batch, pipeline_mode=pl.Buffered(1)),
                      tok, tok, tok, whole_cache, whole_cache],
            out_specs=[q_tile, tok_out],
            scratch_shapes=[pltpu.VMEM((nb, aw), F32),
                            pltpu.VMEM((nb, MOBA_BLOCK, LANES), BF16),
                            pltpu.VMEM((n_heads, 4 * HEAD_DIM, MOBA_BLOCK), BF16),
                            pltpu.VMEM((n_heads, 1, MOBA_BLOCK), F32),
                            pltpu.VMEM((n_heads, V_ROWS, MOBA_BLOCK), F32),
                            *page_scratch]),
        out_shape=[jax.ShapeDtypeStruct((b, aw, t), BF16), jax.ShapeDtypeStruct((n_seq, 1, aw), F32)],
        compiler_params=_params("arbitrary", "arbitrary"),
        name="moba",
    )(page_table, qt, kb4, vt4, q_s, k_s, v_s, cache_kt, cache_vt)


def _post_kernel(*refs, tm, d_model, d_ff, att_transposed, per_row_state):
    (x_ref, pool_ref, att_ref, mod_ref, wop_ref, woa_ref, gpost_ref, gpre2_ref, gpost2_ref,
     wup_ref, cw_ref, cb_ref, wdn_ref) = refs[:13]
    if per_row_state:
        st_ref, y_ref, stn_ref, actbuf = refs[13:]
    else:
        y_ref, ctail_ref, carry, upbuf, actbuf = refs[13:]
        i = pl.program_id(1)

        @pl.when(i == 0)
        def _():
            carry[...] = jnp.zeros(carry.shape, F32)

    d = d_model
    x = x_ref[0]
    mod = mod_ref[0]
    gate1 = mod[:, 2 * d:3 * d]
    shift2, scale2, gate2 = mod[:, 3 * d:4 * d], mod[:, 4 * d:5 * d], mod[:, 5 * d:6 * d]

    mix = jnp.dot(pool_ref[0], wop_ref[...], preferred_element_type=F32)
    if att_transposed:
        mix += lax.dot_general(att_ref[0], woa_ref[...], (((0,), (0,)), ((), ())), preferred_element_type=F32)
    else:
        mix += jnp.dot(att_ref[0], woa_ref[...], preferred_element_type=F32)
    x1 = x + gate1 * _rms(mix, gpost_ref[...])
    h2 = (_rms(x1, gpre2_ref[...]) * (1.0 + scale2) + shift2).astype(BF16)

    fc = d_ff // FFN_CHUNKS

    def up_proj(c):
        out = []
        for base in (0, d_ff):
            cols = slice(base + c * fc, base + (c + 1) * fc)
            out.append((cols, jnp.dot(h2, wup_ref[:, cols], preferred_element_type=F32)))
        return out

    def conv_act(c, ups):
        halves = []
        for half, (cols, up) in enumerate(ups):
            if per_row_state:
                prev2, prev1 = st_ref[0, :, 0, cols], st_ref[0, :, 1, cols]
                stn_ref[0, :, 0, cols] = prev1
                stn_ref[0, :, 1, cols] = up
            else:
                buf = upbuf.at[(2 * c + half) % upbuf.shape[0]]
                buf[0:SUBLANES, :] = carry[:, cols]
                buf[SUBLANES:SUBLANES + tm, :] = up
                prev1 = buf[SUBLANES - 1:SUBLANES - 1 + tm, :]
                prev2 = buf[SUBLANES - 2:SUBLANES - 2 + tm, :]
                carry[:, cols] = buf[tm:tm + SUBLANES, :]
            halves.append(cb_ref[:, cols] + cw_ref[0:1, cols] * prev2 + cw_ref[1:2, cols] * prev1
                          + cw_ref[2:3, cols] * up)
        a, g = halves
        actbuf[:, c * fc:(c + 1) * fc] = (a * jax.nn.sigmoid(a) * g).astype(BF16)

    split_at = [((s + 1) * FFN_CHUNKS) // FFN_DOWN_SPLITS for s in range(FFN_DOWN_SPLITS)]
    ahead = {c: up_proj(c) for c in range(min(FFN_LOOKAHEAD, FFN_CHUNKS))}
    ffn, done = None, 0
    for c in range(FFN_CHUNKS):
        if c + FFN_LOOKAHEAD < FFN_CHUNKS:
            ahead[c + FFN_LOOKAHEAD] = up_proj(c + FFN_LOOKAHEAD)
        conv_act(c, ahead.pop(c))
        if c + 1 in split_at:
            rows = slice(done * fc, (c + 1) * fc)
            part = jnp.dot(actbuf[:, rows], wdn_ref[rows, :], preferred_element_type=F32)
            ffn = part if ffn is None else ffn + part
            done = c + 1
    y_ref[0] = x1 + gate2 * _rms(ffn, gpost2_ref[...])
    if not per_row_state:
        ctail_ref[0] = carry[...]


def _post_call(x, pool_o, att, mod, w, *, att_transposed, state=None):
    b, t, d = x.shape
    pw = pool_o.shape[2]
    aw = w["w_out_a"].shape[0]
    d_ff = w["w_down"].shape[0]
    per_row = state is not None
    tm = t if per_row else min(POST_TILE, t)
    assert t % tm == 0 and d_ff % (FFN_CHUNKS * LANES) == 0 and CONV_WIDTH == 3
    fc = d_ff // FFN_CHUNKS
    kern = functools.partial(_post_kernel, tm=tm, d_model=d, d_ff=d_ff, att_transposed=att_transposed,
                             per_row_state=per_row)
    row_tile = lambda bi, i: (bi, i, 0)
    const2 = lambda bi, i: (0, 0)
    mod_rows = mod.shape[1]
    in_specs = [pl.BlockSpec((1, tm, d), row_tile),
                pl.BlockSpec((1, tm, pw), row_tile),
                (pl.BlockSpec((1, aw, tm), lambda bi, i: (bi, 0, i)) if att_transposed
                 else pl.BlockSpec((1, tm, aw), row_tile)),
                pl.BlockSpec((1, mod_rows, mod.shape[2]), (row_tile if mod_rows > 1 else lambda bi, i: (bi, 0, 0))),
                _resident((pw, d), const2),
                _resident((aw, d), const2),
                _resident((1, d), const2),
                _resident((1, d), const2),
                _resident((1, d), const2),
                _resident((d, 2 * d_ff), const2),
                _resident((CONV_WIDTH, 2 * d_ff), const2),
                _resident((1, 2 * d_ff), const2),
                _resident((d_ff, d), const2)]
    args = [x, pool_o, att, mod, w["w_out_p"], w["w_out_a"], w["g_mix_post"], w["g_ffn_pre"], w["g_ffn_post"],
            w["w_up"], w["conv_w"], w["conv_b"], w["w_down"]]
    if per_row:
        state_block = pl.BlockSpec((1, tm, CONV_WIDTH - 1, 2 * d_ff), lambda bi, i: (bi, i, 0, 0))
        in_specs.append(state_block)
        args.append(state)
        out_specs = [pl.BlockSpec((1, tm, d), row_tile), state_block]
        out_shape = [jax.ShapeDtypeStruct((b, t, d), F32), jax.ShapeDtypeStruct(state.shape, F32)]
        scratch = []
    else:
        out_specs = [pl.BlockSpec((1, tm, d), row_tile),
                     pl.BlockSpec((1, SUBLANES, 2 * d_ff), lambda bi, i: (bi, 0, 0))]
        out_shape = [jax.ShapeDtypeStruct((b, t, d), F32), jax.ShapeDtypeStruct((b, SUBLANES, 2 * d_ff), F32)]
        scratch = [pltpu.VMEM((SUBLANES, 2 * d_ff), F32),
                   pltpu.VMEM((4, tm + SUBLANES, fc), F32)]
    scratch.append(pltpu.VMEM((tm, d_ff), BF16))
    return pl.pallas_call(
        kern,
        grid=(b, t // tm),
        in_specs=in_specs,
        out_specs=out_specs,
        out_shape=out_shape,
        scratch_shapes=scratch,
        compiler_params=_params("arbitrary", "arbitrary"),
        name="post_sample" if per_row else "post_prompt",
    )(*args)


def _pre_sample_kernel(x_ref, mod_ref, g_ref, win_ref, poolw_ref, pscale_ref, cos_ref, sin_ref, st_ref,
                       stn_ref, q_ref, k_ref, v_ref, pool_ref, *, d_model, pw, aw, pos0):
    gc = pw // len(POOL_WINDOWS)
    x = x_ref[...]
    mod = mod_ref[...]
    shift, scale = mod[:, 0:d_model], mod[:, d_model:2 * d_model]
    h = (_rms(x, g_ref[...]) * (1.0 + scale) + shift).astype(BF16)
    project = lambda lo, hi: jnp.dot(h, win_ref[:, lo:hi], preferred_element_type=F32)
    u = project(0, pw)
    cos, sin_signed = cos_ref[...], sin_ref[...]
    q_ref[...] = _rope(project(pw, pw + aw), cos, sin_signed)
    k_ref[...] = _rope(project(pw + aw, pw + 2 * aw), cos, sin_signed)
    v_ref[...] = project(pw + 2 * aw, pw + 3 * aw)

    pooled = []
    for g, w in enumerate(POOL_WINDOWS):
        cols = slice(g * gc, (g + 1) * gc)
        win = u[:, cols]
        for r in range(POOL_STATE - (w - 1), POOL_STATE):
            win = win + st_ref[r, :, cols]
        pooled.append(win / float(min(w, pos0 + 1)) - u[:, cols])
    pool_ref[...] = _pool_project(jnp.concatenate(pooled, axis=1), poolw_ref, pscale_ref[...]).astype(BF16)

    for r in range(POOL_STATE - 1):
        stn_ref[r] = st_ref[r + 1]
    stn_ref[POOL_STATE - 1] = u


def _pre_sample_call(x, mod, g_pre, w_in_b, pool_w_b, pool_scale, cos_t, sin_t, state_pool, *, pw, aw, pos0):
    rows, d = x.shape
    tr = SAMPLE_ROWS if rows % SAMPLE_ROWS == 0 else rows
    kern = functools.partial(_pre_sample_kernel, d_model=d, pw=pw, aw=aw, pos0=pos0)
    const2 = lambda i: (0, 0)
    seqs = lambda i: (i, 0)
    return pl.pallas_call(
        kern,
        grid=(rows // tr,),
        in_specs=[pl.BlockSpec((tr, d), seqs),
                  pl.BlockSpec((tr, 2 * d), seqs),
                  _resident((1, d), const2),
                  _resident(w_in_b.shape, const2),
                  _resident(pool_w_b.shape, lambda i: (0, 0, 0)),
                  _resident((1, pw), const2),
                  _resident((1, LANES), const2),
                  _resident((1, LANES), const2),
                  pl.BlockSpec((POOL_STATE, tr, pw), lambda i: (0, i, 0))],
        out_specs=[pl.BlockSpec((POOL_STATE, tr, pw), lambda i: (0, i, 0)),
                   pl.BlockSpec((tr, aw), seqs),
                   pl.BlockSpec((tr, aw), seqs),
                   pl.BlockSpec((tr, aw), seqs),
                   pl.BlockSpec((tr, pw), seqs)],
        out_shape=[jax.ShapeDtypeStruct((POOL_STATE, rows, pw), F32),
                   jax.ShapeDtypeStruct((rows, aw), F32),
                   jax.ShapeDtypeStruct((rows, aw), F32),
                   jax.ShapeDtypeStruct((rows, aw), F32),
                   jax.ShapeDtypeStruct((rows, pw), BF16)],
        compiler_params=_params("arbitrary"),
        name="pre_sample",
    )(x, mod, g_pre, w_in_b, pool_w_b, pool_scale, cos_t, sin_t, state_pool)


def _stage_pages(pt_ref, ck_hbm, cv_hbm, kbuf, vbuf, page_sem, step, n_steps, first_seq, n_seq):
    _, seqs_per_step, n_pages = kbuf.shape[:3]
    slot = step % 2

    def page_copies(s, into):
        copies = []
        group = jnp.minimum(s, n_seq // seqs_per_step - 1)
        for j in range(seqs_per_step):
            seq = first_seq + group * seqs_per_step + j
            for hbm, buf, which in ((ck_hbm, kbuf, 0), (cv_hbm, vbuf, 1)):
                for p in range(n_pages):
                    copies.append(pltpu.make_async_copy(hbm.at[pt_ref[seq, p]], buf.at[into, j, p],
                                                        page_sem.at[into, which]))
        return copies

    @pl.when(step == 0)
    def _():
        for copy in page_copies(step, slot):
            copy.start()

    @pl.when(step + 1 < n_steps)
    def _():
        for copy in page_copies(step + 1, 1 - slot):
            copy.start()

    for copy in page_copies(step, slot):
        copy.wait()
    return ([[kbuf.at[slot, j, p] for p in range(n_pages)] for j in range(seqs_per_step)],
            [[vbuf.at[slot, j, p] for p in range(n_pages)] for j in range(seqs_per_step)])


def _own_lanes(n_heads):
    aw = n_heads * HEAD_DIM
    head_row = lax.broadcasted_iota(jnp.int32, (n_heads, aw), 0)
    head_of_lane = lax.broadcasted_iota(jnp.int32, (n_heads, aw), 1) // HEAD_DIM
    return head_row == head_of_lane


def _sample_weights(q, k_new, k_refs, page_size):
    n_pages = len(k_refs)
    n_heads = k_refs[0].shape[0]
    aw = n_heads * HEAD_DIM
    ppb = MOBA_BLOCK // page_size
    n_blocks = n_pages // ppb

    qbd = jnp.where(_own_lanes(n_heads), q * (HEAD_DIM ** -0.5 * LOG2E), 0.0)
    qbd_b = qbd.astype(BF16)

    scores = []
    for n in range(n_blocks):
        kt = jnp.concatenate([k_refs[p][...].reshape(aw, page_size).astype(BF16)
                              for p in range(n * ppb, (n + 1) * ppb)], axis=1)
        s = jnp.dot(qbd_b, kt, preferred_element_type=F32)
        scores += [s[:, r * page_size:(r + 1) * page_size] for r in range(ppb)]

    pages_of = lambda n: range(n * ppb, (n + 1) * ppb)
    gates = [sum(jnp.sum(scores[p], axis=-1, keepdims=True) for p in pages_of(n)) for n in range(n_blocks)]
    peaks = [functools.reduce(jnp.maximum, [jnp.max(scores[p], axis=-1, keepdims=True) for p in pages_of(n)])
             for n in range(n_blocks)]
    chosen = []
    for n in range(n_blocks):
        beaten = jnp.zeros((n_heads, 1), jnp.int32)
        for m in range(n_blocks):
            if m != n:
                wins = (gates[m] >= gates[n]) if m < n else (gates[m] > gates[n])
                beaten = beaten + wins.astype(jnp.int32)
        chosen.append(beaten < MOBA_TOPK)

    s_new = jnp.sum(qbd * k_new, axis=-1, keepdims=True)
    m = s_new
    for n in range(n_blocks):
        m = jnp.maximum(m, jnp.where(chosen[n], peaks[n], -jnp.inf))
    p_new = jnp.exp2(s_new - m)
    weights = [jnp.exp2(jnp.where(chosen[p // ppb], scores[p], -jnp.inf) - m) for p in range(n_pages)]
    inv_l = 1.0 / (p_new + sum(jnp.sum(w, axis=-1, keepdims=True) for w in weights))
    return weights, p_new, inv_l


def _sample_values(weights, p_new, inv_l, v_new, v_refs, page_size):
    n_pages = len(v_refs)
    n_heads = v_refs[0].shape[0]
    partial = []
    for h in range(n_heads):
        acc = None
        for p in range(n_pages):
            term = v_refs[p][h] * weights[p][h:h + 1, :]
            acc = term if acc is None else acc + term
        partial.append(acc)
    partial = jnp.concatenate(partial, axis=0)
    ones = jnp.ones((SUBLANES, page_size), BF16)
    high = partial.astype(BF16)
    low = (partial - high.astype(F32)).astype(BF16)
    contract_lanes = (((1,), (1,)), ((), ()))
    past = (lax.dot_general(ones, high, contract_lanes, preferred_element_type=F32)
            + lax.dot_general(ones, low, contract_lanes, preferred_element_type=F32))[0:1, :]
    own_lanes = _own_lanes(n_heads)
    inv_row = jnp.sum(jnp.where(own_lanes, inv_l, 0.0), axis=0, keepdims=True)
    own_row = jnp.sum(jnp.where(own_lanes, p_new, 0.0), axis=0, keepdims=True)
    return (past + own_row * v_new) * inv_row


def _rope_tables(positions):
    half = HEAD_DIM // 2
    lane = jnp.arange(LANES, dtype=jnp.int32)
    inv = ROPE_THETA ** (-(lane % half).astype(F32) / half)
    sign = jnp.where((lane % HEAD_DIM) < half, -1.0, 1.0).astype(F32)
    ang = positions.astype(F32)[:, None] * inv[None, :]
    return jnp.cos(ang), jnp.sin(ang) * sign[None, :]


def kernel(x_prompt, x_sample, c_prompt, c_sample, cache_k, cache_v, page_table, state_pool, state_conv, w_ada, b_ada, g_mix_pre, g_mix_post, w_in, pool_w, pool_scale, w_out, g_ffn_pre, g_ffn_post, w_up, conv_w, conv_b, w_down):
    depth = w_ada.shape[0]
    b, t, d = x_prompt.shape
    n_seq, dec_t, _ = x_sample.shape
    n_pages = page_table.shape[1]
    page_size, n_heads = cache_k.shape[2], cache_k.shape[3]
    past_len = n_pages * page_size
    aw = n_heads * HEAD_DIM
    pw = state_pool.shape[3]
    d_ff = w_down.shape[1]
    assert dec_t == 1 and t % MOBA_BLOCK == 0 and t % PRE_TILE == 0 and pw == d - aw
    assert (pw // len(POOL_WINDOWS)) % LANES == 0 and w_in.shape[2] == pw + 3 * aw

    tm = min(PRE_TILE, t)
    cos_r, sin_r = _rope_tables(jnp.arange(tm, dtype=jnp.int32))
    cos_t, sin_t = _rope_tables(tm * jnp.arange(t // tm, dtype=jnp.int32))
    rope_p = (cos_r, sin_r, cos_t[:, None, :], sin_t[:, None, :])
    cos_s, sin_s = _rope_tables(past_len + jnp.arange(dec_t, dtype=jnp.int32))

    y_p, y_s = x_prompt, x_sample.reshape(1, n_seq, d)
    outs = [[] for _ in range(8)]
    for l in range(depth):
        w = dict(w_out_p=w_out[l, :pw].astype(BF16), w_out_a=w_out[l, pw:].astype(BF16),
                 g_mix_post=g_mix_post[l].reshape(1, d), g_ffn_pre=g_ffn_pre[l].reshape(1, d),
                 g_ffn_post=g_ffn_post[l].reshape(1, d), w_up=w_up[l].astype(BF16), conv_w=conv_w[l],
                 conv_b=conv_b[l].reshape(1, 2 * d_ff), w_down=w_down[l].astype(BF16))
        g_pre = g_mix_pre[l].reshape(1, d)
        w_in_b = w_in[l].astype(BF16)
        pool_w_b = pool_w[l].astype(BF16)
        pscale = pool_scale[l].reshape(1, pw)

        mod_p, mod_s = _mod_call(c_prompt, c_sample, w_ada[l], b_ada[l])
        mod_p, mod_s = mod_p.reshape(b, 1, 6 * d), mod_s.reshape(1, n_seq, 6 * d)

        pool_state_s, q_s, k_s, v_s, pool_s = _pre_sample_call(
            y_s[0], mod_s[0], g_pre, w_in_b, pool_w_b, pscale, cos_s, sin_s, jnp.swapaxes(state_pool[l], 0, 1),
            pw=pw, aw=aw, pos0=past_len)
        tok = lambda a: a.reshape(n_seq, 1, aw)
        sample = (page_table, tok(q_s), tok(k_s), tok(v_s),
                  jnp.transpose(cache_k[l], (0, 2, 3, 1)), jnp.transpose(cache_v[l], (0, 2, 3, 1)))
        n_early = max(1, min(n_seq - 1, int(n_seq * SAMPLE_EARLY_SHARE)))
        k_p, v_p, qt, kb, vt4, pool_p, utail, att_s0 = _pre_prompt_call(
            y_p, mod_p, g_pre, w_in_b, pool_w_b, pscale, rope_p, *sample, pw=pw, aw=aw, first_seq=0, n_seq=n_early)
        att_t, att_s1 = _moba_call(qt, kb.reshape(b, t // MOBA_BLOCK, MOBA_BLOCK, aw), vt4, *sample,
                                   first_seq=n_early, n_seq=n_seq - n_early)
        att_s = jnp.concatenate([att_s0, att_s1], axis=0)
        y_p, ctail = _post_call(y_p, pool_p, att_t, mod_p, w, att_transposed=True)
        y_s, conv_state_s = _post_call(y_s, pool_s.reshape(1, n_seq, pw), att_s.reshape(1, n_seq, aw).astype(BF16),
                                       mod_s, w, att_transposed=False, state=state_conv[l][None])

        outs[0].append(k_p.reshape(b, t, n_heads, HEAD_DIM))
        outs[1].append(v_p.reshape(b, t, n_heads, HEAD_DIM))
        outs[2].append(utail[:, POOL_HALO - POOL_STATE:])
        outs[3].append(ctail[:, SUBLANES - (CONV_WIDTH - 1):])
        outs[4].append(k_s.reshape(n_seq, dec_t, n_heads, HEAD_DIM))
        outs[5].append(v_s.reshape(n_seq, dec_t, n_heads, HEAD_DIM))
        outs[6].append(jnp.swapaxes(pool_state_s, 0, 1))
        outs[7].append(conv_state_s[0])
    stacked = [jnp.stack(o) for o in outs]
    return (y_p, y_s.reshape(n_seq, dec_t, d), *stacked)
```

```python
import functools
import math

import jax
import jax.numpy as jnp
from jax import lax
from jax.experimental import pallas as pl
from jax.experimental.pallas import tpu as pltpu

HEAD_DIM = 64
MOBA_BLOCK = 256
MOBA_TOPK = 3
POOL_WINDOWS = (2, 4, 8, 16)
POOL_STATE = max(POOL_WINDOWS) - 1
CONV_WIDTH = 3
ROPE_THETA = 10000.0
RMS_EPS = 1e-6

LANES = 128
SUBLANES = 8
VMEM_LIMIT_BYTES = 56 * 1024 * 1024

BF16_ROWS = 16
V_ROWS = HEAD_DIM + BF16_ROWS

POOL_HALO = 16
PRE_TILE = 256
SAMPLE_EARLY_SHARE = 0.5
POST_TILE = 512
SAMPLE_ROWS = 32
CONV_ROWS = 256
FFN_CHUNKS = 11
FFN_LOOKAHEAD = 2
FFN_DOWN_SPLITS = 2
PAST_GROUP = 8
SOFTMAX_BLOCKS = 1
SCORE_LOOKAHEAD = 5

F32 = jnp.float32
BF16 = jnp.bfloat16
LOG2E = math.log2(math.e)
MASKED = -1e30


def _params(*semantics):
    return pltpu.CompilerParams(dimension_semantics=semantics, vmem_limit_bytes=VMEM_LIMIT_BYTES)


def _resident(shape, index_map):
    return pl.BlockSpec(shape, index_map, pipeline_mode=pl.Buffered(1))


def _rms(x, g):
    ms = jnp.mean(x * x, axis=-1, keepdims=True)
    return x * lax.rsqrt(ms + RMS_EPS) * g


def _rope(x, cos, sin_signed):
    lane = lax.broadcasted_iota(jnp.int32, (1, LANES), 1)
    first_half = (lane % HEAD_DIM) < HEAD_DIM // 2
    half = HEAD_DIM // 2
    outs = []
    for c in range(x.shape[1] // LANES):
        blk = x[:, c * LANES:(c + 1) * LANES]
        partner = jnp.where(first_half, pltpu.roll(blk, LANES - half, 1), pltpu.roll(blk, half, 1))
        outs.append(blk * cos + partner * sin_signed)
    return jnp.concatenate(outs, axis=1)


def _pool_project(pooled, poolw_ref, pscale):
    n_groups, gc, _ = poolw_ref.shape
    outs = [jnp.dot(pooled[:, g * gc:(g + 1) * gc].astype(BF16), poolw_ref[g], preferred_element_type=F32)
            for g in range(n_groups)]
    return jnp.concatenate(outs, axis=1) * pscale


def _mod_kernel(cp_ref, cs_ref, wa_ref, wb_ref, b_ref, op_ref, os_ref):
    half = wa_ref.shape[1]
    acts = [(c_ref[...] * jax.nn.sigmoid(c_ref[...])).astype(BF16) for c_ref in (cp_ref, cs_ref)]
    for r, w_ref in enumerate((wa_ref, wb_ref)):
        cols = slice(r * half, (r + 1) * half)
        w = w_ref[...].astype(BF16)
        for a, o_ref in zip(acts, (op_ref, os_ref)):
            o_ref[:, cols] = jnp.dot(a, w, preferred_element_type=F32) + b_ref[:, cols]


def _mod_call(c_prompt, c_sample, w_ada, b_ada):
    d, n_out = w_ada.shape
    rows_p, rows_s = c_prompt.shape[0], c_sample.shape[0]
    full = lambda j: (0, 0)
    col = lambda j: (0, j)
    halves = [pl.BlockSpec((d, d // 2), lambda j, r=r: (0, 2 * j + r)) for r in range(2)]
    return pl.pallas_call(
        _mod_kernel,
        grid=(n_out // d,),
        in_specs=[pl.BlockSpec((rows_p, d), full), pl.BlockSpec((rows_s, d), full), *halves,
                  pl.BlockSpec((1, d), col)],
        out_specs=[pl.BlockSpec((rows_p, d), col), pl.BlockSpec((rows_s, d), col)],
        out_shape=[jax.ShapeDtypeStruct((rows_p, n_out), F32), jax.ShapeDtypeStruct((rows_s, n_out), F32)],
        compiler_params=_params("arbitrary"),
        name="adaln_mod",
    )(c_prompt, c_sample, w_ada, w_ada, b_ada.reshape(1, n_out))


def _pre_prompt_kernel(pt_ref, x_ref, mod_ref, g_ref, win_ref, poolw_ref, pscale_ref, cos_ref, sin_ref, cost_ref,
                       sint_ref, qs_ref, kn_ref, vn_ref, ck_hbm, cv_hbm,
                       k_ref, v_ref, qt_ref, kb_ref, vt_ref, pool_ref, utail_ref, os_ref,
                       ubuf, s2buf, s4buf, s8buf, kbuf, vbuf, page_sem, *, tm, d_model, pw, aw, q_scale, first_seq,
                       n_seq):
    i = pl.program_id(1)
    halo = POOL_HALO
    gc = pw // len(POOL_WINDOWS)
    bufs = (ubuf, s2buf, s4buf, s8buf)
    page_size = kbuf.shape[-1]
    step = pl.program_id(0) * pl.num_programs(1) + i
    k_pages, v_pages = _stage_pages(pt_ref, ck_hbm, cv_hbm, kbuf, vbuf, page_sem, step,
                                    pl.num_programs(0) * pl.num_programs(1), first_seq, n_seq)
    seqs_per_step = len(k_pages)

    @pl.when(i == 0)
    def _():
        for buf in bufs:
            buf[0:halo, :] = jnp.zeros((halo, buf.shape[1]), F32)

    sample_weights = [_sample_weights(qs_ref[j], kn_ref[j], k_pages[j], page_size) for j in range(seqs_per_step)]

    x = x_ref[0]
    mod = mod_ref[0]
    shift, scale = mod[:, 0:d_model], mod[:, d_model:2 * d_model]
    h = _rms(x, g_ref[...]) * (1.0 + scale) + shift
    proj = jnp.dot(h.astype(BF16), win_ref[...], preferred_element_type=F32)
    u = proj[:, 0:pw]
    cos_t, sin_t, cos_r, sin_r = cost_ref[0], sint_ref[0], cos_ref[...], sin_ref[...]
    cos = cos_t * cos_r - sin_t * sin_r
    sin_signed = sin_t * cos_r + cos_t * sin_r
    q = _rope(proj[:, pw:pw + aw], cos, sin_signed)
    k = _rope(proj[:, pw + aw:pw + 2 * aw], cos, sin_signed)
    v = proj[:, pw + 2 * aw:pw + 3 * aw]

    k_ref[0] = k
    v_ref[0] = v
    kb_ref[0] = k.astype(BF16)
    qt_ref[0] = (q * q_scale).T.astype(BF16)
    ones_pad = jnp.where(lax.broadcasted_iota(jnp.int32, (BF16_ROWS, MOBA_BLOCK), 0) == 0, 1.0, 0.0).astype(BF16)
    for r in range(tm // MOBA_BLOCK):
        vt = v[r * MOBA_BLOCK:(r + 1) * MOBA_BLOCK, :].T.astype(BF16)
        for hd in range(aw // HEAD_DIM):
            vt_ref[0, r, hd * V_ROWS:hd * V_ROWS + HEAD_DIM, :] = vt[hd * HEAD_DIM:(hd + 1) * HEAD_DIM, :]
            vt_ref[0, r, hd * V_ROWS + HEAD_DIM:(hd + 1) * V_ROWS, :] = ones_pad

    ubuf[halo:halo + tm, :] = u
    s2 = u + ubuf[halo - 1:halo - 1 + tm, :]
    s2buf[halo:halo + tm, :] = s2[:, gc:]
    s4 = s2[:, gc:] + s2buf[halo - 2:halo - 2 + tm, :]
    s4buf[halo:halo + tm, :] = s4[:, gc:]
    s8 = s4[:, gc:] + s4buf[halo - 4:halo - 4 + tm, :]
    s8buf[halo:halo + tm, :] = s8[:, gc:]
    s16 = s8[:, gc:] + s8buf[halo - 8:halo - 8 + tm, :]
    wins = (s2[:, 0:gc], s4[:, 0:gc], s8[:, 0:gc], s16)
    pos1 = (i * tm + 1 + lax.broadcasted_iota(jnp.int32, (tm, 1), 0)).astype(F32)
    pooled = jnp.concatenate(
        [wins[g] / jnp.minimum(float(w), pos1) - u[:, g * gc:(g + 1) * gc] for g, w in enumerate(POOL_WINDOWS)],
        axis=1)
    pool_ref[0] = _pool_project(pooled, poolw_ref, pscale_ref[...]).astype(BF16)

    for buf in bufs:
        buf[0:halo, :] = buf[tm:tm + halo, :]
    utail_ref[0] = ubuf[0:halo, :]

    for j in range(seqs_per_step):
        os_ref[j] = _sample_values(*sample_weights[j], vn_ref[j], v_pages[j], page_size)


def _sample_specs(n_steps, step_of, first_seq, n_seq, page_table, cache_kt):
    _, n_pages = page_table.shape
    _, n_heads, dh, page_size = cache_kt.shape
    assert dh == HEAD_DIM and MOBA_BLOCK % page_size == 0 and (n_pages * page_size) % MOBA_BLOCK == 0
    assert 0 < n_seq <= n_steps
    tok = pl.BlockSpec((1, 1, n_heads * dh),
                       lambda *idx: (first_seq + jnp.minimum(step_of(*idx[:-1]), n_seq - 1), 0, 0))
    whole_cache = pl.BlockSpec(memory_space=pl.ANY)
    page_slots = pltpu.VMEM((2, 1, n_pages, n_heads, dh, page_size), F32)
    return tok, whole_cache, [page_slots, page_slots, pltpu.SemaphoreType.DMA((2, 2))]


def _pre_prompt_call(x, mod, g_pre, w_in_b, pool_w_b, pool_scale, rope, page_table, q_s, k_s, v_s, cache_kt, cache_vt,
                     *, pw, aw, first_seq, n_seq):
    b, t, d = x.shape
    tm = min(PRE_TILE, t)
    n_tiles = t // tm
    nb = t // MOBA_BLOCK
    gc = pw // len(POOL_WINDOWS)
    in_w = w_in_b.shape[1]
    vt_rows = (aw // HEAD_DIM) * V_ROWS
    kern = functools.partial(_pre_prompt_kernel, tm=tm, d_model=d, pw=pw, aw=aw,
                             q_scale=HEAD_DIM ** -0.5 * LOG2E, first_seq=first_seq, n_seq=n_seq)
    row_tile = lambda bi, i, pt: (bi, i, 0)
    const2 = lambda bi, i, pt: (0, 0)
    tok, whole_cache, page_scratch = _sample_specs(b * n_tiles, lambda bi, i: bi * n_tiles + i, first_seq, n_seq,
                                                   page_table, cache_kt)
    tok_out = pl.BlockSpec(tok.block_shape, lambda bi, i, pt: (jnp.minimum(bi * n_tiles + i, n_seq - 1), 0, 0))
    return pl.pallas_call(
        kern,
        grid_spec=pltpu.PrefetchScalarGridSpec(
            num_scalar_prefetch=1,
            grid=(b, n_tiles),
            in_specs=[pl.BlockSpec((1, tm, d), row_tile),
                      pl.BlockSpec((1, 1, mod.shape[2]), lambda bi, i, pt: (bi, 0, 0)),
                      _resident((1, d), const2),
                      _resident((d, in_w), const2),
                      _resident(pool_w_b.shape, lambda bi, i, pt: (0, 0, 0)),
                      _resident((1, pw), const2),
                      _resident((tm, LANES), const2),
                      _resident((tm, LANES), const2),
                      pl.BlockSpec((1, 1, LANES), lambda bi, i, pt: (i, 0, 0)),
                      pl.BlockSpec((1, 1, LANES), lambda bi, i, pt: (i, 0, 0)),
                      tok, tok, tok, whole_cache, whole_cache],
            out_specs=[pl.BlockSpec((1, tm, aw), row_tile),
                       pl.BlockSpec((1, tm, aw), row_tile),
                       pl.BlockSpec((1, aw, tm), lambda bi, i, pt: (bi, 0, i)),
                       pl.BlockSpec((1, tm, aw), row_tile),
                       pl.BlockSpec((1, tm // MOBA_BLOCK, vt_rows, MOBA_BLOCK), lambda bi, i, pt: (bi, i, 0, 0)),
                       pl.BlockSpec((1, tm, pw), row_tile),
                       pl.BlockSpec((1, POOL_HALO, pw), lambda bi, i, pt: (bi, 0, 0)),
                       tok_out],
            scratch_shapes=[pltpu.VMEM((tm + POOL_HALO, pw), F32),
                            pltpu.VMEM((tm + POOL_HALO, pw - gc), F32),
                            pltpu.VMEM((tm + POOL_HALO, pw - 2 * gc), F32),
                            pltpu.VMEM((tm + POOL_HALO, pw - 3 * gc), F32),
                            *page_scratch]),
        out_shape=[jax.ShapeDtypeStruct((b, t, aw), F32),
                   jax.ShapeDtypeStruct((b, t, aw), F32),
                   jax.ShapeDtypeStruct((b, aw, t), BF16),
                   jax.ShapeDtypeStruct((b, t, aw), BF16),
                   jax.ShapeDtypeStruct((b, nb, vt_rows, MOBA_BLOCK), BF16),
                   jax.ShapeDtypeStruct((b, t, pw), BF16),
                   jax.ShapeDtypeStruct((b, POOL_HALO, pw), F32),
                   jax.ShapeDtypeStruct((n_seq, 1, aw), F32)],
        compiler_params=_params("arbitrary", "arbitrary"),
        name="pre_prompt",
    )(page_table, x, mod, g_pre, w_in_b, pool_w_b, pool_scale, *rope, q_s, k_s, v_s, cache_kt, cache_vt)


def _moba_kernel(pt_ref, qt_ref, kb_ref, vt_ref, qs_ref, kn_ref, vn_ref, ck_hbm, cv_hbm, o_ref, os_ref,
                 km_ref, ind_ref, q2_ref, m_ref, acc_ref, kbuf, vbuf, page_sem, *, nb, n_heads, first_seq, n_seq):
    i = pl.program_id(1)
    tq = MOBA_BLOCK
    pair_w = 2 * HEAD_DIM
    page_size = kbuf.shape[-1]
    k_pages, v_pages = _stage_pages(pt_ref, ck_hbm, cv_hbm, kbuf, vbuf, page_sem, pl.program_id(0) * nb + i,
                                    pl.num_programs(0) * nb, first_seq, n_seq)

    @pl.when(i == 0)
    def _():
        lane = lax.broadcasted_iota(jnp.int32, (MOBA_BLOCK, LANES), 1)
        for n in range(nb):
            km_ref[n:n + 1, :] = jnp.mean(kb_ref[0, n].astype(F32), axis=0, keepdims=True)
            ind_ref[n] = jnp.where(lane == n, 1.0, 0.0).astype(BF16)
        for h in range(n_heads):
            q2_ref[h, pair_w + nb:, :] = jnp.zeros((pair_w - nb, tq), BF16)

    sample_weights = _sample_weights(qs_ref[0], kn_ref[0], k_pages[0], page_size)

    blk_row = lax.broadcasted_iota(jnp.int32, (nb, tq), 0)
    blk_row_f = blk_row.astype(F32)
    fully_past = blk_row < i
    key_row = lax.broadcasted_iota(jnp.int32, (MOBA_BLOCK, tq), 0)
    qry_col = lax.broadcasted_iota(jnp.int32, (MOBA_BLOCK, tq), 1)
    causal = key_row <= qry_col
    pair_row = lax.broadcasted_iota(jnp.int32, (pair_w, tq), 0)

    for h in range(n_heads):
        lo = (h // 2) * pair_w
        mine = (pair_row >= HEAD_DIM) if h % 2 else (pair_row < HEAD_DIM)
        q2 = jnp.where(mine, qt_ref[0, lo:lo + pair_w, :], jnp.zeros((), BF16))
        q2_ref[h, 0:pair_w, :] = q2

        gate = jnp.dot(km_ref[:, lo:lo + pair_w].astype(BF16), q2, preferred_element_type=F32)
        gate = jnp.where(fully_past, gate, -jnp.inf)
        chosen = jnp.zeros((nb, tq), jnp.bool_)
        for _ in range(MOBA_TOPK):
            best = jnp.max(gate, axis=0, keepdims=True)
            first = jnp.min(jnp.where(gate == best, blk_row_f, float(nb)), axis=0, keepdims=True)
            pick = blk_row_f == first
            chosen = jnp.logical_or(chosen, pick)
            gate = jnp.where(pick, -jnp.inf, gate)
        q2_ref[h, pair_w:pair_w + nb, :] = jnp.where(jnp.logical_and(chosen, fully_past), 0.0, MASKED).astype(BF16)

    def scores(js, h):
        lo = (h // 2) * pair_w
        return [jnp.dot(jnp.concatenate([kb_ref[0, j, :, lo:lo + pair_w], ind_ref[j]], axis=1), q2_ref[h],
                        preferred_element_type=F32) for j in js]

    def own_scores(js, h):
        lo = (h // 2) * pair_w
        return [jnp.dot(kb_ref[0, j, :, lo:lo + pair_w], q2_ref[h, 0:pair_w, :], preferred_element_type=F32)
                for j in js]

    def all_heads(js, consume, scores=scores):
        items = [(js[r:r + SOFTMAX_BLOCKS], h) for r in range(0, len(js), SOFTMAX_BLOCKS) for h in range(n_heads)]
        look = min(SCORE_LOOKAHEAD, len(items))
        ahead = {k: scores(*items[k]) for k in range(look)}
        for k, (sub, h) in enumerate(items):
            if k + look < len(items):
                ahead[k + look] = scores(*items[k + look])
            consume(sub, h, ahead.pop(k))

    def values(j, h):
        return vt_ref[0, j, h * V_ROWS:(h + 1) * V_ROWS, :]

    def first_block(_, h, ss):
        s = jnp.where(causal, ss[0], -jnp.inf)
        m0 = jnp.max(s, axis=0, keepdims=True)
        m_ref[h] = m0
        acc_ref[h] = jnp.dot(values(i, h), jnp.exp2(s - m0).astype(BF16), preferred_element_type=F32)

    all_heads([i], first_block, own_scores)
    os_ref[0] = _sample_values(*sample_weights, vn_ref[0], v_pages[0], page_size)

    def online_update(js, h, ss, m_old, acc_old):
        m_new = m_old
        for s in ss:
            m_new = jnp.maximum(m_new, jnp.max(s, axis=0, keepdims=True))
        pv = None
        for j, s in zip(js, ss):
            d = jnp.dot(values(j, h), jnp.exp2(s - m_new).astype(BF16), preferred_element_type=F32)
            pv = d if pv is None else pv + d
        return m_new, jnp.exp2(m_old - m_new) * acc_old + pv

    def past_blocks(js, h, ss):
        m_ref[h], acc_ref[h] = online_update(js, h, ss, m_ref[h], acc_ref[h])

    size, top = 1, i
    while size < PAST_GROUP:
        @pl.when((i & size) != 0)
        def _(size=size, top=top):
            all_heads([top - size + r for r in range(size)], past_blocks)

        top = top - (i & size)
        size *= 2

    def group_body(t, carry):
        all_heads([PAST_GROUP * t + r for r in range(PAST_GROUP)], past_blocks)
        return carry

    lax.fori_loop(0, i // PAST_GROUP, group_body, 0)

    for h in range(n_heads):
        acc = acc_ref[h]
        o_ref[0, h * HEAD_DIM:(h + 1) * HEAD_DIM, :] = (acc[0:HEAD_DIM] / acc[HEAD_DIM:HEAD_DIM + 1]).astype(BF16)


def _moba_call(qt, kb4, vt4, page_table, q_s, k_s, v_s, cache_kt, cache_vt, *, first_seq, n_seq):
    b, aw, t = qt.shape
    nb = t // MOBA_BLOCK
    n_heads = aw // HEAD_DIM
    assert nb <= LANES and n_heads % 2 == 0
    kern = functools.partial(_moba_kernel, nb=nb, n_heads=n_heads, first_seq=first_seq, n_seq=n_seq)
    per_batch = lambda bi, i, pt: (bi, 0, 0, 0)
    q_tile = pl.BlockSpec((1, aw, MOBA_BLOCK), lambda bi, i, pt: (bi, 0, i))
    tok, whole_cache, page_scratch = _sample_specs(b * nb, lambda bi, i: bi * nb + i, first_seq, n_seq,
                                                   page_table, cache_kt)
    tok_out = pl.BlockSpec(tok.block_shape, lambda bi, i, pt: (jnp.minimum(bi * nb + i, n_seq - 1), 0, 0))
    return pl.pallas_call(
        kern,
        grid_spec=pltpu.PrefetchScalarGridSpec(
            num_scalar_prefetch=1,
            grid=(b, nb),
            in_specs=[q_tile,
                      pl.BlockSpec((1, nb, MOBA_BLOCK, aw), per_batch, pipeline_mode=pl.Buffered(1)),
                      pl.BlockSpec((1, nb, n_heads * V_ROWS, MOBA_BLOCK), per_batch, pipeline_mode=pl.Buffered(1)),
                      tok, tok, tok, whole_cache, whole_cache],
            out_specs=[q_tile, tok_out],
            scratch_shapes=[pltpu.VMEM((nb, aw), F32),
                            pltpu.VMEM((nb, MOBA_BLOCK, LANES), BF16),
                            pltpu.VMEM((n_heads, 4 * HEAD_DIM, MOBA_BLOCK), BF16),
                            pltpu.VMEM((n_heads, 1, MOBA_BLOCK), F32),
                            pltpu.VMEM((n_heads, V_ROWS, MOBA_BLOCK), F32),
                            *page_scratch]),
        out_shape=[jax.ShapeDtypeStruct((b, aw, t), BF16), jax.ShapeDtypeStruct((n_seq, 1, aw), F32)],
        compiler_params=_params("arbitrary", "arbitrary"),
        name="moba",
    )(page_table, qt, kb4, vt4, q_s, k_s, v_s, cache_kt, cache_vt)


def _post_kernel(*refs, tm, d_model, d_ff, att_transposed, per_row_state):
    (x_ref, pool_ref, att_ref, mod_ref, wop_ref, woa_ref, gpost_ref, gpre2_ref, gpost2_ref,
     wup_ref, cw_ref, cb_ref, wdn_ref) = refs[:13]
    if per_row_state:
        st_ref, y_ref, stn_ref, actbuf = refs[13:]
    else:
        y_ref, ctail_ref, carry, upbuf, actbuf = refs[13:]
        i = pl.program_id(1)

        @pl.when(i == 0)
        def _():
            carry[...] = jnp.zeros(carry.shape, F32)

    d = d_model
    x = x_ref[0]
    mod = mod_ref[0]
    gate1 = mod[:, 2 * d:3 * d]
    shift2, scale2, gate2 = mod[:, 3 * d:4 * d], mod[:, 4 * d:5 * d], mod[:, 5 * d:6 * d]

    mix = jnp.dot(pool_ref[0], wop_ref[...], preferred_element_type=F32)
    if att_transposed:
        mix += lax.dot_general(att_ref[0], woa_ref[...], (((0,), (0,)), ((), ())), preferred_element_type=F32)
    else:
        mix += jnp.dot(att_ref[0], woa_ref[...], preferred_element_type=F32)
    x1 = x + gate1 * _rms(mix, gpost_ref[...])
    h2 = (_rms(x1, gpre2_ref[...]) * (1.0 + scale2) + shift2).astype(BF16)

    fc = d_ff // FFN_CHUNKS

    def up_proj(c):
        out = []
        for base in (0, d_ff):
            cols = slice(base + c * fc, base + (c + 1) * fc)
            out.append((cols, jnp.dot(h2, wup_ref[:, cols], preferred_element_type=F32)))
        return out

    def conv_act(c, ups):
        if per_row_state:
            halves = []
            for cols, up in ups:
                prev2, prev1 = st_ref[0, :, 0, cols], st_ref[0, :, 1, cols]
                stn_ref[0, :, 0, cols] = prev1
                stn_ref[0, :, 1, cols] = up
                halves.append(cb_ref[:, cols] + cw_ref[0:1, cols] * prev2 + cw_ref[1:2, cols] * prev1
                              + cw_ref[2:3, cols] * up)
            a, g = halves
            actbuf[:, c * fc:(c + 1) * fc] = (a * jax.nn.sigmoid(a) * g).astype(BF16)
            return
        bufs = []
        for half, (cols, up) in enumerate(ups):
            buf = upbuf.at[(2 * c + half) % upbuf.shape[0]]
            buf[0:SUBLANES, :] = carry[:, cols]
            buf[SUBLANES:SUBLANES + tm, :] = up
            carry[:, cols] = buf[tm:tm + SUBLANES, :]
            bufs.append((cols, buf))
        for r0 in range(0, tm, CONV_ROWS):
            band = []
            for cols, buf in bufs:
                rows = lambda back: buf[SUBLANES - back + r0:SUBLANES - back + r0 + CONV_ROWS, :]
                band.append(cb_ref[:, cols] + cw_ref[0:1, cols] * rows(2) + cw_ref[1:2, cols] * rows(1)
                            + cw_ref[2:3, cols] * rows(0))
            a, g = band
            actbuf[r0:r0 + CONV_ROWS, c * fc:(c + 1) * fc] = (a * jax.nn.sigmoid(a) * g).astype(BF16)

    split_at = [((s + 1) * FFN_CHUNKS) // FFN_DOWN_SPLITS for s in range(FFN_DOWN_SPLITS)]
    ahead = {c: up_proj(c) for c in range(min(FFN_LOOKAHEAD, FFN_CHUNKS))}
    ffn, done = None, 0
    for c in range(FFN_CHUNKS):
        if c + FFN_LOOKAHEAD < FFN_CHUNKS:
            ahead[c + FFN_LOOKAHEAD] = up_proj(c + FFN_LOOKAHEAD)
        conv_act(c, ahead.pop(c))
        if c + 1 in split_at:
            rows = slice(done * fc, (c + 1) * fc)
            part = jnp.dot(actbuf[:, rows], wdn_ref[rows, :], preferred_element_type=F32)
            ffn = part if ffn is None else ffn + part
            done = c + 1
    y_ref[0] = x1 + gate2 * _rms(ffn, gpost2_ref[...])
    if not per_row_state:
        ctail_ref[0] = carry[...]


def _post_call(x, pool_o, att, mod, w, *, att_transposed, state=None):
    b, t, d = x.shape
    pw = pool_o.shape[2]
    aw = w["w_out_a"].shape[0]
    d_ff = w["w_down"].shape[0]
    per_row = state is not None
    tm = t if per_row else min(POST_TILE, t)
    assert t % tm == 0 and d_ff % (FFN_CHUNKS * LANES) == 0 and CONV_WIDTH == 3
    fc = d_ff // FFN_CHUNKS
    kern = functools.partial(_post_kernel, tm=tm, d_model=d, d_ff=d_ff, att_transposed=att_transposed,
                             per_row_state=per_row)
    row_tile = lambda bi, i: (bi, i, 0)
    const2 = lambda bi, i: (0, 0)
    mod_rows = mod.shape[1]
    in_specs = [pl.BlockSpec((1, tm, d), row_tile),
                pl.BlockSpec((1, tm, pw), row_tile),
                (pl.BlockSpec((1, aw, tm), lambda bi, i: (bi, 0, i)) if att_transposed
                 else pl.BlockSpec((1, tm, aw), row_tile)),
                pl.BlockSpec((1, mod_rows, mod.shape[2]), (row_tile if mod_rows > 1 else lambda bi, i: (bi, 0, 0))),
                _resident((pw, d), const2),
                _resident((aw, d), const2),
                _resident((1, d), const2),
                _resident((1, d), const2),
                _resident((1, d), const2),
                _resident((d, 2 * d_ff), const2),
                _resident((CONV_WIDTH, 2 * d_ff), const2),
                _resident((1, 2 * d_ff), const2),
                _resident((d_ff, d), const2)]
    args = [x, pool_o, att, mod, w["w_out_p"], w["w_out_a"], w["g_mix_post"], w["g_ffn_pre"], w["g_ffn_post"],
            w["w_up"], w["conv_w"], w["conv_b"], w["w_down"]]
    if per_row:
        state_block = pl.BlockSpec((1, tm, CONV_WIDTH - 1, 2 * d_ff), lambda bi, i: (bi, i, 0, 0))
        in_specs.append(state_block)
        args.append(state)
        out_specs = [pl.BlockSpec((1, tm, d), row_tile), state_block]
        out_shape = [jax.ShapeDtypeStruct((b, t, d), F32), jax.ShapeDtypeStruct(state.shape, F32)]
        scratch = []
    else:
        out_specs = [pl.BlockSpec((1, tm, d), row_tile),
                     pl.BlockSpec((1, SUBLANES, 2 * d_ff), lambda bi, i: (bi, 0, 0))]
        out_shape = [jax.ShapeDtypeStruct((b, t, d), F32), jax.ShapeDtypeStruct((b, SUBLANES, 2 * d_ff), F32)]
        scratch = [pltpu.VMEM((SUBLANES, 2 * d_ff), F32),
                   pltpu.VMEM((4, tm + SUBLANES, fc), F32)]
    scratch.append(pltpu.VMEM((tm, d_ff), BF16))
    return pl.pallas_call(
        kern,
        grid=(b, t // tm),
        in_specs=in_specs,
        out_specs=out_specs,
        out_shape=out_shape,
        scratch_shapes=scratch,
        compiler_params=_params("arbitrary", "arbitrary"),
        name="post_sample" if per_row else "post_prompt",
    )(*args)


def _pre_sample_kernel(x_ref, mod_ref, g_ref, win_ref, poolw_ref, pscale_ref, cos_ref, sin_ref, st_ref,
                       stn_ref, q_ref, k_ref, v_ref, pool_ref, *, d_model, pw, aw, pos0):
    gc = pw // len(POOL_WINDOWS)
    x = x_ref[...]
    mod = mod_ref[...]
    shift, scale = mod[:, 0:d_model], mod[:, d_model:2 * d_model]
    h = (_rms(x, g_ref[...]) * (1.0 + scale) + shift).astype(BF16)
    project = lambda lo, hi: jnp.dot(h, win_ref[:, lo:hi], preferred_element_type=F32)
    u = project(0, pw)
    cos, sin_signed = cos_ref[...], sin_ref[...]
    q_ref[...] = _rope(project(pw, pw + aw), cos, sin_signed)
    k_ref[...] = _rope(project(pw + aw, pw + 2 * aw), cos, sin_signed)
    v_ref[...] = project(pw + 2 * aw, pw + 3 * aw)

    pooled = []
    for g, w in enumerate(POOL_WINDOWS):
        cols = slice(g * gc, (g + 1) * gc)
        win = u[:, cols]
        for r in range(POOL_STATE - (w - 1), POOL_STATE):
            win = win + st_ref[r, :, cols]
        pooled.append(win / float(min(w, pos0 + 1)) - u[:, cols])
    pool_ref[...] = _pool_project(jnp.concatenate(pooled, axis=1), poolw_ref, pscale_ref[...]).astype(BF16)

    for r in range(POOL_STATE - 1):
        stn_ref[r] = st_ref[r + 1]
    stn_ref[POOL_STATE - 1] = u


def _pre_sample_call(x, mod, g_pre, w_in_b, pool_w_b, pool_scale, cos_t, sin_t, state_pool, *, pw, aw, pos0):
    rows, d = x.shape
    tr = SAMPLE_ROWS if rows % SAMPLE_ROWS == 0 else rows
    kern = functools.partial(_pre_sample_kernel, d_model=d, pw=pw, aw=aw, pos0=pos0)
    const2 = lambda i: (0, 0)
    seqs = lambda i: (i, 0)
    return pl.pallas_call(
        kern,
        grid=(rows // tr,),
        in_specs=[pl.BlockSpec((tr, d), seqs),
                  pl.BlockSpec((tr, 2 * d), seqs),
                  _resident((1, d), const2),
                  _resident(w_in_b.shape, const2),
                  _resident(pool_w_b.shape, lambda i: (0, 0, 0)),
                  _resident((1, pw), const2),
                  _resident((1, LANES), const2),
                  _resident((1, LANES), const2),
                  pl.BlockSpec((POOL_STATE, tr, pw), lambda i: (0, i, 0))],
        out_specs=[pl.BlockSpec((POOL_STATE, tr, pw), lambda i: (0, i, 0)),
                   pl.BlockSpec((tr, aw), seqs),
                   pl.BlockSpec((tr, aw), seqs),
                   pl.BlockSpec((tr, aw), seqs),
                   pl.BlockSpec((tr, pw), seqs)],
        out_shape=[jax.ShapeDtypeStruct((POOL_STATE, rows, pw), F32),
                   jax.ShapeDtypeStruct((rows, aw), F32),
                   jax.ShapeDtypeStruct((rows, aw), F32),
                   jax.ShapeDtypeStruct((rows, aw), F32),
                   jax.ShapeDtypeStruct((rows, pw), BF16)],
        compiler_params=_params("arbitrary"),
        name="pre_sample",
    )(x, mod, g_pre, w_in_b, pool_w_b, pool_scale, cos_t, sin_t, state_pool)


def _stage_pages(pt_ref, ck_hbm, cv_hbm, kbuf, vbuf, page_sem, step, n_steps, first_seq, n_seq):
    _, seqs_per_step, n_pages = kbuf.shape[:3]
    slot = step % 2

    def page_copies(s, into):
        copies = []
        group = jnp.minimum(s, n_seq // seqs_per_step - 1)
        for j in range(seqs_per_step):
            seq = first_seq + group * seqs_per_step + j
            for hbm, buf, which in ((ck_hbm, kbuf, 0), (cv_hbm, vbuf, 1)):
                for p in range(n_pages):
                    copies.append(pltpu.make_async_copy(hbm.at[pt_ref[seq, p]], buf.at[into, j, p],
                                                        page_sem.at[into, which]))
        return copies

    @pl.when(step == 0)
    def _():
        for copy in page_copies(step, slot):
            copy.start()

    @pl.when(step + 1 < n_steps)
    def _():
        for copy in page_copies(step + 1, 1 - slot):
            copy.start()

    for copy in page_copies(step, slot):
        copy.wait()
    return ([[kbuf.at[slot, j, p] for p in range(n_pages)] for j in range(seqs_per_step)],
            [[vbuf.at[slot, j, p] for p in range(n_pages)] for j in range(seqs_per_step)])


def _own_lanes(n_heads):
    aw = n_heads * HEAD_DIM
    head_row = lax.broadcasted_iota(jnp.int32, (n_heads, aw), 0)
    head_of_lane = lax.broadcasted_iota(jnp.int32, (n_heads, aw), 1) // HEAD_DIM
    return head_row == head_of_lane


def _sample_weights(q, k_new, k_refs, page_size):
    n_pages = len(k_refs)
    n_heads = k_refs[0].shape[0]
    aw = n_heads * HEAD_DIM
    ppb = MOBA_BLOCK // page_size
    n_blocks = n_pages // ppb

    qbd = jnp.where(_own_lanes(n_heads), q * (HEAD_DIM ** -0.5 * LOG2E), 0.0)
    qbd_b = qbd.astype(BF16)

    scores = []
    for n in range(n_blocks):
        kt = jnp.concatenate([k_refs[p][...].reshape(aw, page_size).astype(BF16)
                              for p in range(n * ppb, (n + 1) * ppb)], axis=1)
        s = jnp.dot(qbd_b, kt, preferred_element_type=F32)
        scores += [s[:, r * page_size:(r + 1) * page_size] for r in range(ppb)]

    pages_of = lambda n: range(n * ppb, (n + 1) * ppb)
    gates = [sum(jnp.sum(scores[p], axis=-1, keepdims=True) for p in pages_of(n)) for n in range(n_blocks)]
    peaks = [functools.reduce(jnp.maximum, [jnp.max(scores[p], axis=-1, keepdims=True) for p in pages_of(n)])
             for n in range(n_blocks)]
    chosen = []
    for n in range(n_blocks):
        beaten = jnp.zeros((n_heads, 1), jnp.int32)
        for m in range(n_blocks):
            if m != n:
                wins = (gates[m] >= gates[n]) if m < n else (gates[m] > gates[n])
                beaten = beaten + wins.astype(jnp.int32)
        chosen.append(beaten < MOBA_TOPK)

    s_new = jnp.sum(qbd * k_new, axis=-1, keepdims=True)
    m = s_new
    for n in range(n_blocks):
        m = jnp.maximum(m, jnp.where(chosen[n], peaks[n], -jnp.inf))
    p_new = jnp.exp2(s_new - m)
    weights = [jnp.exp2(jnp.where(chosen[p // ppb], scores[p], -jnp.inf) - m) for p in range(n_pages)]
    inv_l = 1.0 / (p_new + sum(jnp.sum(w, axis=-1, keepdims=True) for w in weights))
    return weights, p_new, inv_l


def _sample_values(weights, p_new, inv_l, v_new, v_refs, page_size):
    n_pages = len(v_refs)
    n_heads = v_refs[0].shape[0]
    partial = []
    for h in range(n_heads):
        acc = None
        for p in range(n_pages):
            term = v_refs[p][h] * weights[p][h:h + 1, :]
            acc = term if acc is None else acc + term
        partial.append(acc)
    partial = jnp.concatenate(partial, axis=0)
    ones = jnp.ones((SUBLANES, page_size), BF16)
    high = partial.astype(BF16)
    low = (partial - high.astype(F32)).astype(BF16)
    contract_lanes = (((1,), (1,)), ((), ()))
    past = (lax.dot_general(ones, high, contract_lanes, preferred_element_type=F32)
            + lax.dot_general(ones, low, contract_lanes, preferred_element_type=F32))[0:1, :]
    own_lanes = _own_lanes(n_heads)
    inv_row = jnp.sum(jnp.where(own_lanes, inv_l, 0.0), axis=0, keepdims=True)
    own_row = jnp.sum(jnp.where(own_lanes, p_new, 0.0), axis=0, keepdims=True)
    return (past + own_row * v_new) * inv_row


def _rope_tables(positions):
    half = HEAD_DIM // 2
    lane = jnp.arange(LANES, dtype=jnp.int32)
    inv = ROPE_THETA ** (-(lane % half).astype(F32) / half)
    sign = jnp.where((lane % HEAD_DIM) < half, -1.0, 1.0).astype(F32)
    ang = positions.astype(F32)[:, None] * inv[None, :]
    return jnp.cos(ang), jnp.sin(ang) * sign[None, :]


def kernel(x_prompt, x_sample, c_prompt, c_sample, cache_k, cache_v, page_table, state_pool, state_conv, w_ada, b_ada, g_mix_pre, g_mix_post, w_in, pool_w, pool_scale, w_out, g_ffn_pre, g_ffn_post, w_up, conv_w, conv_b, w_down):
    depth = w_ada.shape[0]
    b, t, d = x_prompt.shape
    n_seq, dec_t, _ = x_sample.shape
    n_pages = page_table.shape[1]
    page_size, n_heads = cache_k.shape[2], cache_k.shape[3]
    past_len = n_pages * page_size
    aw = n_heads * HEAD_DIM
    pw = state_pool.shape[3]
    d_ff = w_down.shape[1]
    assert dec_t == 1 and t % MOBA_BLOCK == 0 and t % PRE_TILE == 0 and pw == d - aw
    assert (pw // len(POOL_WINDOWS)) % LANES == 0 and w_in.shape[2] == pw + 3 * aw

    tm = min(PRE_TILE, t)
    cos_r, sin_r = _rope_tables(jnp.arange(tm, dtype=jnp.int32))
    cos_t, sin_t = _rope_tables(tm * jnp.arange(t // tm, dtype=jnp.int32))
    rope_p = (cos_r, sin_r, cos_t[:, None, :], sin_t[:, None, :])
    cos_s, sin_s = _rope_tables(past_len + jnp.arange(dec_t, dtype=jnp.int32))

    y_p, y_s = x_prompt, x_sample.reshape(1, n_seq, d)
    outs = [[] for _ in range(8)]
    for l in range(depth):
        w = dict(w_out_p=w_out[l, :pw].astype(BF16), w_out_a=w_out[l, pw:].astype(BF16),
                 g_mix_post=g_mix_post[l].reshape(1, d), g_ffn_pre=g_ffn_pre[l].reshape(1, d),
                 g_ffn_post=g_ffn_post[l].reshape(1, d), w_up=w_up[l].astype(BF16), conv_w=conv_w[l],
                 conv_b=conv_b[l].reshape(1, 2 * d_ff), w_down=w_down[l].astype(BF16))
        g_pre = g_mix_pre[l].reshape(1, d)
        w_in_b = w_in[l].astype(BF16)
        pool_w_b = pool_w[l].astype(BF16)
        pscale = pool_scale[l].reshape(1, pw)

        mod_p, mod_s = _mod_call(c_prompt, c_sample, w_ada[l], b_ada[l])
        mod_p, mod_s = mod_p.reshape(b, 1, 6 * d), mod_s.reshape(1, n_seq, 6 * d)

        pool_state_s, q_s, k_s, v_s, pool_s = _pre_sample_call(
            y_s[0], mod_s[0], g_pre, w_in_b, pool_w_b, pscale, cos_s, sin_s, jnp.swapaxes(state_pool[l], 0, 1),
            pw=pw, aw=aw, pos0=past_len)
        tok = lambda a: a.reshape(n_seq, 1, aw)
        sample = (page_table, tok(q_s), tok(k_s), tok(v_s),
                  jnp.transpose(cache_k[l], (0, 2, 3, 1)), jnp.transpose(cache_v[l], (0, 2, 3, 1)))
        n_early = max(1, min(n_seq - 1, int(n_seq * SAMPLE_EARLY_SHARE)))
        k_p, v_p, qt, kb, vt4, pool_p, utail, att_s0 = _pre_prompt_call(
            y_p, mod_p, g_pre, w_in_b, pool_w_b, pscale, rope_p, *sample, pw=pw, aw=aw, first_seq=0, n_seq=n_early)
        att_t, att_s1 = _moba_call(qt, kb.reshape(b, t // MOBA_BLOCK, MOBA_BLOCK, aw), vt4, *sample,
                                   first_seq=n_early, n_seq=n_seq - n_early)
        att_s = jnp.concatenate([att_s0, att_s1], axis=0)
        y_p, ctail = _post_call(y_p, pool_p, att_t, mod_p, w, att_transposed=True)
        y_s, conv_state_s = _post_call(y_s, pool_s.reshape(1, n_seq, pw), att_s.reshape(1, n_seq, aw).astype(BF16),
                                       mod_s, w, att_transposed=False, state=state_conv[l][None])

        outs[0].append(k_p.reshape(b, t, n_heads, HEAD_DIM))
        outs[1].append(v_p.reshape(b, t, n_heads, HEAD_DIM))
        outs[2].append(utail[:, POOL_HALO - POOL_STATE:])
        outs[3].append(ctail[:, SUBLANES - (CONV_WIDTH - 1):])
        outs[4].append(k_s.reshape(n_seq, dec_t, n_heads, HEAD_DIM))
        outs[5].append(v_s.reshape(n_seq, dec_t, n_heads, HEAD_DIM))
        outs[6].append(jnp.swapaxes(pool_state_s, 0, 1))
        outs[7].append(conv_state_s[0])
    stacked = [jnp.stack(o) for o in outs]
    return (y_p, y_s.reshape(n_seq, dec_t, d), *stacked)
```
